```python
import math
import jax
import jax.numpy as jnp
from jax import lax
import numpy as np

D_MODEL = 2048
BATCH = 8
SEQ = 2048
DEPTH = 2

RNN_WIDTH = 1024
RNN_BLOCKS = 8
RNN_BLOCK_W = RNN_WIDTH // RNN_BLOCKS
CONV_WIDTH = 4
LRU_C = 8.0

N_HEADS = 16
N_KV_GROUPS = 4
HEADS_PER_GROUP = N_HEADS // N_KV_GROUPS
HEAD_DIM = 64
ATT_WIDTH = N_HEADS * HEAD_DIM
KV_WIDTH = N_KV_GROUPS * HEAD_DIM
CMP_BLOCK = 32
CMP_STRIDE = 16
CMP_HIDDEN = 128
SEL_BLOCK = 64
SEL_TOP_N = 8
WINDOW = 512
Q_BLOCK = 128

N_BUCKETS = 32
MAX_DISTANCE = 128

N_EXPERTS = 64
EXPERT_TOP_K = 8
N_EXPERT_GROUPS = 8
TOPK_EXPERT_GROUPS = 4
EXPERT_HIDDEN = 512
SHARED_HIDDEN = 512
ROUTED_SCALE = 2.5
MOE_ROW_BLOCK = 128

N_IN = 2 * RNN_WIDTH + ATT_WIDTH + 6 * KV_WIDTH + 3 * N_HEADS + 2 * D_MODEL
EPS = 1e-6
NEG = -1e30

kernel_name = 'hybrid_rglru_nsa_moe_adaln'


def rmsnorm(x, g):
    xf = x.astype(jnp.float32)
    y = xf * lax.rsqrt(jnp.mean(xf * xf, axis=-1, keepdims=True) + EPS)
    return (y * g.astype(jnp.float32)).astype(x.dtype)


def t5_bucket(dist):
    n = jnp.maximum(dist, 0)
    max_exact = N_BUCKETS // 2
    nf = jnp.maximum(n, 1).astype(jnp.float32)
    large = max_exact + (jnp.log(nf / max_exact) / math.log(MAX_DISTANCE / max_exact)
                         * (N_BUCKETS - max_exact)).astype(jnp.int32)
    large = jnp.minimum(large, N_BUCKETS - 1)
    return jnp.where(n < max_exact, n, large)


def masked_softmax(s, mask, axis=-1):
    s = jnp.where(mask, s.astype(jnp.float32), NEG)
    return jax.nn.softmax(s, axis=axis) * mask


def rglru_mixer(u_in, gate_in, conv_w, conv_b, wa, ba, wx, bx, lam):
    b_, s_, r_ = u_in.shape
    u = lax.conv_general_dilated(u_in, conv_w[:, None, :], (1,), [(CONV_WIDTH - 1, 0)],
                                 dimension_numbers=('NWC', 'WIO', 'NWC'),
                                 feature_group_count=r_) + conv_b
    ub = u.reshape(b_, s_, RNN_BLOCKS, RNN_BLOCK_W)
    r = jax.nn.sigmoid((jnp.einsum('bsnh,nhk->bsnk', ub, wa).reshape(b_, s_, r_) + ba).astype(jnp.float32))
    i = jax.nn.sigmoid((jnp.einsum('bsnh,nhk->bsnk', ub, wx).reshape(b_, s_, r_) + bx).astype(jnp.float32))
    log_a = -LRU_C * r * jax.nn.softplus(-lam.astype(jnp.float32))
    a = jnp.exp(log_a)
    b = jnp.sqrt(-jnp.expm1(2.0 * log_a)) * (i * u.astype(jnp.float32))

    def combine(left, right):
        a1, b1 = left
        a2, b2 = right
        return a1 * a2, a2 * b1 + b2

    _, h = lax.associative_scan(combine, (a, b), axis=1)
    return (jax.nn.gelu(gate_in.astype(jnp.float32)) * h).astype(u_in.dtype)


def nsa_mixer(q, k_cmp, v_cmp, k_slc, v_slc, k_win, v_win, gates, rel_bias,
              pe_k, w1_k, w2_k, pe_v, w1_v, w2_v):
    b_, s_ = q.shape[0], q.shape[1]
    g_, hg, dh = N_KV_GROUPS, HEADS_PER_GROUP, HEAD_DIM
    qg = q.reshape(b_, s_, g_, hg, dh) * (HEAD_DIM ** -0.5)
    t_pos = jnp.arange(s_)

    n_cmp = (s_ - CMP_BLOCK) // CMP_STRIDE + 1
    cmp_start = jnp.arange(n_cmp) * CMP_STRIDE
    blk_idx = cmp_start[:, None] + jnp.arange(CMP_BLOCK)

    def compress(kv, pe, w1, w2):
        blocks = kv[:, blk_idx] + pe[None, None, :, None, :]
        hid = jax.nn.gelu(jnp.einsum('bnlgd,ldf->bngf', blocks, w1))
        return jnp.einsum('bngf,fd->bngd', hid, w2)

    kc = compress(k_cmp, pe_k, w1_k, w2_k)
    vc = compress(v_cmp, pe_v, w1_v, w2_v)
    cmp_end = cmp_start + CMP_BLOCK - 1
    cmask = cmp_end[None, :] <= t_pos[:, None]
    cbias = jnp.transpose(rel_bias[t5_bucket(t_pos[:, None] - cmp_end[None, :])], (2, 0, 1))
    cbias = cbias.reshape(g_, hg, s_, n_cmp).astype(jnp.float32)
    s_c = jnp.einsum('bsghd,bngd->bghsn', qg, kc).astype(jnp.float32) + cbias
    p_c = masked_softmax(s_c, cmask)
    o_cmp = jnp.einsum('bghsn,bngd->bsghd', p_c.astype(vc.dtype), vc)

    n_sel = s_ // SEL_BLOCK
    sel_start = jnp.arange(n_sel) * SEL_BLOCK
    overlap = ((cmp_start[:, None] < sel_start[None, :] + SEL_BLOCK)
               & (cmp_start[:, None] + CMP_BLOCK > sel_start[None, :])).astype(jnp.float32)
    imp = jnp.einsum('bghsn,nj->bgsj', p_c, overlap)
    cur = t_pos // SEL_BLOCK
    jj = jnp.arange(n_sel)
    valid = sel_start[None, :] <= t_pos[:, None]
    forced = (jj[None, :] == 0) | (jj[None, :] == cur[:, None]) | (jj[None, :] == cur[:, None] - 1)
    imp = jnp.where(forced, jnp.inf, jnp.where(valid, imp, -jnp.inf))
    top_n = min(SEL_TOP_N, n_sel)
    _, sel_idx = lax.top_k(imp, top_n)

    ks_blocks = jnp.transpose(k_slc.reshape(b_, n_sel, SEL_BLOCK, g_, dh), (0, 3, 1, 2, 4))
    vs_blocks = jnp.transpose(v_slc.reshape(b_, n_sel, SEL_BLOCK, g_, dh), (0, 3, 1, 2, 4))
    k_win_pad = jnp.pad(k_win, ((0, 0), (WINDOW, 0), (0, 0), (0, 0)))
    v_win_pad = jnp.pad(v_win, ((0, 0), (WINDOW, 0), (0, 0), (0, 0)))
    tbl_g = jnp.transpose(rel_bias.reshape(N_BUCKETS, g_, hg), (1, 0, 2)).astype(jnp.float32)

    n_qb = s_ // Q_BLOCK
    q_blocks = jnp.moveaxis(qg.reshape(b_, n_qb, Q_BLOCK, g_, hg, dh), 1, 0)
    idx_blocks = jnp.moveaxis(sel_idx.reshape(b_, g_, n_qb, Q_BLOCK, top_n), 2, 0)
    qi = jnp.arange(Q_BLOCK)
    ki = jnp.arange(Q_BLOCK + WINDOW)
    wbias = jnp.transpose(rel_bias[t5_bucket(qi[:, None] + WINDOW - ki[None, :])], (2, 0, 1))
    wbias = wbias.reshape(g_, hg, Q_BLOCK, Q_BLOCK + WINDOW).astype(jnp.float32)
    band = (ki[None, :] > qi[:, None]) & (ki[None, :] <= qi[:, None] + WINDOW)
    bi = jnp.arange(b_)[:, None, None, None]
    gi = jnp.arange(g_)[None, :, None, None]

    def block_fn(args):
        qb, ib, blk = args
        s0 = blk * Q_BLOCK
        tq = s0 + qi
        ks = ks_blocks[bi, gi, ib]
        vs = vs_blocks[bi, gi, ib]
        pos = ib[..., None] * SEL_BLOCK + jnp.arange(SEL_BLOCK)
        dist = tq[None, None, :, None, None] - pos
        sbias = jnp.moveaxis(tbl_g[gi[..., None], t5_bucket(dist)], -1, 2)
        s_s = jnp.einsum('bqghd,bgqnkd->bghqnk', qb, ks).astype(jnp.float32) + sbias
        p_s = masked_softmax(s_s, (dist >= 0)[:, :, None], axis=(-2, -1))
        o_s = jnp.einsum('bghqnk,bgqnkd->bqghd', p_s.astype(vs.dtype), vs)
        kw = lax.dynamic_slice_in_dim(k_win_pad, s0, Q_BLOCK + WINDOW, axis=1)
        vw = lax.dynamic_slice_in_dim(v_win_pad, s0, Q_BLOCK + WINDOW, axis=1)
        wmask = band & (ki[None, :] >= WINDOW - s0)
        s_w = jnp.einsum('bqghd,bkgd->bghqk', qb, kw).astype(jnp.float32) + wbias
        p_w = masked_softmax(s_w, wmask)
        o_w = jnp.einsum('bghqk,bkgd->bqghd', p_w.astype(vw.dtype), vw)
        return o_s, o_w

    o_slc, o_win = lax.map(block_fn, (q_blocks, idx_blocks, jnp.arange(n_qb)))
    o_slc = jnp.moveaxis(o_slc, 0, 1).reshape(b_, s_, N_HEADS, dh)
    o_win = jnp.moveaxis(o_win, 0, 1).reshape(b_, s_, N_HEADS, dh)
    o_cmp = o_cmp.reshape(b_, s_, N_HEADS, dh)
    g = jax.nn.sigmoid(gates.astype(jnp.float32))
    o = g[..., 0:1] * o_cmp + g[..., 1:2] * o_slc + g[..., 2:3] * o_win
    return o.reshape(b_, s_, ATT_WIDTH).astype(q.dtype)


def moe_ffn(h, router_w, router_bias, w_gate, w_up, w_down, sh_gate, sh_up, sh_down):
    b_, s_, d_ = h.shape
    n_tok = b_ * s_
    hf = h.reshape(n_tok, d_)
    scores = jax.nn.sigmoid((hf @ router_w).astype(jnp.float32))
    biased = scores + router_bias.astype(jnp.float32)
    per_group = N_EXPERTS // N_EXPERT_GROUPS
    group_score = lax.top_k(biased.reshape(n_tok, N_EXPERT_GROUPS, per_group), 2)[0].sum(-1)
    _, group_idx = lax.top_k(group_score, TOPK_EXPERT_GROUPS)
    group_mask = jnp.zeros((n_tok, N_EXPERT_GROUPS), bool).at[jnp.arange(n_tok)[:, None], group_idx].set(True)
    expert_mask = jnp.repeat(group_mask, per_group, axis=1)
    _, expert_idx = lax.top_k(jnp.where(expert_mask, biased, -jnp.inf), EXPERT_TOP_K)
    wts = jnp.take_along_axis(scores, expert_idx, axis=1)
    wts = ROUTED_SCALE * wts / jnp.sum(wts, axis=-1, keepdims=True)
    n_assign = n_tok * EXPERT_TOP_K
    flat_e = expert_idx.reshape(n_assign)
    order = jnp.argsort(flat_e)
    e_sorted = flat_e[order]
    counts = jnp.bincount(flat_e, length=N_EXPERTS)
    padded = (counts + MOE_ROW_BLOCK - 1) // MOE_ROW_BLOCK * MOE_ROW_BLOCK
    padded_end = jnp.cumsum(padded)
    dest = (padded_end - padded)[e_sorted] + jnp.arange(n_assign) - (jnp.cumsum(counts) - counts)[e_sorted]
    n_blocks = -(-n_assign // MOE_ROW_BLOCK) + N_EXPERTS
    n_rows = n_blocks * MOE_ROW_BLOCK
    row_tok = jnp.full((n_rows,), n_tok, jnp.int32).at[dest].set((order // EXPERT_TOP_K).astype(jnp.int32))
    row_wt = jnp.zeros((n_rows,), jnp.float32).at[dest].set(wts.reshape(n_assign)[order])
    block_expert = jnp.minimum(
        jnp.searchsorted(padded_end, jnp.arange(n_blocks) * MOE_ROW_BLOCK, side='right'), N_EXPERTS - 1)
    h_pad = jnp.concatenate([hf, jnp.zeros((1, d_), hf.dtype)], axis=0)

    def expert_block(acc, inp):
        tok, wt, e = inp
        xb = h_pad[tok]
        hb = jax.nn.silu(xb @ w_gate[e]) * (xb @ w_up[e])
        yb = (hb @ w_down[e]).astype(jnp.float32) * wt[:, None]
        return acc.at[tok].add(yb), None

    acc0 = jnp.zeros((n_tok + 1, d_), jnp.float32)
    acc, _ = lax.scan(expert_block, acc0,
                      (row_tok.reshape(n_blocks, MOE_ROW_BLOCK), row_wt.reshape(n_blocks, MOE_ROW_BLOCK), block_expert))
    shared = (jax.nn.silu(hf @ sh_gate) * (hf @ sh_up)) @ sh_down
    return (acc[:n_tok] + shared.astype(jnp.float32)).astype(h.dtype).reshape(b_, s_, d_)


def setup_inputs(seed: int = 0) -> dict:
    key = jax.random.key(seed)
    keys = iter(jax.random.split(key, 48))
    f32 = jnp.float32
    L = DEPTH

    def nrm(shape, fan_in, gain=1.0):
        return gain * (fan_in ** -0.5) * jax.random.normal(next(keys), shape, f32)

    def small(shape, s=0.02):
        return s * jax.random.normal(next(keys), shape, f32)

    x = jax.random.normal(next(keys), (BATCH, SEQ, D_MODEL), f32)
    c = jax.random.normal(next(keys), (BATCH, D_MODEL), f32)
    rel_bias = small((N_BUCKETS, N_HEADS), 0.2)
    final_norm = 1.0 + small((D_MODEL,))
    ada_w = nrm((L, D_MODEL, 6 * D_MODEL), D_MODEL, 0.5)
    ada_b = small((L, 6 * D_MODEL))
    norm_mix = 1.0 + small((L, D_MODEL))
    norm_ffn = 1.0 + small((L, D_MODEL))
    w_in = nrm((L, D_MODEL, N_IN), D_MODEL)
    conv_w = nrm((L, CONV_WIDTH, RNN_WIDTH), CONV_WIDTH)
    conv_b = small((L, RNN_WIDTH))
    lru_wa = nrm((L, RNN_BLOCKS, RNN_BLOCK_W, RNN_BLOCK_W), RNN_BLOCK_W)
    lru_ba = small((L, RNN_WIDTH))
    lru_wx = nrm((L, RNN_BLOCKS, RNN_BLOCK_W, RNN_BLOCK_W), RNN_BLOCK_W)
    lru_bx = small((L, RNN_WIDTH))
    a_c = jax.random.uniform(next(keys), (L, RNN_WIDTH), f32, 0.9, 0.999)
    a = a_c ** (1.0 / LRU_C)
    lru_lambda = jnp.log(a) - jnp.log1p(-a)
    cmp_pe_k = small((L, CMP_BLOCK, HEAD_DIM), 0.1)
    cmp_w1_k = nrm((L, CMP_BLOCK, HEAD_DIM, CMP_HIDDEN), CMP_BLOCK * HEAD_DIM)
    cmp_w2_k = nrm((L, CMP_HIDDEN, HEAD_DIM), CMP_HIDDEN)
    cmp_pe_v = small((L, CMP_BLOCK, HEAD_DIM), 0.1)
    cmp_w1_v = nrm((L, CMP_BLOCK, HEAD_DIM, CMP_HIDDEN), CMP_BLOCK * HEAD_DIM)
    cmp_w2_v = nrm((L, CMP_HIDDEN, HEAD_DIM), CMP_HIDDEN)
    w_up_rnn = nrm((L, RNN_WIDTH, D_MODEL), RNN_WIDTH)
    w_up_att = nrm((L, ATT_WIDTH, D_MODEL), ATT_WIDTH)
    w_out = nrm((L, D_MODEL, D_MODEL), D_MODEL)
    router_w = nrm((L, D_MODEL, N_EXPERTS), D_MODEL)
    router_bias = small((L, N_EXPERTS), 0.01)
    exp_w_gate = nrm((L, N_EXPERTS, D_MODEL, EXPERT_HIDDEN), D_MODEL)
    exp_w_up = nrm((L, N_EXPERTS, D_MODEL, EXPERT_HIDDEN), D_MODEL)
    exp_w_down = nrm((L, N_EXPERTS, EXPERT_HIDDEN, D_MODEL), EXPERT_HIDDEN)
    sh_w_gate = nrm((L, D_MODEL, SHARED_HIDDEN), D_MODEL)
    sh_w_up = nrm((L, D_MODEL, SHARED_HIDDEN), D_MODEL)
    sh_w_down = nrm((L, SHARED_HIDDEN, D_MODEL), SHARED_HIDDEN)
    return {'x': x, 'c': c, 'rel_bias': rel_bias, 'final_norm': final_norm,
            'ada_w': ada_w, 'ada_b': ada_b, 'norm_mix': norm_mix, 'norm_ffn': norm_ffn,
            'w_in': w_in, 'conv_w': conv_w, 'conv_b': conv_b,
            'lru_wa': lru_wa, 'lru_ba': lru_ba, 'lru_wx': lru_wx, 'lru_bx': lru_bx, 'lru_lambda': lru_lambda,
            'cmp_pe_k': cmp_pe_k, 'cmp_w1_k': cmp_w1_k, 'cmp_w2_k': cmp_w2_k,
            'cmp_pe_v': cmp_pe_v, 'cmp_w1_v': cmp_w1_v, 'cmp_w2_v': cmp_w2_v,
            'w_up_rnn': w_up_rnn, 'w_up_att': w_up_att, 'w_out': w_out,
            'router_w': router_w, 'router_bias': router_bias,
            'exp_w_gate': exp_w_gate, 'exp_w_up': exp_w_up, 'exp_w_down': exp_w_down,
            'sh_w_gate': sh_w_gate, 'sh_w_up': sh_w_up, 'sh_w_down': sh_w_down}


def reference(x, c, rel_bias, final_norm, ada_w, ada_b, norm_mix, norm_ffn, w_in, conv_w, conv_b,
              lru_wa, lru_ba, lru_wx, lru_bx, lru_lambda,
              cmp_pe_k, cmp_w1_k, cmp_w2_k, cmp_pe_v, cmp_w1_v, cmp_w2_v,
              w_up_rnn, w_up_att, w_out, router_w, router_bias,
              exp_w_gate, exp_w_up, exp_w_down, sh_w_gate, sh_w_up, sh_w_down):
    b_, s_, _ = x.shape
    c_act = jax.nn.silu(c)
    widths = [RNN_WIDTH, RNN_WIDTH, ATT_WIDTH] + [KV_WIDTH] * 6 + [3 * N_HEADS, D_MODEL]
    splits = np.cumsum(widths).tolist()
    kv_shape = (b_, s_, N_KV_GROUPS, HEAD_DIM)
    for l in range(DEPTH):
        mod = c_act @ ada_w[l] + ada_b[l]
        shift1, scale1, gate1, shift2, scale2, gate2 = jnp.split(mod[:, None, :], 6, axis=-1)
        h = rmsnorm(x, norm_mix[l]) * (1.0 + scale1) + shift1
        z = h @ w_in[l]
        (u_r, g_r, q, k_c, v_c, k_s, v_s, k_w, v_w, g_nsa, m_a, m_b) = jnp.split(z, splits, axis=-1)
        y_rnn = rglru_mixer(u_r, g_r, conv_w[l], conv_b[l], lru_wa[l], lru_ba[l],
                            lru_wx[l], lru_bx[l], lru_lambda[l])
        y_att = nsa_mixer(q.reshape(b_, s_, N_HEADS, HEAD_DIM),
                          k_c.reshape(kv_shape), v_c.reshape(kv_shape),
                          k_s.reshape(kv_shape), v_s.reshape(kv_shape),
                          k_w.reshape(kv_shape), v_w.reshape(kv_shape),
                          g_nsa.reshape(b_, s_, N_HEADS, 3), rel_bias,
                          cmp_pe_k[l], cmp_w1_k[l], cmp_w2_k[l], cmp_pe_v[l], cmp_w1_v[l], cmp_w2_v[l])
        merged = jax.nn.sigmoid(m_a) * (y_rnn @ w_up_rnn[l]) + jax.nn.sigmoid(m_b) * (y_att @ w_up_att[l])
        x = x + gate1 * (merged @ w_out[l])
        h = rmsnorm(x, norm_ffn[l]) * (1.0 + scale2) + shift2
        x = x + gate2 * moe_ffn(h, router_w[l], router_bias[l], exp_w_gate[l], exp_w_up[l], exp_w_down[l],
                                sh_w_gate[l], sh_w_up[l], sh_w_down[l])
    return rmsnorm(x, final_norm)
```

```python
import functools
import math

import numpy as np
import jax
import jax.numpy as jnp
from jax import lax
from jax.experimental import pallas as pl
from jax.experimental.pallas import tpu as pltpu

DEPTH = 2
RNN_WIDTH = 1024
RNN_BLOCKS = 8
CONV_WIDTH = 4
LRU_C = 8.0
N_HEADS = 16
N_KV_GROUPS = 4
HEAD_DIM = 64
CMP_BLOCK = 32
CMP_STRIDE = 16
CMP_HIDDEN = 128
SEL_BLOCK = 64
SEL_TOP_N = 8
WINDOW = 512
N_BUCKETS = 32
MAX_DISTANCE = 128
N_EXPERTS = 64
EXPERT_TOP_K = 8
N_EXPERT_GROUPS = 8
TOPK_EXPERT_GROUPS = 4
ROUTED_SCALE = 2.5
EPS = 1e-6
NEG = -1e30

LANE = 128
SUBLANE = 8
VMEM_LIMIT = 52 * 1024 * 1024

TM_IN = 512
TN_IN = 1024
TN_MOD = 1024
TC_RNN = 256
TQ = 128
TM_MERGE = 512
TN_MERGE = 1024
TM_OUT = 256
TM_ROUTE = 256
TM_DISP = 128
ROW_BLOCK = 256
TM_COMB = 128

F32 = jnp.float32
BF16 = jnp.bfloat16


def _cparams(sem):
    return pltpu.CompilerParams(dimension_semantics=sem, vmem_limit_bytes=VMEM_LIMIT)


def _round_up(a, b):
    return (a + b - 1) // b * b


def _tile(n, pref):
    if n <= pref:
        return n
    t = pref // LANE * LANE
    while n % t:
        t -= LANE
    return t


def _gelu_tanh(x):
    return x * (0.5 * (1.0 + jnp.tanh(math.sqrt(2.0 / math.pi) * (x + 0.044715 * (x * x * x)))))


def _sigmoid(x):
    return jax.nn.sigmoid(x)


def _mod_kernel(c_ref, w_ref, b_ref, o_ref):
    c = c_ref[...]
    ca = (c * _sigmoid(c)).astype(BF16)
    o_ref[...] = jnp.dot(ca, w_ref[...].astype(BF16), preferred_element_type=F32) + b_ref[...]


def _adaln_mod(c, ada_w, ada_b):
    nl, d, n6 = ada_w.shape
    b = c.shape[0]
    tn = _tile(n6, TN_MOD)
    return pl.pallas_call(
        _mod_kernel,
        grid=(nl, n6 // tn),
        in_specs=[pl.BlockSpec((b, d), lambda l, j: (0, 0)),
                  pl.BlockSpec((None, d, tn), lambda l, j: (l, 0, j)),
                  pl.BlockSpec((None, 1, tn), lambda l, j: (l, 0, j))],
        out_specs=pl.BlockSpec((None, b, tn), lambda l, j: (l, 0, j)),
        out_shape=jax.ShapeDtypeStruct((nl, b, n6), F32),
        compiler_params=_cparams(("arbitrary", "arbitrary")),
        name="adaln_mod",
    )(c, ada_w, ada_b.reshape(nl, 1, n6))


def _inproj_kernel(x_ref, nw_ref, sc_ref, sh_ref, w_ref, o_ref, h_ref):
    @pl.when(pl.program_id(1) == 0)
    def _():
        x = x_ref[...]
        ms = jnp.mean(x * x, axis=-1, keepdims=True)
        y = (x * lax.rsqrt(ms + EPS)) * nw_ref[...]
        h_ref[...] = (y * (1.0 + sc_ref[...]) + sh_ref[...]).astype(BF16)

    o_ref[...] = jnp.dot(h_ref[...], w_ref[...], preferred_element_type=F32)


def _inproj(x2d, norm_w, modr, mod_base, seq, w_p):
    t, d = x2d.shape
    n_p = w_p.shape[1]
    tm = min(TM_IN, seq)
    tn = min(TN_IN, n_p)

    def mod_idx(k):
        return lambda i, j: (mod_base + ((i * tm) // seq) * 6 + k, 0, 0)

    return pl.pallas_call(
        _inproj_kernel,
        grid=(t // tm, n_p // tn),
        in_specs=[pl.BlockSpec((tm, d), lambda i, j: (i, 0)),
                  pl.BlockSpec((1, d), lambda i, j: (0, 0)),
                  pl.BlockSpec((None, 1, d), mod_idx(1)),
                  pl.BlockSpec((None, 1, d), mod_idx(0)),
                  pl.BlockSpec((d, tn), lambda i, j: (0, j))],
        out_specs=pl.BlockSpec((tm, tn), lambda i, j: (i, j)),
        out_shape=jax.ShapeDtypeStruct((t, n_p), F32),
        scratch_shapes=[pltpu.VMEM((tm, d), BF16)],
        compiler_params=_cparams(("arbitrary", "arbitrary")),
        name="inproj",
    )(x2d, norm_w.reshape(1, d), modr, modr, w_p)


def _rglru_kernel(u_ref, g_ref, cw_ref, cb_ref, wa_ref, ba_ref, wx_ref, bx_ref, lam_ref, y_ref,
                  ubuf, a_s, b_s, h_s, hcar):
    tc, r = u_ref.shape
    nb = wa_ref.shape[0]
    bw = r // nb

    @pl.when(pl.program_id(1) == 0)
    def _():
        ubuf[0:SUBLANE, :] = jnp.zeros((SUBLANE, r), F32)
        hcar[...] = jnp.zeros_like(hcar)

    ubuf[SUBLANE:SUBLANE + tc, :] = u_ref[...]
    cw = cw_ref[...]
    uc = cb_ref[...] + cw[CONV_WIDTH - 1:CONV_WIDTH, :] * ubuf[SUBLANE:SUBLANE + tc, :]
    for k in range(CONV_WIDTH - 1):
        off = SUBLANE - (CONV_WIDTH - 1) + k
        uc = uc + cw[k:k + 1, :] * ubuf[off:off + tc, :]
    ubuf[0:SUBLANE, :] = ubuf[tc:tc + SUBLANE, :]

    ucb = uc.astype(BF16)
    rp = []
    xp = []
    for n in range(nb):
        blk = ucb[:, n * bw:(n + 1) * bw]
        rp.append(jnp.dot(blk, wa_ref[n], preferred_element_type=F32))
        xp.append(jnp.dot(blk, wx_ref[n], preferred_element_type=F32))
    rg = _sigmoid(jnp.concatenate(rp, axis=1) + ba_ref[...])
    ig = _sigmoid(jnp.concatenate(xp, axis=1) + bx_ref[...])
    nl = -lam_ref[...]
    sp = jnp.maximum(nl, 0.0) + jnp.log1p(jnp.exp(-jnp.abs(nl)))
    log_a = (-LRU_C * rg) * sp
    a_s[...] = jnp.exp(log_a)
    th = jnp.tanh(log_a)
    one_minus_a2 = (-2.0 * th) / (1.0 - th)
    b_s[...] = jnp.sqrt(one_minus_a2) * (ig * uc)

    def step(t, h):
        h = a_s[pl.ds(t, 1), :] * h + b_s[pl.ds(t, 1), :]
        h_s[pl.ds(t, 1), :] = h
        return h

    h_last = lax.fori_loop(0, tc, step, hcar[0:1, :], unroll=8)
    hcar[0:1, :] = h_last
    y_ref[...] = (_gelu_tanh(g_ref[...]) * h_s[...]).astype(y_ref.dtype)


def _rglru(z2d, batch, seq, u_blk, g_blk, conv_w, conv_b, wa, ba, wx, bx, lam):
    r = conv_w.shape[1]
    tc = min(TC_RNN, seq)
    nt = seq // tc
    nb, bw, _ = wa.shape
    row = lambda v: v.reshape(1, r)
    full = lambda shape: pl.BlockSpec(shape, lambda b, t: (0,) * len(shape))
    return pl.pallas_call(
        _rglru_kernel,
        grid=(batch, nt),
        in_specs=[pl.BlockSpec((tc, r), lambda b, t: (b * nt + t, u_blk)),
                  pl.BlockSpec((tc, r), lambda b, t: (b * nt + t, g_blk)),
                  full((CONV_WIDTH, r)), full((1, r)),
                  full((nb, bw, bw)), full((1, r)),
                  full((nb, bw, bw)), full((1, r)), full((1, r))],
        out_specs=pl.BlockSpec((tc, r), lambda b, t: (b * nt + t, 0)),
        out_shape=jax.ShapeDtypeStruct((batch * seq, r), BF16),
        scratch_shapes=[pltpu.VMEM((tc + SUBLANE, r), F32), pltpu.VMEM((tc, r), F32),
                        pltpu.VMEM((tc, r), F32), pltpu.VMEM((tc, r), F32),
                        pltpu.VMEM((SUBLANE, r), F32)],
        compiler_params=_cparams(("arbitrary", "arbitrary")),
        name="rglru",
    )(z2d, z2d, conv_w, row(conv_b), wa.astype(BF16), row(ba), wx.astype(BF16), row(bx), row(lam))


def _compress_kernel(xk_ref, xv_ref, pek_ref, w1k_ref, w2k_ref, pev_ref, w1v_ref, w2v_ref, kc_ref, vc_ref):
    def one(x_ref, pe_ref, w1_ref, w2_ref, o_ref):
        blocks = (x_ref[...] + pe_ref[...]).astype(BF16)
        hid = _gelu_tanh(jnp.dot(blocks, w1_ref[...], preferred_element_type=F32))
        o_ref[...] = jnp.dot(hid.astype(BF16), w2_ref[...], preferred_element_type=F32).astype(o_ref.dtype)

    one(xk_ref, pek_ref, w1k_ref, w2k_ref, kc_ref)
    one(xv_ref, pev_ref, w1v_ref, w2v_ref, vc_ref)


def _compress(xk, xv, pe_k, w1_k, w2_k, pe_v, w1_v, w2_v):
    bg, nc, kd = xk.shape
    dh = w2_k.shape[1]
    hid = w2_k.shape[0]
    x_spec = pl.BlockSpec((None, nc, kd), lambda i: (i, 0, 0))
    full = lambda shape: pl.BlockSpec(shape, lambda i: (0,) * len(shape))
    o_spec = pl.BlockSpec((None, nc, dh), lambda i: (i, 0, 0))
    prep = lambda pe, w1, w2: (pe.reshape(1, kd), w1.reshape(kd, hid).astype(BF16), w2.astype(BF16))
    return pl.pallas_call(
        _compress_kernel,
        grid=(bg,),
        in_specs=[x_spec, x_spec, full((1, kd)), full((kd, hid)), full((hid, dh)),
                  full((1, kd)), full((kd, hid)), full((hid, dh))],
        out_specs=[o_spec, o_spec],
        out_shape=[jax.ShapeDtypeStruct((bg, nc, dh), BF16)] * 2,
        compiler_params=_cparams(("arbitrary",)),
        name="nsa_compress",
    )(xk, xv, *prep(pe_k, w1_k, w2_k), *prep(pe_v, w1_v, w2_v))


def _col_max(x):
    return jnp.max(x, axis=0, keepdims=True)


def _attn_kernel(qT_ref, kc_ref, vcT_ref, ks_ref, kw_ref, vsT_ref, vwT_ref, gT_ref,
                 tdiag_ref, tprev_ref, cfar_ref, cbias_ref, ovl_ref, o_ref,
                 qpad, sel_s, m_s, l_s, acc_s):
    g = pl.program_id(1)
    i = pl.program_id(2)
    hg = N_HEADS // N_KV_GROUPS
    dh = HEAD_DIM
    tq = qT_ref.shape[1]
    nw = hg * tq
    n_sel = sel_s.shape[0]

    qT = qT_ref[...]
    qcat = jnp.concatenate([qT[h * dh:(h + 1) * dh, :] for h in range(hg)], axis=1)
    qcat = (qcat.astype(F32) * (HEAD_DIM ** -0.5)).astype(BF16)
    qpad[...] = jnp.zeros_like(qpad)
    qpad[pl.ds(pl.multiple_of(g * dh, dh), dh), :] = qcat
    q_pad = qpad[...]

    lane = lax.broadcasted_iota(jnp.int32, (tq, nw), 1)
    krow = lax.broadcasted_iota(jnp.int32, (tq, nw), 0)
    qloc = lane % tq
    t_abs = i * tq + qloc

    nc = kc_ref.shape[0]
    sc = jnp.dot(kc_ref[...], qcat, preferred_element_type=F32) + cbias_ref[...]
    nrow = lax.broadcasted_iota(jnp.int32, (nc, nw), 0)
    tc_abs = i * tq + lax.broadcasted_iota(jnp.int32, (nc, nw), 1) % tq
    cmask = (nrow * CMP_STRIDE + (CMP_BLOCK - 1)) <= tc_abs
    sc = jnp.where(cmask, sc, NEG)
    mc = _col_max(sc)
    pc = jnp.where(cmask, jnp.exp(sc - mc), 0.0)
    lc = jnp.sum(pc, axis=0, keepdims=True)
    pc = pc * jnp.where(lc > 0.0, 1.0 / lc, 0.0)
    o_cmp = jnp.dot(vcT_ref[...], pc.astype(BF16), preferred_element_type=F32)

    psum = pc[:, 0:tq]
    for h in range(1, hg):
        psum = psum + pc[:, h * tq:(h + 1) * tq]
    imp = jnp.dot(ovl_ref[...], psum, preferred_element_type=F32, precision=lax.Precision.HIGHEST)
    jrow = lax.broadcasted_iota(jnp.int32, (n_sel, tq), 0)
    tq_abs = i * tq + lax.broadcasted_iota(jnp.int32, (n_sel, tq), 1)
    cur = tq_abs // SEL_BLOCK
    forced = (jrow == 0) | (jrow == cur) | (jrow == cur - 1)
    valid = jrow * SEL_BLOCK <= tq_abs
    work = jnp.where(forced, jnp.inf, jnp.where(valid, imp, -jnp.inf))
    jrow_f = jrow.astype(F32)
    sel = jnp.zeros((n_sel, tq), F32)
    for _ in range(min(SEL_TOP_N, n_sel)):
        mx = _col_max(work)
        first = jnp.min(jnp.where(work == mx, jrow_f, float(n_sel)), axis=0, keepdims=True)
        pick = jrow_f == first
        sel = jnp.where(pick, 1.0, sel)
        work = jnp.where(pick, -jnp.inf, work)
    sel_s[...] = sel

    def start(k_ref, vT_ref, j, bias, mask):
        s = jnp.dot(k_ref[pl.ds(pl.multiple_of(j * tq, tq), tq), :], q_pad, preferred_element_type=F32) + bias
        s = jnp.where(mask, s, NEG)
        m = _col_max(s)
        p = jnp.exp(s - m)
        m_s[...] = m
        l_s[...] = jnp.sum(p, axis=0, keepdims=True)
        acc_s[...] = jnp.dot(vT_ref[:, pl.ds(pl.multiple_of(j * tq, tq), tq)], p.astype(BF16),
                             preferred_element_type=F32)

    def update(k_ref, vT_ref, j, bias, mask):
        s = jnp.dot(k_ref[pl.ds(pl.multiple_of(j * tq, tq), tq), :], q_pad, preferred_element_type=F32) + bias
        if mask is not None:
            s = jnp.where(mask, s, NEG)
        m_old = m_s[...]
        m_new = jnp.maximum(m_old, _col_max(s))
        alpha = jnp.exp(m_old - m_new)
        p = jnp.exp(s - m_new)
        m_s[...] = m_new
        l_s[...] = alpha * l_s[...] + jnp.sum(p, axis=0, keepdims=True)
        acc_s[...] = alpha * acc_s[...] + jnp.dot(
            vT_ref[:, pl.ds(pl.multiple_of(j * tq, tq), tq)], p.astype(BF16), preferred_element_type=F32)

    def sel_mask(j):
        per = tq // SEL_BLOCK
        parts = []
        for c in range(per):
            rowv = sel_s[pl.ds(j * per + c, 1), :]
            rowv = jnp.concatenate([rowv] * hg, axis=1)
            parts.append(jnp.broadcast_to(rowv, (SEL_BLOCK, nw)))
        return jnp.concatenate(parts, axis=0) > 0.5

    causal = krow <= qloc
    start(ks_ref, vsT_ref, i, tdiag_ref[...], causal & sel_mask(i))

    @pl.when(i >= 1)
    def _():
        update(ks_ref, vsT_ref, i - 1, tprev_ref[...], sel_mask(i - 1))

    def far_slc(j, c):
        update(ks_ref, vsT_ref, j, cfar_ref[...], sel_mask(j))
        return c

    lax.fori_loop(0, jnp.maximum(i - 1, 0), far_slc, 0)
    o_slc = acc_s[...] * (1.0 / l_s[...])

    start(kw_ref, vwT_ref, i, tdiag_ref[...], causal)
    nwin = WINDOW // tq
    for d in range(1, nwin + 1):
        bias_ref = tprev_ref if d == 1 else cfar_ref
        msk = (krow > qloc) if d == nwin else None

        @pl.when(i >= d)
        def _(d=d, bias_ref=bias_ref, msk=msk):
            update(kw_ref, vwT_ref, i - d, bias_ref[...], msk)

    o_win = acc_s[...] * (1.0 / l_s[...])

    gate = _sigmoid(gT_ref[...])

    def grow(j):
        return jnp.concatenate([gate[h * 3 + j:h * 3 + j + 1, :] for h in range(hg)], axis=1)

    o_t = grow(0) * o_cmp + grow(1) * o_slc + grow(2) * o_win
    o_hd = jnp.concatenate([o_t[:, h * tq:(h + 1) * tq] for h in range(hg)], axis=0)
    o_ref[...] = o_hd.T.astype(o_ref.dtype)


def _t5_bucket(dist):
    n = jnp.maximum(dist, 0)
    max_exact = N_BUCKETS // 2
    nf = jnp.maximum(n, 1).astype(F32)
    large = max_exact + (jnp.log(nf / max_exact) / math.log(MAX_DISTANCE / max_exact)
                         * (N_BUCKETS - max_exact)).astype(jnp.int32)
    large = jnp.minimum(large, N_BUCKETS - 1)
    return jnp.where(n < max_exact, n, large)


def _bias_tables(rel_bias, seq):
    g_, hg = N_KV_GROUPS, N_HEADS // N_KV_GROUPS
    tq = min(TQ, seq)
    assert MAX_DISTANCE <= tq, "tiles two or more behind the diagonal must all fall in the last bucket"
    tbl = rel_bias.astype(F32).reshape(N_BUCKETS, g_, hg)
    kk = jnp.arange(tq)[:, None]
    qq = jnp.arange(tq)[None, :]

    def toeplitz(offset):
        b = tbl[_t5_bucket(offset + qq - kk)]
        return jnp.transpose(b, (2, 0, 3, 1)).reshape(g_, tq, hg * tq)

    tdiag = toeplitz(0)
    tprev = toeplitz(tq)
    cfar = jnp.broadcast_to(tbl[N_BUCKETS - 1][:, :, None], (g_, hg, tq)).reshape(g_, 1, hg * tq)
    nc = _round_up((seq - CMP_BLOCK) // CMP_STRIDE + 1, LANE)
    cmp_end = jnp.arange(nc) * CMP_STRIDE + CMP_BLOCK - 1
    t_pos = jnp.arange(seq)
    cb = tbl[_t5_bucket(t_pos[None, :] - cmp_end[:, None])]
    cb = jnp.transpose(cb.reshape(nc, seq // tq, tq, g_, hg), (3, 1, 0, 4, 2)).reshape(g_, seq // tq, nc, hg * tq)
    n_sel = seq // SEL_BLOCK
    cmp_start = jnp.arange(nc) * CMP_STRIDE
    sel_start = jnp.arange(n_sel) * SEL_BLOCK
    ovl = ((cmp_start[None, :] < sel_start[:, None] + SEL_BLOCK)
           & (cmp_start[None, :] + CMP_BLOCK > sel_start[:, None])).astype(F32)
    return tdiag, tprev, cfar, cb, ovl


def _attention(qT, kc, vcT, ks, kw, vsT, vwT, gT, tables, batch, seq):
    tdiag, tprev, cfar, cbias, ovl = tables
    g_, hg, dh = N_KV_GROUPS, N_HEADS // N_KV_GROUPS, HEAD_DIM
    tq = min(TQ, seq)
    nqt = seq // tq
    nw = hg * tq
    nc = kc.shape[2]
    n_sel = seq // SEL_BLOCK
    kvw = g_ * dh
    return pl.pallas_call(
        _attn_kernel,
        grid=(batch, g_, nqt),
        in_specs=[pl.BlockSpec((None, hg * dh, tq), lambda b, g, i: (b, g, i)),
                  pl.BlockSpec((None, None, nc, dh), lambda b, g, i: (b, g, 0, 0)),
                  pl.BlockSpec((None, None, dh, nc), lambda b, g, i: (b, g, 0, 0)),
                  pl.BlockSpec((None, seq, kvw), lambda b, g, i: (b, 0, 0)),
                  pl.BlockSpec((None, seq, kvw), lambda b, g, i: (b, 0, 0)),
                  pl.BlockSpec((None, dh, seq), lambda b, g, i: (b, g, 0)),
                  pl.BlockSpec((None, dh, seq), lambda b, g, i: (b, g, 0)),
                  pl.BlockSpec((None, None, 16, tq), lambda b, g, i: (b, g, 0, i)),
                  pl.BlockSpec((None, tq, nw), lambda b, g, i: (g, 0, 0)),
                  pl.BlockSpec((None, tq, nw), lambda b, g, i: (g, 0, 0)),
                  pl.BlockSpec((None, 1, nw), lambda b, g, i: (g, 0, 0)),
                  pl.BlockSpec((None, None, nc, nw), lambda b, g, i: (g, i, 0, 0)),
                  pl.BlockSpec((n_sel, nc), lambda b, g, i: (0, 0))],
        out_specs=pl.BlockSpec((None, tq, hg * dh), lambda b, g, i: (b, i, g)),
        out_shape=jax.ShapeDtypeStruct((batch, seq, N_HEADS * dh), BF16),
        scratch_shapes=[pltpu.VMEM((kvw, nw), BF16), pltpu.VMEM((n_sel, tq), F32),
                        pltpu.VMEM((1, nw), F32), pltpu.VMEM((1, nw), F32), pltpu.VMEM((dh, nw), F32)],
        compiler_params=_cparams(("arbitrary", "arbitrary", "arbitrary")),
        name="nsa_attention",
    )(qT, kc, vcT, ks, kw, vsT, vwT, gT, tdiag, tprev, cfar, cbias, ovl)


def _merge_kernel(yr_ref, ya_ref, wr_ref, wa_ref, ma_ref, mb_ref, o_ref):
    pr = jnp.dot(yr_ref[...], wr_ref[...], preferred_element_type=F32)
    pa = jnp.dot(ya_ref[...], wa_ref[...], preferred_element_type=F32)
    o_ref[...] = (_sigmoid(ma_ref[...]) * pr + _sigmoid(mb_ref[...]) * pa).astype(o_ref.dtype)


def _merge(y_rnn, y_att, w_ur, w_ua, z2d, ma_blk, mb_blk):
    t, r = y_rnn.shape
    a = y_att.shape[1]
    d = w_ur.shape[1]
    tm = min(TM_MERGE, t)
    tn = min(TN_MERGE, d)
    nj = d // tn
    return pl.pallas_call(
        _merge_kernel,
        grid=(t // tm, nj),
        in_specs=[pl.BlockSpec((tm, r), lambda i, j: (i, 0)),
                  pl.BlockSpec((tm, a), lambda i, j: (i, 0)),
                  pl.BlockSpec((r, tn), lambda i, j: (0, j)),
                  pl.BlockSpec((a, tn), lambda i, j: (0, j)),
                  pl.BlockSpec((tm, tn), lambda i, j: (i, ma_blk * nj + j)),
                  pl.BlockSpec((tm, tn), lambda i, j: (i, mb_blk * nj + j))],
        out_specs=pl.BlockSpec((tm, tn), lambda i, j: (i, j)),
        out_shape=jax.ShapeDtypeStruct((t, d), BF16),
        compiler_params=_cparams(("arbitrary", "arbitrary")),
        name="merge",
    )(y_rnn, y_att, w_ur, w_ua, z2d, z2d)


def _outproj_kernel(mg_ref, w_ref, x_ref, g1_ref, nw_ref, sc_ref, sh_ref, x1_ref, h2_ref):
    x1 = x_ref[...] + g1_ref[...] * jnp.dot(mg_ref[...], w_ref[...], preferred_element_type=F32)
    x1_ref[...] = x1
    ms = jnp.mean(x1 * x1, axis=-1, keepdims=True)
    y = (x1 * lax.rsqrt(ms + EPS)) * nw_ref[...]
    h2_ref[...] = y * (1.0 + sc_ref[...]) + sh_ref[...]


def _outproj(merged, w_out, x2d, modr, mod_base, seq, norm_w):
    t, d = x2d.shape
    tm = min(TM_OUT, seq)

    def mod_idx(k):
        return lambda i: (mod_base + ((i * tm) // seq) * 6 + k, 0, 0)

    row_spec = pl.BlockSpec((tm, d), lambda i: (i, 0))
    return pl.pallas_call(
        _outproj_kernel,
        grid=(t // tm,),
        in_specs=[row_spec, pl.BlockSpec((d, d), lambda i: (0, 0)), row_spec,
                  pl.BlockSpec((None, 1, d), mod_idx(2)),
                  pl.BlockSpec((1, d), lambda i: (0, 0)),
                  pl.BlockSpec((None, 1, d), mod_idx(4)),
                  pl.BlockSpec((None, 1, d), mod_idx(3))],
        out_specs=[row_spec, row_spec],
        out_shape=[jax.ShapeDtypeStruct((t, d), F32)] * 2,
        compiler_params=_cparams(("arbitrary",)),
        name="outproj",
    )(merged, w_out, x2d, modr, norm_w.reshape(1, d), modr, modr)


def _router_kernel(h_ref, rw_ref, rb_ref, idx_ref, wt_ref, rank_ref, cnt_ref, carry):
    ne = rw_ref.shape[0]
    tm = h_ref.shape[0]
    per = ne // N_EXPERT_GROUPS
    assert per == SUBLANE, "one expert group per sublane tile"

    @pl.when(pl.program_id(0) == 0)
    def _():
        carry[...] = jnp.zeros_like(carry)

    logits = lax.dot_general(rw_ref[...], h_ref[...].astype(BF16), (((1,), (1,)), ((), ())),
                             preferred_element_type=F32)
    scores = _sigmoid(logits)
    biased = scores + rb_ref[...]
    erow = lax.broadcasted_iota(jnp.int32, (ne, tm), 0).astype(F32)
    grow = lax.broadcasted_iota(jnp.int32, (ne, tm), 0) // per

    gparts = []
    sub = lax.broadcasted_iota(jnp.int32, (per, tm), 0).astype(F32)
    for gi in range(N_EXPERT_GROUPS):
        xg = biased[gi * per:(gi + 1) * per, :]
        m1 = _col_max(xg)
        f1 = jnp.min(jnp.where(xg == m1, sub, float(per)), axis=0, keepdims=True)
        m2 = _col_max(jnp.where(sub == f1, -jnp.inf, xg))
        gparts.append(jnp.broadcast_to(m1 + m2, (per, tm)))
    gscore = jnp.concatenate(gparts, axis=0)

    kparts = []
    for gi in range(N_EXPERT_GROUPS):
        gs = gscore[gi * per:gi * per + 1, :]
        beats = (gscore > gs) | ((gscore == gs) & (grow < gi))
        nbeat = jnp.sum(beats.astype(F32), axis=0, keepdims=True)
        kparts.append(jnp.broadcast_to(nbeat < float(TOPK_EXPERT_GROUPS * per), (per, tm)))
    gkeep = jnp.concatenate(kparts, axis=0)

    work = jnp.where(gkeep, biased, -jnp.inf)
    picks = []
    chosen = jnp.zeros((ne, tm), F32)
    for _ in range(EXPERT_TOP_K):
        mx = _col_max(work)
        first = jnp.min(jnp.where(work == mx, erow, float(ne)), axis=0, keepdims=True)
        pick = erow == first
        picks.append(pick)
        chosen = jnp.where(pick, 1.0, chosen)
        work = jnp.where(pick, -jnp.inf, work)

    tri = (lax.broadcasted_iota(jnp.int32, (tm, tm), 0) < lax.broadcasted_iota(jnp.int32, (tm, tm), 1))
    before = jnp.dot(chosen.astype(BF16), tri.astype(BF16), preferred_element_type=F32)
    pos = before + carry[:, 0:1]
    new_carry = carry[:, 0:1] + jnp.sum(chosen, axis=1, keepdims=True)
    carry[...] = jnp.broadcast_to(new_carry, carry.shape)
    cnt_ref[...] = carry[...]

    krow = lax.broadcasted_iota(jnp.int32, (EXPERT_TOP_K, tm), 0)
    idx_o = jnp.zeros((EXPERT_TOP_K, tm), F32)
    wt_o = jnp.zeros((EXPERT_TOP_K, tm), F32)
    rk_o = jnp.zeros((EXPERT_TOP_K, tm), F32)
    for k, pick in enumerate(picks):
        sel = lambda v: jnp.sum(jnp.where(pick, v, 0.0), axis=0, keepdims=True)
        idx_o = jnp.where(krow == k, sel(erow), idx_o)
        wt_o = jnp.where(krow == k, sel(scores), wt_o)
        rk_o = jnp.where(krow == k, sel(pos), rk_o)
    wsum = jnp.sum(wt_o, axis=0, keepdims=True)
    idx_ref[...] = idx_o.astype(jnp.int32)
    wt_ref[...] = (ROUTED_SCALE * wt_o) / wsum
    rank_ref[...] = rk_o.astype(jnp.int32)


def _router(h2, router_w, router_bias):
    t, d = h2.shape
    ne = router_w.shape[1]
    tm = min(TM_ROUTE, t)
    k_spec = pl.BlockSpec((EXPERT_TOP_K, tm), lambda i: (0, i))
    return pl.pallas_call(
        _router_kernel,
        grid=(t // tm,),
        in_specs=[pl.BlockSpec((tm, d), lambda i: (i, 0)),
                  pl.BlockSpec((ne, d), lambda i: (0, 0)),
                  pl.BlockSpec((ne, 1), lambda i: (0, 0))],
        out_specs=[k_spec, k_spec, k_spec, pl.BlockSpec((ne, LANE), lambda i: (0, 0))],
        out_shape=[jax.ShapeDtypeStruct((EXPERT_TOP_K, t), jnp.int32),
                   jax.ShapeDtypeStruct((EXPERT_TOP_K, t), F32),
                   jax.ShapeDtypeStruct((EXPERT_TOP_K, t), jnp.int32),
                   jax.ShapeDtypeStruct((ne, LANE), F32)],
        scratch_shapes=[pltpu.VMEM((ne, LANE), F32)],
        compiler_params=_cparams(("arbitrary",)),
        name="moe_router",
    )(h2, router_w.T.astype(BF16), router_bias.reshape(ne, 1).astype(F32))


def _row_copy(src_ref, src_row, dst_ref, dst_row, sem):
    return pltpu.make_async_copy(src_ref.at[pl.ds(src_row, 1)], dst_ref.at[pl.ds(dst_row, 1)], sem)


def _dispatch_kernel(dest_ref, h_ref, xs_in_ref, xs_ref, sem):
    del xs_in_ref
    tm = h_ref.shape[0]

    def issue(r, c):
        for k in range(EXPERT_TOP_K):
            _row_copy(h_ref, r, xs_ref, dest_ref[k, r], sem).start()
        return c

    lax.fori_loop(0, tm, issue, 0)

    def drain(r, c):
        for k in range(EXPERT_TOP_K):
            _row_copy(h_ref, r, xs_ref, dest_ref[k, r], sem).wait()
        return c

    lax.fori_loop(0, tm, drain, 0)


def _dispatch(dest, h2, n_rows):
    t, d = h2.shape
    tm = min(TM_DISP, t)
    xs0 = jnp.zeros((n_rows, d), h2.dtype)
    return pl.pallas_call(
        _dispatch_kernel,
        grid=(t // tm,),
        in_specs=[pl.BlockSpec((EXPERT_TOP_K, tm), lambda i: (0, i), memory_space=pltpu.SMEM),
                  pl.BlockSpec((tm, d), lambda i: (i, 0)),
                  pl.BlockSpec(memory_space=pl.ANY)],
        out_specs=pl.BlockSpec(memory_space=pl.ANY),
        out_shape=jax.ShapeDtypeStruct((n_rows, d), h2.dtype),
        input_output_aliases={2: 0},
        scratch_shapes=[pltpu.SemaphoreType.DMA(())],
        compiler_params=_cparams(("arbitrary",)),
        name="moe_dispatch",
    )(dest, h2, xs0)


def _mlp_kernel(be_ref, nb_ref, x_ref, wg_ref, wu_ref, wd_ref, y_ref):
    del be_ref

    @pl.when(pl.program_id(0) < nb_ref[0])
    def _():
        x = x_ref[...].astype(BF16)
        gt = jnp.dot(x, wg_ref[...], preferred_element_type=F32)
        up = jnp.dot(x, wu_ref[...], preferred_element_type=F32)
        hb = ((gt * _sigmoid(gt)) * up).astype(BF16)
        y_ref[...] = jnp.dot(hb, wd_ref[...], preferred_element_type=F32)

    @pl.when(pl.program_id(0) >= nb_ref[0])
    def _():
        y_ref[...] = jnp.zeros_like(y_ref)


def _grouped_mlp(block_expert, n_used, xs, w_gate, w_up, w_down, name):
    n_rows, d = xs.shape
    hid = w_gate.shape[2]
    rb = min(ROW_BLOCK, n_rows)
    row_spec = pl.BlockSpec((rb, d), lambda i, be, nb: (i, 0))
    return pl.pallas_call(
        _mlp_kernel,
        grid_spec=pltpu.PrefetchScalarGridSpec(
            num_scalar_prefetch=2,
            grid=(n_rows // rb,),
            in_specs=[row_spec,
                      pl.BlockSpec((None, d, hid), lambda i, be, nb: (be[i], 0, 0)),
                      pl.BlockSpec((None, d, hid), lambda i, be, nb: (be[i], 0, 0)),
                      pl.BlockSpec((None, hid, d), lambda i, be, nb: (be[i], 0, 0))],
            out_specs=row_spec),
        out_shape=jax.ShapeDtypeStruct((n_rows, d), F32),
        compiler_params=_cparams(("arbitrary",)),
        name=name,
    )(block_expert, n_used, xs, w_gate, w_up, w_down)


def _combine_kernel(dest_ref, wt_ref, ysh_ref, x_ref, g2_ref, fn_ref, ys_ref, o_ref, ybuf, sem, *, final):
    tm, d = x_ref.shape

    def issue(r, c):
        for k in range(EXPERT_TOP_K):
            _row_copy(ys_ref, dest_ref[k, r], ybuf.at[k], r, sem).start()
        return c

    lax.fori_loop(0, tm, issue, 0)

    def drain(r, c):
        for k in range(EXPERT_TOP_K):
            _row_copy(ys_ref, dest_ref[k, r], ybuf.at[k], r, sem).wait()
        return c

    lax.fori_loop(0, tm, drain, 0)

    wt = wt_ref[...]
    acc = ysh_ref[...]
    for k in range(EXPERT_TOP_K):
        acc = acc + wt[:, k:k + 1] * ybuf[k]
    xn = x_ref[...] + g2_ref[...] * acc
    if final:
        ms = jnp.mean(xn * xn, axis=-1, keepdims=True)
        xn = (xn * lax.rsqrt(ms + EPS)) * fn_ref[...]
    o_ref[...] = xn


def _combine(dest, wts_t, y_sorted, y_shared, x2d, modr, mod_base, seq, final_norm, final):
    t, d = x2d.shape
    tm = min(TM_COMB, seq)
    row_spec = pl.BlockSpec((tm, d), lambda i: (i, 0))
    return pl.pallas_call(
        functools.partial(_combine_kernel, final=final),
        grid=(t // tm,),
        in_specs=[pl.BlockSpec((EXPERT_TOP_K, tm), lambda i: (0, i), memory_space=pltpu.SMEM),
                  pl.BlockSpec((tm, EXPERT_TOP_K), lambda i: (i, 0)),
                  row_spec, row_spec,
                  pl.BlockSpec((None, 1, d), lambda i: (mod_base + ((i * tm) // seq) * 6 + 5, 0, 0)),
                  pl.BlockSpec((1, d), lambda i: (0, 0)),
                  pl.BlockSpec(memory_space=pl.ANY)],
        out_specs=row_spec,
        out_shape=jax.ShapeDtypeStruct((t, d), F32),
        scratch_shapes=[pltpu.VMEM((EXPERT_TOP_K, tm, d), F32), pltpu.SemaphoreType.DMA(())],
        compiler_params=_cparams(("arbitrary",)),
        name="moe_combine",
    )(dest, wts_t, y_shared, x2d, modr, final_norm.reshape(1, d), y_sorted)


def _layout(d_model):
    r, a, kvw = RNN_WIDTH, N_HEADS * HEAD_DIM, N_KV_GROUPS * HEAD_DIM
    off = {}
    off["ma"], off["mb"] = 0, d_model
    off["u"] = 2 * d_model
    off["g"] = off["u"] + r
    off["q"] = off["g"] + r
    off["kv"] = off["q"] + a
    off["gn"] = off["kv"] + 6 * kvw
    off["np"] = _round_up(off["gn"] + LANE, TN_IN)
    return off


def _pack_w_in(w_in_l, d_model):
    r, a, kvw = RNN_WIDTH, N_HEADS * HEAD_DIM, N_KV_GROUPS * HEAD_DIM
    off = _layout(d_model)
    s_u, s_g, s_q = 0, r, 2 * r
    s_kv = s_q + a
    s_gn = s_kv + 6 * kvw
    s_ma = s_gn + 3 * N_HEADS
    s_mb = s_ma + d_model
    cols = [w_in_l[:, s_ma:s_mb], w_in_l[:, s_mb:s_mb + d_model], w_in_l[:, s_u:s_g], w_in_l[:, s_g:s_q],
            w_in_l[:, s_q:s_kv], w_in_l[:, s_kv:s_gn], w_in_l[:, s_gn:s_ma]]
    w = jnp.concatenate(cols, axis=1)
    return jnp.pad(w, ((0, 0), (0, off["np"] - w.shape[1]))).astype(BF16)


def _moe_plan(idx, rank, counts, n_tok):
    cnt = counts[:, 0].astype(jnp.int32)
    padded = (cnt + ROW_BLOCK - 1) // ROW_BLOCK * ROW_BLOCK
    ends = jnp.cumsum(padded)
    starts = ends - padded
    dest = starts[idx] + rank
    n_blocks = (n_tok * EXPERT_TOP_K) // ROW_BLOCK + N_EXPERTS
    blk_start = jnp.arange(n_blocks, dtype=jnp.int32) * ROW_BLOCK
    block_expert = jnp.minimum(jnp.searchsorted(ends, blk_start, side="right"), N_EXPERTS - 1).astype(jnp.int32)
    n_used = (ends[-1] // ROW_BLOCK).astype(jnp.int32).reshape(1)
    return dest.astype(jnp.int32), block_expert, n_used, n_blocks * ROW_BLOCK


def kernel(x, c, rel_bias, final_norm, ada_w, ada_b, norm_mix, norm_ffn, w_in, conv_w, conv_b, lru_wa, lru_ba, lru_wx, lru_bx, lru_lambda, cmp_pe_k, cmp_w1_k, cmp_w2_k, cmp_pe_v, cmp_w1_v, cmp_w2_v, w_up_rnn, w_up_att, w_out, router_w, router_bias, exp_w_gate, exp_w_up, exp_w_down, sh_w_gate, sh_w_up, sh_w_down):
    batch, seq, d = x.shape
    n_tok = batch * seq
    depth = ada_w.shape[0]
    g_, hg, dh = N_KV_GROUPS, N_HEADS // N_KV_GROUPS, HEAD_DIM
    a_w, kvw, r = N_HEADS * HEAD_DIM, N_KV_GROUPS * HEAD_DIM, RNN_WIDTH
    off = _layout(d)
    assert seq % TQ == 0 or seq < TQ

    mod = _adaln_mod(c, ada_w, ada_b)
    modr = mod.reshape(depth * batch * 6, 1, d)
    tables = _bias_tables(rel_bias, seq)
    nc = tables[3].shape[2]
    n_cmp = (seq - CMP_BLOCK) // CMP_STRIDE + 1
    blk_idx = (jnp.arange(nc) * CMP_STRIDE)[:, None] + jnp.arange(CMP_BLOCK)[None, :]
    blk_ok = (jnp.arange(nc) < n_cmp)[:, None] & (blk_idx < seq)
    blk_idx = jnp.minimum(blk_idx, seq - 1)

    x2d = x.reshape(n_tok, d)
    for l in range(depth):
        mod_base = l * batch * 6
        z = _inproj(x2d, norm_mix[l], modr, mod_base, seq, _pack_w_in(w_in[l], d))
        y_rnn = _rglru(z, batch, seq, off["u"] // r, off["g"] // r, conv_w[l], conv_b[l],
                       lru_wa[l], lru_ba[l], lru_wx[l], lru_bx[l], lru_lambda[l])

        z3 = z.reshape(batch, seq, -1)
        kv = lambda n: z3[:, :, off["kv"] + n * kvw: off["kv"] + (n + 1) * kvw]

        def unfold(v):
            blocks = jnp.where(blk_ok[None, :, :, None], v[:, blk_idx], 0.0)
            blocks = blocks.reshape(batch, nc, CMP_BLOCK, g_, dh)
            return jnp.transpose(blocks, (0, 3, 1, 2, 4)).reshape(batch * g_, nc, CMP_BLOCK * dh)

        kc, vc = _compress(unfold(kv(0)), unfold(kv(1)), cmp_pe_k[l], cmp_w1_k[l], cmp_w2_k[l],
                           cmp_pe_v[l], cmp_w1_v[l], cmp_w2_v[l])
        kc = kc.reshape(batch, g_, nc, dh)
        vcT = jnp.swapaxes(vc.reshape(batch, g_, nc, dh), 2, 3)
        qT = jnp.swapaxes(z3[:, :, off["q"]:off["q"] + a_w], 1, 2).astype(BF16)
        ks, kw = kv(2).astype(BF16), kv(4).astype(BF16)
        vsT = jnp.swapaxes(kv(3), 1, 2).astype(BF16)
        vwT = jnp.swapaxes(kv(5), 1, 2).astype(BF16)
        gts = z3[:, :, off["gn"]:off["gn"] + 3 * N_HEADS].reshape(batch, seq, g_, hg * 3)
        gT = jnp.pad(jnp.transpose(gts, (0, 2, 3, 1)), ((0, 0), (0, 0), (0, 16 - hg * 3), (0, 0)))
        y_att = _attention(qT, kc, vcT, ks, kw, vsT, vwT, gT, tables, batch, seq).reshape(n_tok, a_w)

        merged = _merge(y_rnn, y_att, w_up_rnn[l].astype(BF16), w_up_att[l].astype(BF16), z,
                        off["ma"] // d, off["mb"] // d)
        x1, h2 = _outproj(merged, w_out[l].astype(BF16), x2d, modr, mod_base, seq, norm_ffn[l])

        idx, wts, rank, counts = _router(h2, router_w[l], router_bias[l])
        dest, block_expert, n_used, n_rows = _moe_plan(idx, rank, counts, n_tok)
        xs = _dispatch(dest, h2, n_rows)
        y_sorted = _grouped_mlp(block_expert, n_used, xs, exp_w_gate[l].astype(BF16), exp_w_up[l].astype(BF16),
                                exp_w_down[l].astype(BF16), "moe_experts")
        sh_blocks = n_tok // min(ROW_BLOCK, n_tok)
        y_shared = _grouped_mlp(jnp.zeros((sh_blocks,), jnp.int32), jnp.full((1,), sh_blocks, jnp.int32), h2,
                                sh_w_gate[l][None].astype(BF16), sh_w_up[l][None].astype(BF16),
                                sh_w_down[l][None].astype(BF16), "moe_shared")
        x2d = _combine(dest, wts.T, y_sorted, y_shared, x1, modr, mod_base, seq, final_norm, l == depth - 1)
    return x2d.reshape(batch, seq, d)
```

```python
import functools
import math

import numpy as np
import jax
import jax.numpy as jnp
from jax import lax
from jax.experimental import pallas as pl
from jax.experimental.pallas import tpu as pltpu

DEPTH = 2
RNN_WIDTH = 1024
RNN_BLOCKS = 8
CONV_WIDTH = 4
LRU_C = 8.0
N_HEADS = 16
N_KV_GROUPS = 4
HEAD_DIM = 64
CMP_BLOCK = 32
CMP_STRIDE = 16
CMP_HIDDEN = 128
SEL_BLOCK = 64
SEL_TOP_N = 8
WINDOW = 512
N_BUCKETS = 32
MAX_DISTANCE = 128
N_EXPERTS = 64
EXPERT_TOP_K = 8
N_EXPERT_GROUPS = 8
TOPK_EXPERT_GROUPS = 4
ROUTED_SCALE = 2.5
EPS = 1e-6
NEG = -1e30

LANE = 128
SUBLANE = 8
VMEM_LIMIT = 52 * 1024 * 1024
DMA_PRIORITIES = 2

TM_IN = 1024
TN_IN = 1024
TN_MOD = 1024
TC_RNN = 256
TQ = 128
FAR_TILES = 4
TM_MERGE = 512
TN_MERGE = 1024
TM_OUT = 256
TM_ROUTE = 256
TM_DISP = 128
ROW_BLOCK = 256
TM_COMB = 128

F32 = jnp.float32
BF16 = jnp.bfloat16


def _cparams(sem):
    return pltpu.CompilerParams(dimension_semantics=sem, vmem_limit_bytes=VMEM_LIMIT)


def _round_up(a, b):
    return (a + b - 1) // b * b


def _tile(n, pref):
    if n <= pref:
        return n
    t = pref // LANE * LANE
    while n % t:
        t -= LANE
    return t


def _gelu_tanh(x):
    return x * (0.5 * (1.0 + jnp.tanh(math.sqrt(2.0 / math.pi) * (x + 0.044715 * (x * x * x)))))


def _sigmoid(x):
    return jax.nn.sigmoid(x)


def _mod_kernel(c_ref, w_ref, b_ref, o_ref):
    c = c_ref[...]
    ca = (c * _sigmoid(c)).astype(BF16)
    o_ref[...] = jnp.dot(ca, w_ref[...].astype(BF16), preferred_element_type=F32) + b_ref[...]


def _adaln_mod(c, ada_w, ada_b):
    nl, d, n6 = ada_w.shape
    b = c.shape[0]
    tn = _tile(n6, TN_MOD)
    return pl.pallas_call(
        _mod_kernel,
        grid=(nl, n6 // tn),
        in_specs=[pl.BlockSpec((b, d), lambda l, j: (0, 0)),
                  pl.BlockSpec((None, d, tn), lambda l, j: (l, 0, j)),
                  pl.BlockSpec((None, 1, tn), lambda l, j: (l, 0, j))],
        out_specs=pl.BlockSpec((None, b, tn), lambda l, j: (l, 0, j)),
        out_shape=jax.ShapeDtypeStruct((nl, b, n6), F32),
        compiler_params=_cparams(("arbitrary", "arbitrary")),
        name="adaln_mod",
    )(c, ada_w, ada_b.reshape(nl, 1, n6))


def _inproj_kernel(x_ref, nw_ref, sc_ref, sh_ref, w_ref, o_ref, h_ref):
    @pl.when(pl.program_id(1) == 0)
    def _():
        x = x_ref[...]
        ms = jnp.mean(x * x, axis=-1, keepdims=True)
        y = (x * lax.rsqrt(ms + EPS)) * nw_ref[...]
        h_ref[...] = (y * (1.0 + sc_ref[...]) + sh_ref[...]).astype(BF16)

    o_ref[...] = jnp.dot(h_ref[...], w_ref[...], preferred_element_type=F32)


def _inproj(x2d, norm_w, modr, mod_base, seq, w_p):
    t, d = x2d.shape
    n_p = w_p.shape[1]
    tm = min(TM_IN, seq)
    tn = min(TN_IN, n_p)

    def mod_idx(k):
        return lambda i, j: (mod_base + ((i * tm) // seq) * 6 + k, 0, 0)

    return pl.pallas_call(
        _inproj_kernel,
        grid=(t // tm, n_p // tn),
        in_specs=[pl.BlockSpec((tm, d), lambda i, j: (i, 0)),
                  pl.BlockSpec((1, d), lambda i, j: (0, 0)),
                  pl.BlockSpec((None, 1, d), mod_idx(1)),
                  pl.BlockSpec((None, 1, d), mod_idx(0)),
                  pl.BlockSpec((d, tn), lambda i, j: (0, j))],
        out_specs=pl.BlockSpec((tm, tn), lambda i, j: (i, j)),
        out_shape=jax.ShapeDtypeStruct((t, n_p), F32),
        scratch_shapes=[pltpu.VMEM((tm, d), BF16)],
        compiler_params=_cparams(("arbitrary", "arbitrary")),
        name="inproj",
    )(x2d, norm_w.reshape(1, d), modr, modr, w_p)


def _rglru_kernel(u_ref, g_ref, cw_ref, cb_ref, wa_ref, ba_ref, wx_ref, bx_ref, lam_ref, y_ref,
                  ubuf, a_s, b_s, h_s, hcar):
    tc, r = u_ref.shape
    nb = wa_ref.shape[0]
    bw = r // nb

    @pl.when(pl.program_id(1) == 0)
    def _():
        ubuf[0:SUBLANE, :] = jnp.zeros((SUBLANE, r), F32)
        hcar[...] = jnp.zeros_like(hcar)

    ubuf[SUBLANE:SUBLANE + tc, :] = u_ref[...]
    cw = cw_ref[...]
    uc = cb_ref[...] + cw[CONV_WIDTH - 1:CONV_WIDTH, :] * ubuf[SUBLANE:SUBLANE + tc, :]
    for k in range(CONV_WIDTH - 1):
        off = SUBLANE - (CONV_WIDTH - 1) + k
        uc = uc + cw[k:k + 1, :] * ubuf[off:off + tc, :]
    ubuf[0:SUBLANE, :] = ubuf[tc:tc + SUBLANE, :]

    ucb = uc.astype(BF16)
    rp = []
    xp = []
    for n in range(nb):
        blk = ucb[:, n * bw:(n + 1) * bw]
        rp.append(jnp.dot(blk, wa_ref[n], preferred_element_type=F32))
        xp.append(jnp.dot(blk, wx_ref[n], preferred_element_type=F32))
    rg = _sigmoid(jnp.concatenate(rp, axis=1) + ba_ref[...])
    ig = _sigmoid(jnp.concatenate(xp, axis=1) + bx_ref[...])
    nl = -lam_ref[...]
    sp = jnp.maximum(nl, 0.0) + jnp.log1p(jnp.exp(-jnp.abs(nl)))
    log_a = (-LRU_C * rg) * sp
    a_s[...] = jnp.exp(log_a)
    th = jnp.tanh(log_a)
    one_minus_a2 = (-2.0 * th) / (1.0 - th)
    b_s[...] = jnp.sqrt(one_minus_a2) * (ig * uc)

    def step(t, h):
        h = a_s[pl.ds(t, 1), :] * h + b_s[pl.ds(t, 1), :]
        h_s[pl.ds(t, 1), :] = h
        return h

    h_last = lax.fori_loop(0, tc, step, hcar[0:1, :], unroll=8)
    hcar[0:1, :] = h_last
    y_ref[...] = (_gelu_tanh(g_ref[...]) * h_s[...]).astype(y_ref.dtype)


def _rglru(z2d, batch, seq, u_blk, g_blk, conv_w, conv_b, wa, ba, wx, bx, lam):
    r = conv_w.shape[1]
    tc = min(TC_RNN, seq)
    nt = seq // tc
    nb, bw, _ = wa.shape
    row = lambda v: v.reshape(1, r)
    full = lambda shape: pl.BlockSpec(shape, lambda b, t: (0,) * len(shape))
    return pl.pallas_call(
        _rglru_kernel,
        grid=(batch, nt),
        in_specs=[pl.BlockSpec((tc, r), lambda b, t: (b * nt + t, u_blk)),
                  pl.BlockSpec((tc, r), lambda b, t: (b * nt + t, g_blk)),
                  full((CONV_WIDTH, r)), full((1, r)),
                  full((nb, bw, bw)), full((1, r)),
                  full((nb, bw, bw)), full((1, r)), full((1, r))],
        out_specs=pl.BlockSpec((tc, r), lambda b, t: (b * nt + t, 0)),
        out_shape=jax.ShapeDtypeStruct((batch * seq, r), BF16),
        scratch_shapes=[pltpu.VMEM((tc + SUBLANE, r), F32), pltpu.VMEM((tc, r), F32),
                        pltpu.VMEM((tc, r), F32), pltpu.VMEM((tc, r), F32),
                        pltpu.VMEM((SUBLANE, r), F32)],
        compiler_params=_cparams(("arbitrary", "arbitrary")),
        name="rglru",
    )(z2d, z2d, conv_w, row(conv_b), wa.astype(BF16), row(ba), wx.astype(BF16), row(bx), row(lam))


def _compress_kernel(xk_ref, xv_ref, pek_ref, w1k_ref, w2k_ref, pev_ref, w1v_ref, w2v_ref, kc_ref, vc_ref):
    def one(x_ref, pe_ref, w1_ref, w2_ref, o_ref):
        blocks = (x_ref[...] + pe_ref[...]).astype(BF16)
        hid = _gelu_tanh(jnp.dot(blocks, w1_ref[...], preferred_element_type=F32))
        o_ref[...] = jnp.dot(hid.astype(BF16), w2_ref[...], preferred_element_type=F32).astype(o_ref.dtype)

    one(xk_ref, pek_ref, w1k_ref, w2k_ref, kc_ref)
    one(xv_ref, pev_ref, w1v_ref, w2v_ref, vc_ref)


def _compress(xk, xv, pe_k, w1_k, w2_k, pe_v, w1_v, w2_v):
    bg, nc, kd = xk.shape
    dh = w2_k.shape[1]
    hid = w2_k.shape[0]
    x_spec = pl.BlockSpec((None, nc, kd), lambda i: (i, 0, 0))
    full = lambda shape: pl.BlockSpec(shape, lambda i: (0,) * len(shape))
    o_spec = pl.BlockSpec((None, nc, dh), lambda i: (i, 0, 0))
    prep = lambda pe, w1, w2: (pe.reshape(1, kd), w1.reshape(kd, hid).astype(BF16), w2.astype(BF16))
    return pl.pallas_call(
        _compress_kernel,
        grid=(bg,),
        in_specs=[x_spec, x_spec, full((1, kd)), full((kd, hid)), full((hid, dh)),
                  full((1, kd)), full((kd, hid)), full((hid, dh))],
        out_specs=[o_spec, o_spec],
        out_shape=[jax.ShapeDtypeStruct((bg, nc, dh), BF16)] * 2,
        compiler_params=_cparams(("arbitrary",)),
        name="nsa_compress",
    )(xk, xv, *prep(pe_k, w1_k, w2_k), *prep(pe_v, w1_v, w2_v))


def _col_max(x):
    return jnp.max(x, axis=0, keepdims=True)


def _attn_kernel(qT_ref, kc_ref, vcT_ref, ks_ref, kw_ref, vsT_ref, vwT_ref, gT_ref,
                 tnear_ref, cfar_ref, cbias_ref, ovl_ref, o_ref,
                 qbd, sel_s, m_s, l_s, acc_s):
    i = pl.program_id(1)
    g_, hg, dh = N_KV_GROUPS, N_HEADS // N_KV_GROUPS, HEAD_DIM
    tq = qT_ref.shape[1]
    gw = hg * tq
    nw = g_ * gw
    n_sel = sel_s.shape[0]
    per = tq // SEL_BLOCK

    qT = (qT_ref[...].astype(F32) * (HEAD_DIM ** -0.5)).astype(BF16)
    zero_blk = jnp.zeros((dh, gw), BF16)
    rows = []
    for g in range(g_):
        qcat = jnp.concatenate([qT[(g * hg + h) * dh:(g * hg + h + 1) * dh, :] for h in range(hg)], axis=1)
        rows.append(jnp.concatenate([zero_blk] * g + [qcat] + [zero_blk] * (g_ - 1 - g), axis=1))
    qbd[...] = jnp.concatenate(rows, axis=0)

    def lane_q(shape):
        return lax.broadcasted_iota(jnp.int32, shape, 1) % tq

    def pv(vT_ref, start, nrows, p):
        pb = p.astype(BF16)
        outs = []
        for g in range(g_):
            if start is None:
                v = vT_ref[g * dh:(g + 1) * dh, :]
            else:
                v = vT_ref[g * dh:(g + 1) * dh, pl.ds(start, nrows)]
            outs.append(jnp.dot(v, pb[:, g * gw:(g + 1) * gw], preferred_element_type=F32))
        return jnp.concatenate(outs, axis=1)

    nc = kc_ref.shape[0]
    sc = jnp.dot(kc_ref[...], qbd[...], preferred_element_type=F32) + cbias_ref[...]
    nrow = lax.broadcasted_iota(jnp.int32, (nc, nw), 0)
    cmask = (nrow * CMP_STRIDE + (CMP_BLOCK - 1)) <= i * tq + lane_q((nc, nw))
    sc = jnp.where(cmask, sc, NEG)
    pc = jnp.where(cmask, jnp.exp(sc - _col_max(sc)), 0.0)
    lc = jnp.sum(pc, axis=0, keepdims=True)
    pc = pc * jnp.where(lc > 0.0, 1.0 / lc, 0.0)
    o_cmp = pv(vcT_ref, None, nc, pc)

    psum = jnp.concatenate(
        [sum(pc[:, g * gw + h * tq:g * gw + (h + 1) * tq] for h in range(hg)) for g in range(g_)], axis=1)
    imp = jnp.dot(ovl_ref[...], psum, preferred_element_type=F32, precision=lax.Precision.HIGHEST)
    jrow = lax.broadcasted_iota(jnp.int32, (n_sel, g_ * tq), 0)
    tq_abs = i * tq + lane_q((n_sel, g_ * tq))
    cur = tq_abs // SEL_BLOCK
    forced = (jrow == 0) | (jrow == cur) | (jrow == cur - 1)
    valid = jrow * SEL_BLOCK <= tq_abs
    work = jnp.where(forced, jnp.inf, jnp.where(valid, imp, -jnp.inf))
    jrow_f = jrow.astype(F32)
    sel = jnp.zeros((n_sel, g_ * tq), F32)
    for _ in range(min(SEL_TOP_N, n_sel)):
        mx = _col_max(work)
        first = jnp.min(jnp.where(work == mx, jrow_f, float(n_sel)), axis=0, keepdims=True)
        pick = jrow_f == first
        sel = jnp.where(pick, 1.0, sel)
        work = jnp.where(pick, -jnp.inf, work)
    sel_s[...] = jnp.where(sel > 0.5, 0.0, NEG)

    def sel_add(first_blk, n_blk):
        parts = []
        for c in range(n_blk):
            rowv = sel_s[pl.ds(first_blk + c, 1), :]
            rowv = jnp.concatenate([rowv[:, g * tq:(g + 1) * tq] for g in range(g_) for _ in range(hg)], axis=1)
            parts.append(jnp.broadcast_to(rowv, (SEL_BLOCK, nw)))
        return jnp.concatenate(parts, axis=0)

    def scores(k_ref, start, nrows, bias):
        return jnp.dot(k_ref[pl.ds(start, nrows), :], qbd[...], preferred_element_type=F32) + bias

    def flash_init(s, vT_ref, start, nrows):
        m = _col_max(s)
        p = jnp.exp(s - m)
        m_s[...] = m
        l_s[...] = jnp.sum(p, axis=0, keepdims=True)
        acc_s[...] = pv(vT_ref, start, nrows, p)

    def flash_update(s, vT_ref, start, nrows):
        m_old = m_s[...]
        m_new = jnp.maximum(m_old, _col_max(s))
        alpha = jnp.exp(m_old - m_new)
        p = jnp.exp(s - m_new)
        m_s[...] = m_new
        l_s[...] = alpha * l_s[...] + jnp.sum(p, axis=0, keepdims=True)
        acc_s[...] = alpha * acc_s[...] + pv(vT_ref, start, nrows, p)

    def near(k_ref, vT_ref, use_sel):
        @pl.when(i == 0)
        def _():
            krow = lax.broadcasted_iota(jnp.int32, (tq, nw), 0)
            add = jnp.where(krow <= lane_q((tq, nw)), 0.0, NEG)
            if use_sel:
                add = add + sel_add(0, per)
            flash_init(scores(k_ref, 0, tq, tnear_ref[tq:2 * tq, :]) + add, vT_ref, 0, tq)

        @pl.when(i > 0)
        def _():
            start = pl.multiple_of((i - 1) * tq, tq)
            krow = lax.broadcasted_iota(jnp.int32, (2 * tq, nw), 0)
            add = jnp.where(krow - tq <= lane_q((2 * tq, nw)), 0.0, NEG)
            if use_sel:
                add = add + sel_add((i - 1) * per, 2 * per)
            flash_init(scores(k_ref, start, 2 * tq, tnear_ref[...]) + add, vT_ref, start, 2 * tq)

    far_end = jnp.maximum(i - 1, 0)

    near(ks_ref, vsT_ref, True)
    fr = FAR_TILES * tq

    def far_slc(c, carry):
        hi_t = far_end - c * FAR_TILES
        st_t = jnp.maximum(hi_t - FAR_TILES, 0)
        start = pl.multiple_of(st_t * tq, tq)
        key_abs = start + lax.broadcasted_iota(jnp.int32, (fr, nw), 0)
        add = jnp.where(key_abs < hi_t * tq, 0.0, NEG) + sel_add(st_t * per, FAR_TILES * per)
        flash_update(scores(ks_ref, start, fr, cfar_ref[...]) + add, vsT_ref, start, fr)
        return carry

    lax.fori_loop(0, (far_end + FAR_TILES - 1) // FAR_TILES, far_slc, 0)
    o_slc = acc_s[...] * (1.0 / l_s[...])

    near(kw_ref, vwT_ref, False)
    wr = WINDOW - tq
    st_t = jnp.maximum(i - WINDOW // tq, 0)
    start = pl.multiple_of(st_t * tq, tq)
    key_abs = start + lax.broadcasted_iota(jnp.int32, (wr, nw), 0)
    keep = (key_abs < far_end * tq) & (key_abs > i * tq + lane_q((wr, nw)) - WINDOW)
    flash_update(scores(kw_ref, start, wr, cfar_ref[...]) + jnp.where(keep, 0.0, NEG), vwT_ref, start, wr)
    o_win = acc_s[...] * (1.0 / l_s[...])

    gate = _sigmoid(gT_ref[...])

    def grow(j):
        return jnp.concatenate([gate[h * 3 + j:h * 3 + j + 1, :] for h in range(N_HEADS)], axis=1)

    o_t = grow(0) * o_cmp + grow(1) * o_slc + grow(2) * o_win
    o_hd = jnp.concatenate([o_t[:, h * tq:(h + 1) * tq] for h in range(N_HEADS)], axis=0)
    o_ref[...] = o_hd.T.astype(o_ref.dtype)


def _t5_bucket_np(dist):
    n = np.maximum(dist, 0)
    max_exact = N_BUCKETS // 2
    nf = np.maximum(n, 1).astype(np.float64)
    large = max_exact + (np.log(nf / max_exact) / math.log(MAX_DISTANCE / max_exact)
                         * (N_BUCKETS - max_exact)).astype(np.int64)
    large = np.minimum(large, N_BUCKETS - 1)
    return np.where(n < max_exact, n, large).astype(np.int32)


def _bias_tables(rel_bias, seq):
    tq = min(TQ, seq)
    assert MAX_DISTANCE <= tq, "tiles two or more behind the diagonal must all fall in the last bucket"
    nh = N_HEADS
    nc = _round_up((seq - CMP_BLOCK) // CMP_STRIDE + 1, LANE)
    off_max = (nc - 1) * CMP_STRIDE + CMP_BLOCK - 1
    fd = rel_bias.astype(F32)[_t5_bucket_np(np.arange(-off_max, seq))]
    fdT = fd.T
    kk = np.arange(tq)[:, None]
    qq = np.arange(tq)[None, :]

    def toeplitz(offset):
        idx = off_max + np.maximum(offset + qq - kk, 0)
        return jnp.transpose(fd[idx], (0, 2, 1)).reshape(tq, nh * tq)

    tnear = jnp.concatenate([toeplitz(tq), toeplitz(0)], axis=0)
    cfar = jnp.broadcast_to(rel_bias.astype(F32)[N_BUCKETS - 1][:, None], (nh, tq)).reshape(1, nh * tq)
    rows = [lax.slice_in_dim(fdT, off_max - (n * CMP_STRIDE + CMP_BLOCK - 1),
                             off_max - (n * CMP_STRIDE + CMP_BLOCK - 1) + seq, axis=1) for n in range(nc)]
    cb = jnp.stack(rows, axis=0).reshape(nc, nh, seq // tq, tq)
    cbias = jnp.transpose(cb, (2, 0, 1, 3)).reshape(seq // tq, nc, nh * tq)
    n_sel = seq // SEL_BLOCK
    cmp_start = np.arange(nc) * CMP_STRIDE
    sel_start = np.arange(n_sel) * SEL_BLOCK
    ovl = ((cmp_start[None, :] < sel_start[:, None] + SEL_BLOCK)
           & (cmp_start[None, :] + CMP_BLOCK > sel_start[:, None])).astype(np.float32)
    return tnear, cfar, cbias, jnp.asarray(ovl)


def _attention(qT, kc, vcT, ks, kw, vsT, vwT, gT, tables, batch, seq):
    tnear, cfar, cbias, ovl = tables
    g_, dh = N_KV_GROUPS, HEAD_DIM
    tq = min(TQ, seq)
    assert seq % tq == 0 and seq >= FAR_TILES * tq and WINDOW % tq == 0 and tq % SEL_BLOCK == 0
    nqt = seq // tq
    nw = N_HEADS * tq
    nc = kc.shape[1]
    n_sel = seq // SEL_BLOCK
    kvw = g_ * dh
    per_b = lambda shape: pl.BlockSpec((None,) + shape, lambda b, i: (b,) + (0,) * len(shape))
    full = lambda shape: pl.BlockSpec(shape, lambda b, i: (0,) * len(shape))
    return pl.pallas_call(
        _attn_kernel,
        grid=(batch, nqt),
        in_specs=[pl.BlockSpec((None, N_HEADS * dh, tq), lambda b, i: (b, 0, i)),
                  per_b((nc, kvw)),
                  per_b((kvw, nc)),
                  per_b((seq, kvw)), per_b((seq, kvw)),
                  per_b((kvw, seq)), per_b((kvw, seq)),
                  pl.BlockSpec((None, 3 * N_HEADS, tq), lambda b, i: (b, 0, i)),
                  full((2 * tq, nw)), full((1, nw)),
                  pl.BlockSpec((None, nc, nw), lambda b, i: (i, 0, 0)),
                  full((n_sel, nc))],
        out_specs=pl.BlockSpec((None, tq, N_HEADS * dh), lambda b, i: (b, i, 0)),
        out_shape=jax.ShapeDtypeStruct((batch, seq, N_HEADS * dh), BF16),
        scratch_shapes=[pltpu.VMEM((kvw, nw), BF16), pltpu.VMEM((n_sel, g_ * tq), F32),
                        pltpu.VMEM((1, nw), F32), pltpu.VMEM((1, nw), F32), pltpu.VMEM((dh, nw), F32)],
        compiler_params=_cparams(("arbitrary", "arbitrary")),
        name="nsa_attention",
    )(qT, kc, vcT, ks, kw, vsT, vwT, gT, tnear, cfar, cbias, ovl)


def _merge_kernel(yr_ref, ya_ref, wr_ref, wa_ref, ma_ref, mb_ref, o_ref):
    pr = jnp.dot(yr_ref[...], wr_ref[...], preferred_element_type=F32)
    pa = jnp.dot(ya_ref[...], wa_ref[...], preferred_element_type=F32)
    o_ref[...] = (_sigmoid(ma_ref[...]) * pr + _sigmoid(mb_ref[...]) * pa).astype(o_ref.dtype)


def _merge(y_rnn, y_att, w_ur, w_ua, z2d, ma_blk, mb_blk):
    t, r = y_rnn.shape
    a = y_att.shape[1]
    d = w_ur.shape[1]
    tm = min(TM_MERGE, t)
    tn = min(TN_MERGE, d)
    nj = d // tn
    return pl.pallas_call(
        _merge_kernel,
        grid=(t // tm, nj),
        in_specs=[pl.BlockSpec((tm, r), lambda i, j: (i, 0)),
                  pl.BlockSpec((tm, a), lambda i, j: (i, 0)),
                  pl.BlockSpec((r, tn), lambda i, j: (0, j)),
                  pl.BlockSpec((a, tn), lambda i, j: (0, j)),
                  pl.BlockSpec((tm, tn), lambda i, j: (i, ma_blk * nj + j)),
                  pl.BlockSpec((tm, tn), lambda i, j: (i, mb_blk * nj + j))],
        out_specs=pl.BlockSpec((tm, tn), lambda i, j: (i, j)),
        out_shape=jax.ShapeDtypeStruct((t, d), BF16),
        compiler_params=_cparams(("arbitrary", "arbitrary")),
        name="merge",
    )(y_rnn, y_att, w_ur, w_ua, z2d, z2d)


def _outproj_kernel(mg_ref, w_ref, x_ref, g1_ref, nw_ref, sc_ref, sh_ref, x1_ref, h2_ref):
    x1 = x_ref[...] + g1_ref[...] * jnp.dot(mg_ref[...], w_ref[...], preferred_element_type=F32)
    x1_ref[...] = x1
    ms = jnp.mean(x1 * x1, axis=-1, keepdims=True)
    y = (x1 * lax.rsqrt(ms + EPS)) * nw_ref[...]
    h2_ref[...] = y * (1.0 + sc_ref[...]) + sh_ref[...]


def _outproj(merged, w_out, x2d, modr, mod_base, seq, norm_w):
    t, d = x2d.shape
    tm = min(TM_OUT, seq)

    def mod_idx(k):
        return lambda i: (mod_base + ((i * tm) // seq) * 6 + k, 0, 0)

    row_spec = pl.BlockSpec((tm, d), lambda i: (i, 0))
    return pl.pallas_call(
        _outproj_kernel,
        grid=(t // tm,),
        in_specs=[row_spec, pl.BlockSpec((d, d), lambda i: (0, 0)), row_spec,
                  pl.BlockSpec((None, 1, d), mod_idx(2)),
                  pl.BlockSpec((1, d), lambda i: (0, 0)),
                  pl.BlockSpec((None, 1, d), mod_idx(4)),
                  pl.BlockSpec((None, 1, d), mod_idx(3))],
        out_specs=[row_spec, row_spec],
        out_shape=[jax.ShapeDtypeStruct((t, d), F32)] * 2,
        compiler_params=_cparams(("arbitrary",)),
        name="outproj",
    )(merged, w_out, x2d, modr, norm_w.reshape(1, d), modr, modr)


def _router_kernel(h_ref, rw_ref, rb_ref, idx_ref, wt_ref, rank_ref, cnt_ref, carry):
    ne = rw_ref.shape[0]
    tm = h_ref.shape[0]
    per = ne // N_EXPERT_GROUPS
    assert per == SUBLANE, "one expert group per sublane tile"

    @pl.when(pl.program_id(0) == 0)
    def _():
        carry[...] = jnp.zeros_like(carry)

    logits = lax.dot_general(rw_ref[...], h_ref[...].astype(BF16), (((1,), (1,)), ((), ())),
                             preferred_element_type=F32)
    scores = _sigmoid(logits)
    biased = scores + rb_ref[...]
    erow = lax.broadcasted_iota(jnp.int32, (ne, tm), 0).astype(F32)
    grow = lax.broadcasted_iota(jnp.int32, (ne, tm), 0) // per

    gparts = []
    sub = lax.broadcasted_iota(jnp.int32, (per, tm), 0).astype(F32)
    for gi in range(N_EXPERT_GROUPS):
        xg = biased[gi * per:(gi + 1) * per, :]
        m1 = _col_max(xg)
        f1 = jnp.min(jnp.where(xg == m1, sub, float(per)), axis=0, keepdims=True)
        m2 = _col_max(jnp.where(sub == f1, -jnp.inf, xg))
        gparts.append(jnp.broadcast_to(m1 + m2, (per, tm)))
    gscore = jnp.concatenate(gparts, axis=0)

    kparts = []
    for gi in range(N_EXPERT_GROUPS):
        gs = gscore[gi * per:gi * per + 1, :]
        beats = (gscore > gs) | ((gscore == gs) & (grow < gi))
        nbeat = jnp.sum(beats.astype(F32), axis=0, keepdims=True)
        kparts.append(jnp.broadcast_to(nbeat < float(TOPK_EXPERT_GROUPS * per), (per, tm)))
    gkeep = jnp.concatenate(kparts, axis=0)

    work = jnp.where(gkeep, biased, -jnp.inf)
    picks = []
    chosen = jnp.zeros((ne, tm), F32)
    for _ in range(EXPERT_TOP_K):
        mx = _col_max(work)
        first = jnp.min(jnp.where(work == mx, erow, float(ne)), axis=0, keepdims=True)
        pick = erow == first
        picks.append(pick)
        chosen = jnp.where(pick, 1.0, chosen)
        work = jnp.where(pick, -jnp.inf, work)

    tri = (lax.broadcasted_iota(jnp.int32, (tm, tm), 0) < lax.broadcasted_iota(jnp.int32, (tm, tm), 1))
    before = jnp.dot(chosen.astype(BF16), tri.astype(BF16), preferred_element_type=F32)
    pos = before + carry[:, 0:1]
    new_carry = carry[:, 0:1] + jnp.sum(chosen, axis=1, keepdims=True)
    carry[...] = jnp.broadcast_to(new_carry, carry.shape)
    cnt_ref[...] = carry[...]

    krow = lax.broadcasted_iota(jnp.int32, (EXPERT_TOP_K, tm), 0)
    idx_o = jnp.zeros((EXPERT_TOP_K, tm), F32)
    wt_o = jnp.zeros((EXPERT_TOP_K, tm), F32)
    rk_o = jnp.zeros((EXPERT_TOP_K, tm), F32)
    for k, pick in enumerate(picks):
        sel = lambda v: jnp.sum(jnp.where(pick, v, 0.0), axis=0, keepdims=True)
        idx_o = jnp.where(krow == k, sel(erow), idx_o)
        wt_o = jnp.where(krow == k, sel(scores), wt_o)
        rk_o = jnp.where(krow == k, sel(pos), rk_o)
    wsum = jnp.sum(wt_o, axis=0, keepdims=True)
    idx_ref[...] = idx_o.astype(jnp.int32)
    wt_ref[...] = (ROUTED_SCALE * wt_o) / wsum
    rank_ref[...] = rk_o.astype(jnp.int32)


def _router(h2, router_w, router_bias):
    t, d = h2.shape
    ne = router_w.shape[1]
    tm = min(TM_ROUTE, t)
    k_spec = pl.BlockSpec((EXPERT_TOP_K, tm), lambda i: (0, i))
    return pl.pallas_call(
        _router_kernel,
        grid=(t // tm,),
        in_specs=[pl.BlockSpec((tm, d), lambda i: (i, 0)),
                  pl.BlockSpec((ne, d), lambda i: (0, 0)),
                  pl.BlockSpec((ne, 1), lambda i: (0, 0))],
        out_specs=[k_spec, k_spec, k_spec, pl.BlockSpec((ne, LANE), lambda i: (0, 0))],
        out_shape=[jax.ShapeDtypeStruct((EXPERT_TOP_K, t), jnp.int32),
                   jax.ShapeDtypeStruct((EXPERT_TOP_K, t), F32),
                   jax.ShapeDtypeStruct((EXPERT_TOP_K, t), jnp.int32),
                   jax.ShapeDtypeStruct((ne, LANE), F32)],
        scratch_shapes=[pltpu.VMEM((ne, LANE), F32)],
        compiler_params=_cparams(("arbitrary",)),
        name="moe_router",
    )(h2, router_w.T.astype(BF16), router_bias.reshape(ne, 1).astype(F32))


def _row_copy(src_ref, src_row, dst_ref, dst_row, sem):
    return pltpu.make_async_copy(src_ref.at[pl.ds(src_row, 1)], dst_ref.at[pl.ds(dst_row, 1)], sem)


def _dispatch_kernel(start_ref, cnt_ref, idx_ref, rank_ref, h_ref, xs_ref, zblk, sem):
    tm = h_ref.shape[0]
    ne = start_ref.shape[0] - 1
    rb = zblk.shape[0]
    n_blocks = xs_ref.shape[0] // rb

    def pad_rows(e, fn):
        lo = start_ref[e] + cnt_ref[e]
        lax.fori_loop(lo, start_ref[e + 1], lambda r, c: fn(_row_copy(zblk, 0, xs_ref, r, sem)) or c, 0)

    def tail_blocks(fn):
        def body(b, c):
            fn(pltpu.make_async_copy(zblk, xs_ref.at[pl.ds(pl.multiple_of(b * rb, rb), rb)], sem))
            return c
        lax.fori_loop(start_ref[ne] // rb, n_blocks, body, 0)

    @pl.when(pl.program_id(0) == 0)
    def _():
        zblk[...] = jnp.zeros_like(zblk)
        lax.fori_loop(0, ne, lambda e, c: pad_rows(e, lambda cp: cp.start()) or c, 0)
        tail_blocks(lambda cp: cp.start())
        lax.fori_loop(0, ne, lambda e, c: pad_rows(e, lambda cp: cp.wait()) or c, 0)
        tail_blocks(lambda cp: cp.wait())

    def token_rows(fn):
        def body(r, c):
            for k in range(EXPERT_TOP_K):
                fn(_row_copy(h_ref, r, xs_ref, start_ref[idx_ref[k, r]] + rank_ref[k, r], sem), k)
            return c
        lax.fori_loop(0, tm, body, 0)

    token_rows(lambda cp, k: cp.start(priority=k % DMA_PRIORITIES))
    token_rows(lambda cp, k: cp.wait())


def _dispatch(starts, cnt, idx, rank, h2, n_rows):
    t, d = h2.shape
    tm = min(TM_DISP, t)
    smem_k = pl.BlockSpec((EXPERT_TOP_K, tm), lambda i, *_: (0, i), memory_space=pltpu.SMEM)
    return pl.pallas_call(
        _dispatch_kernel,
        grid_spec=pltpu.PrefetchScalarGridSpec(
            num_scalar_prefetch=2,
            grid=(t // tm,),
            in_specs=[smem_k, smem_k, pl.BlockSpec((tm, d), lambda i, *_: (i, 0))],
            out_specs=pl.BlockSpec(memory_space=pl.ANY),
            scratch_shapes=[pltpu.VMEM((min(ROW_BLOCK, n_rows), d), h2.dtype), pltpu.SemaphoreType.DMA(())]),
        out_shape=jax.ShapeDtypeStruct((n_rows, d), h2.dtype),
        compiler_params=_cparams(("arbitrary",)),
        name="moe_dispatch",
    )(starts, cnt, idx, rank, h2)


def _mlp_kernel(be_ref, nb_ref, x_ref, wg_ref, wu_ref, wd_ref, y_ref, wg_s, wu_s, wd_s):
    i = pl.program_id(0)

    @pl.when((i == 0) | (be_ref[i] != be_ref[jnp.maximum(i - 1, 0)]))
    def _():
        wg_s[...] = wg_ref[...].astype(BF16)
        wu_s[...] = wu_ref[...].astype(BF16)
        wd_s[...] = wd_ref[...].astype(BF16)

    @pl.when(i < nb_ref[0])
    def _():
        x = x_ref[...].astype(BF16)
        gt = jnp.dot(x, wg_s[...], preferred_element_type=F32)
        up = jnp.dot(x, wu_s[...], preferred_element_type=F32)
        hb = ((gt * _sigmoid(gt)) * up).astype(BF16)
        y_ref[...] = jnp.dot(hb, wd_s[...], preferred_element_type=F32)

    @pl.when(i >= nb_ref[0])
    def _():
        y_ref[...] = jnp.zeros_like(y_ref)


def _grouped_mlp(block_expert, n_used, xs, w_gate, w_up, w_down, name):
    n_rows, d = xs.shape
    hid = w_gate.shape[2]
    rb = min(ROW_BLOCK, n_rows)
    return pl.pallas_call(
        _mlp_kernel,
        grid_spec=pltpu.PrefetchScalarGridSpec(
            num_scalar_prefetch=2,
            grid=(n_rows // rb,),
            in_specs=[pl.BlockSpec((rb, d), lambda i, be, nb: (jnp.minimum(i, nb[0] - 1), 0)),
                      pl.BlockSpec((None, d, hid), lambda i, be, nb: (be[i], 0, 0)),
                      pl.BlockSpec((None, d, hid), lambda i, be, nb: (be[i], 0, 0)),
                      pl.BlockSpec((None, hid, d), lambda i, be, nb: (be[i], 0, 0))],
            out_specs=pl.BlockSpec((rb, d), lambda i, be, nb: (i, 0)),
            scratch_shapes=[pltpu.VMEM((d, hid), BF16), pltpu.VMEM((d, hid), BF16), pltpu.VMEM((hid, d), BF16)]),
        out_shape=jax.ShapeDtypeStruct((n_rows, d), F32),
        compiler_params=_cparams(("arbitrary",)),
        name=name,
    )(block_expert, n_used, xs, w_gate, w_up, w_down)


def _combine_kernel(start_ref, idx_ref, rank_ref, wt_ref, ysh_ref, x_ref, g2_ref, fn_ref, ys_ref, o_ref,
                    ybuf, sem, *, final):
    tm, d = x_ref.shape

    def token_rows(fn):
        def body(r, c):
            for k in range(EXPERT_TOP_K):
                fn(_row_copy(ys_ref, start_ref[idx_ref[k, r]] + rank_ref[k, r], ybuf.at[k], r, sem), k)
            return c
        lax.fori_loop(0, tm, body, 0)

    token_rows(lambda cp, k: cp.start(priority=k % DMA_PRIORITIES))
    token_rows(lambda cp, k: cp.wait())

    wt = wt_ref[...]
    acc = ysh_ref[...]
    for k in range(EXPERT_TOP_K):
        acc = acc + wt[:, k:k + 1] * ybuf[k]
    xn = x_ref[...] + g2_ref[...] * acc
    if final:
        ms = jnp.mean(xn * xn, axis=-1, keepdims=True)
        xn = (xn * lax.rsqrt(ms + EPS)) * fn_ref[...]
    o_ref[...] = xn


def _combine(starts, idx, rank, wts_t, y_sorted, y_shared, x2d, modr, mod_base, seq, final_norm, final):
    t, d = x2d.shape
    tm = min(TM_COMB, seq)
    row_spec = pl.BlockSpec((tm, d), lambda i, *_: (i, 0))
    smem_k = pl.BlockSpec((EXPERT_TOP_K, tm), lambda i, *_: (0, i), memory_space=pltpu.SMEM)
    return pl.pallas_call(
        functools.partial(_combine_kernel, final=final),
        grid_spec=pltpu.PrefetchScalarGridSpec(
            num_scalar_prefetch=1,
            grid=(t // tm,),
            in_specs=[smem_k, smem_k,
                      pl.BlockSpec((tm, EXPERT_TOP_K), lambda i, *_: (i, 0)),
                      row_spec, row_spec,
                      pl.BlockSpec((None, 1, d), lambda i, *_: (mod_base + ((i * tm) // seq) * 6 + 5, 0, 0)),
                      pl.BlockSpec((1, d), lambda i, *_: (0, 0)),
                      pl.BlockSpec(memory_space=pl.ANY)],
            out_specs=row_spec,
            scratch_shapes=[pltpu.VMEM((EXPERT_TOP_K, tm, d), F32), pltpu.SemaphoreType.DMA(())]),
        out_shape=jax.ShapeDtypeStruct((t, d), F32),
        compiler_params=_cparams(("arbitrary",)),
        name="moe_combine",
    )(starts, idx, rank, wts_t, y_shared, x2d, modr, final_norm.reshape(1, d), y_sorted)


def _layout(d_model):
    r, a, kvw = RNN_WIDTH, N_HEADS * HEAD_DIM, N_KV_GROUPS * HEAD_DIM
    off = {}
    off["ma"], off["mb"] = 0, d_model
    off["u"] = 2 * d_model
    off["g"] = off["u"] + r
    off["q"] = off["g"] + r
    off["kv"] = off["q"] + a
    off["gn"] = off["kv"] + 6 * kvw
    off["np"] = _round_up(off["gn"] + LANE, TN_IN)
    return off


def _pack_w_in(w_in_l, d_model):
    r, a, kvw = RNN_WIDTH, N_HEADS * HEAD_DIM, N_KV_GROUPS * HEAD_DIM
    off = _layout(d_model)
    s_u, s_g, s_q = 0, r, 2 * r
    s_kv = s_q + a
    s_gn = s_kv + 6 * kvw
    s_ma = s_gn + 3 * N_HEADS
    s_mb = s_ma + d_model
    cols = [w_in_l[:, s_ma:s_mb], w_in_l[:, s_mb:s_mb + d_model], w_in_l[:, s_u:s_g], w_in_l[:, s_g:s_q],
            w_in_l[:, s_q:s_kv], w_in_l[:, s_kv:s_gn], w_in_l[:, s_gn:s_ma]]
    w = jnp.concatenate(cols, axis=1)
    return jnp.pad(w, ((0, 0), (0, off["np"] - w.shape[1]))).astype(BF16)


def _moe_plan(counts, n_tok):
    cnt = counts[:, 0].astype(jnp.int32)
    padded = (cnt + ROW_BLOCK - 1) // ROW_BLOCK * ROW_BLOCK
    ends = jnp.cumsum(padded)
    starts = jnp.concatenate([jnp.zeros((1,), jnp.int32), ends]).astype(jnp.int32)
    n_blocks = (n_tok * EXPERT_TOP_K) // ROW_BLOCK + N_EXPERTS
    blk_start = jnp.arange(n_blocks, dtype=jnp.int32) * ROW_BLOCK
    owner = jnp.sum((ends[None, :] <= blk_start[:, None]).astype(jnp.int32), axis=1)
    block_expert = jnp.minimum(owner, N_EXPERTS - 1).astype(jnp.int32)
    n_used = (ends[-1] // ROW_BLOCK).astype(jnp.int32).reshape(1)
    return starts, cnt, block_expert, n_used, n_blocks * ROW_BLOCK


def kernel(x, c, rel_bias, final_norm, ada_w, ada_b, norm_mix, norm_ffn, w_in, conv_w, conv_b, lru_wa, lru_ba, lru_wx, lru_bx, lru_lambda, cmp_pe_k, cmp_w1_k, cmp_w2_k, cmp_pe_v, cmp_w1_v, cmp_w2_v, w_up_rnn, w_up_att, w_out, router_w, router_bias, exp_w_gate, exp_w_up, exp_w_down, sh_w_gate, sh_w_up, sh_w_down):
    batch, seq, d = x.shape
    n_tok = batch * seq
    depth = ada_w.shape[0]
    g_, dh = N_KV_GROUPS, HEAD_DIM
    a_w, kvw, r = N_HEADS * HEAD_DIM, N_KV_GROUPS * HEAD_DIM, RNN_WIDTH
    off = _layout(d)
    assert seq % CMP_STRIDE == 0 and CMP_BLOCK == 2 * CMP_STRIDE

    mod = _adaln_mod(c, ada_w, ada_b)
    modr = mod.reshape(depth * batch * 6, 1, d)
    tables = _bias_tables(rel_bias, seq)
    nc = tables[2].shape[1]
    n_chunk = seq // CMP_STRIDE

    x2d = x.reshape(n_tok, d)
    for l in range(depth):
        mod_base = l * batch * 6
        z = _inproj(x2d, norm_mix[l], modr, mod_base, seq, _pack_w_in(w_in[l], d))
        y_rnn = _rglru(z, batch, seq, off["u"] // r, off["g"] // r, conv_w[l], conv_b[l],
                       lru_wa[l], lru_ba[l], lru_wx[l], lru_bx[l], lru_lambda[l])

        z3 = z.reshape(batch, seq, -1)
        kv = lambda n: z3[:, :, off["kv"] + n * kvw: off["kv"] + (n + 1) * kvw]

        def unfold(v):
            ch = jnp.transpose(v.reshape(batch, n_chunk, CMP_STRIDE, g_, dh), (0, 3, 1, 2, 4))
            ch = ch.reshape(batch * g_, n_chunk, CMP_STRIDE * dh)
            blocks = jnp.concatenate([ch[:, :-1], ch[:, 1:]], axis=2)
            return jnp.pad(blocks, ((0, 0), (0, nc - (n_chunk - 1)), (0, 0)))

        kc, vc = _compress(unfold(kv(0)), unfold(kv(1)), cmp_pe_k[l], cmp_w1_k[l], cmp_w2_k[l],
                           cmp_pe_v[l], cmp_w1_v[l], cmp_w2_v[l])
        kc = jnp.transpose(kc.reshape(batch, g_, nc, dh), (0, 2, 1, 3)).reshape(batch, nc, kvw)
        vcT = jnp.swapaxes(vc.reshape(batch, g_, nc, dh), 2, 3).reshape(batch, kvw, nc)
        qT = jnp.swapaxes(z3[:, :, off["q"]:off["q"] + a_w], 1, 2).astype(BF16)
        ks, kw = kv(2).astype(BF16), kv(4).astype(BF16)
        vsT = jnp.swapaxes(kv(3), 1, 2).astype(BF16)
        vwT = jnp.swapaxes(kv(5), 1, 2).astype(BF16)
        gT = jnp.swapaxes(z3[:, :, off["gn"]:off["gn"] + 3 * N_HEADS], 1, 2)
        y_att = _attention(qT, kc, vcT, ks, kw, vsT, vwT, gT, tables, batch, seq).reshape(n_tok, a_w)

        merged = _merge(y_rnn, y_att, w_up_rnn[l].astype(BF16), w_up_att[l].astype(BF16), z,
                        off["ma"] // d, off["mb"] // d)
        x1, h2 = _outproj(merged, w_out[l].astype(BF16), x2d, modr, mod_base, seq, norm_ffn[l])

        idx, wts, rank, counts = _router(h2, router_w[l], router_bias[l])
        starts, cnt, block_expert, n_used, n_rows = _moe_plan(counts, n_tok)
        xs = _dispatch(starts, cnt, idx, rank, h2, n_rows)
        y_sorted = _grouped_mlp(block_expert, n_used, xs, exp_w_gate[l], exp_w_up[l], exp_w_down[l], "moe_experts")
        sh_blocks = n_tok // min(ROW_BLOCK, n_tok)
        y_shared = _grouped_mlp(jnp.zeros((sh_blocks,), jnp.int32), jnp.full((1,), sh_blocks, jnp.int32), h2,
                                sh_w_gate[l][None], sh_w_up[l][None], sh_w_down[l][None], "moe_shared")
        x2d = _combine(starts, idx, rank, wts.T, y_sorted, y_shared, x1, modr, mod_base, seq, final_norm,
                       l == depth - 1)
    return x2d.reshape(batch, seq, d)
```

```python
import functools
import math

import numpy as np
import jax
import jax.numpy as jnp
from jax import lax
from jax.experimental import pallas as pl
from jax.experimental.pallas import tpu as pltpu

DEPTH = 2
RNN_WIDTH = 1024
RNN_BLOCKS = 8
CONV_WIDTH = 4
LRU_C = 8.0
N_HEADS = 16
N_KV_GROUPS = 4
HEAD_DIM = 64
CMP_BLOCK = 32
CMP_STRIDE = 16
CMP_HIDDEN = 128
SEL_BLOCK = 64
SEL_TOP_N = 8
WINDOW = 512
N_BUCKETS = 32
MAX_DISTANCE = 128
N_EXPERTS = 64
EXPERT_TOP_K = 8
N_EXPERT_GROUPS = 8
TOPK_EXPERT_GROUPS = 4
ROUTED_SCALE = 2.5
EPS = 1e-6
NEG = -1e30

LANE = 128
SUBLANE = 8
VMEM_LIMIT = 52 * 1024 * 1024
DMA_PRIORITIES = 2

TM_IN = 1024
TN_IN = 512
TN_MOD = 1024
TC_RNN = 256
TQ = 128
FAR_TILES = 4
TM_MERGE = 512
TN_MERGE = 1024
TM_OUT = 256
TM_ROUTE = 256
TM_DISP = 128
ROW_BLOCK = 256
TM_COMB = 128

F32 = jnp.float32
BF16 = jnp.bfloat16


def _cparams(sem):
    return pltpu.CompilerParams(dimension_semantics=sem, vmem_limit_bytes=VMEM_LIMIT)


def _round_up(a, b):
    return (a + b - 1) // b * b


def _tile(n, pref):
    if n <= pref:
        return n
    t = pref // LANE * LANE
    while n % t:
        t -= LANE
    return t


def _gelu_tanh(x):
    return x * (0.5 * (1.0 + jnp.tanh(math.sqrt(2.0 / math.pi) * (x + 0.044715 * (x * x * x)))))


def _sigmoid(x):
    return jax.nn.sigmoid(x)


def _mod_kernel(c_ref, w_ref, b_ref, o_ref):
    c = c_ref[...]
    ca = (c * _sigmoid(c)).astype(BF16)
    o_ref[...] = jnp.dot(ca, w_ref[...].astype(BF16), preferred_element_type=F32) + b_ref[...]


def _adaln_mod(c, ada_w, ada_b):
    nl, d, n6 = ada_w.shape
    b = c.shape[0]
    tn = _tile(n6, TN_MOD)
    return pl.pallas_call(
        _mod_kernel,
        grid=(nl, n6 // tn),
        in_specs=[pl.BlockSpec((b, d), lambda l, j: (0, 0)),
                  pl.BlockSpec((None, d, tn), lambda l, j: (l, 0, j)),
                  pl.BlockSpec((None, 1, tn), lambda l, j: (l, 0, j))],
        out_specs=pl.BlockSpec((None, b, tn), lambda l, j: (l, 0, j)),
        out_shape=jax.ShapeDtypeStruct((nl, b, n6), F32),
        compiler_params=_cparams(("arbitrary", "arbitrary")),
        name="adaln_mod",
    )(c, ada_w, ada_b.reshape(nl, 1, n6))


def _inproj_kernel(x_ref, nw_ref, sc_ref, sh_ref, w_ref, zf_ref, zb_ref, zg_ref, h_ref, *, nf, nj):
    j = pl.program_id(1)

    @pl.when(j == 0)
    def _():
        x = x_ref[...]
        ms = jnp.mean(x * x, axis=-1, keepdims=True)
        y = (x * lax.rsqrt(ms + EPS)) * nw_ref[...]
        h_ref[...] = (y * (1.0 + sc_ref[...]) + sh_ref[...]).astype(BF16)

    res = jnp.dot(h_ref[...], w_ref[...], preferred_element_type=F32)

    @pl.when(j < nf)
    def _():
        zf_ref[...] = res

    @pl.when(j >= nf)
    def _():
        zb_ref[...] = res.astype(BF16)

    @pl.when(j == nj - 1)
    def _():
        zg_ref[...] = res[:, 0:LANE]


def _inproj(x2d, norm_w, modr, mod_base, seq, w_p, n_f32):
    t, d = x2d.shape
    n_p = w_p.shape[1]
    tm = min(TM_IN, seq)
    tn = TN_IN
    assert n_f32 % tn == 0 and n_p % tn == 0
    nf, nj = n_f32 // tn, n_p // tn

    def mod_idx(k):
        return lambda i, j: (mod_base + ((i * tm) // seq) * 6 + k, 0, 0)

    return pl.pallas_call(
        functools.partial(_inproj_kernel, nf=nf, nj=nj),
        grid=(t // tm, nj),
        in_specs=[pl.BlockSpec((tm, d), lambda i, j: (i, 0)),
                  pl.BlockSpec((1, d), lambda i, j: (0, 0)),
                  pl.BlockSpec((None, 1, d), mod_idx(1)),
                  pl.BlockSpec((None, 1, d), mod_idx(0)),
                  pl.BlockSpec((d, tn), lambda i, j: (0, j))],
        out_specs=[pl.BlockSpec((tm, tn), lambda i, j: (i, jnp.minimum(j, nf - 1))),
                   pl.BlockSpec((tm, tn), lambda i, j: (i, jnp.maximum(j - nf, 0))),
                   pl.BlockSpec((tm, LANE), lambda i, j: (i, 0))],
        out_shape=[jax.ShapeDtypeStruct((t, n_f32), F32), jax.ShapeDtypeStruct((t, n_p - n_f32), BF16),
                   jax.ShapeDtypeStruct((t, LANE), F32)],
        scratch_shapes=[pltpu.VMEM((tm, d), BF16)],
        compiler_params=_cparams(("arbitrary", "arbitrary")),
        name="inproj",
    )(x2d, norm_w.reshape(1, d), modr, modr, w_p)


def _rglru_kernel(u_ref, g_ref, cw_ref, cb_ref, wa_ref, ba_ref, wx_ref, bx_ref, lam_ref, y_ref,
                  ubuf, a_s, b_s, h_s, hcar):
    tc, r = u_ref.shape
    nb = wa_ref.shape[0]
    bw = r // nb

    @pl.when(pl.program_id(1) == 0)
    def _():
        ubuf[0:SUBLANE, :] = jnp.zeros((SUBLANE, r), F32)
        hcar[...] = jnp.zeros_like(hcar)

    ubuf[SUBLANE:SUBLANE + tc, :] = u_ref[...]
    cw = cw_ref[...]
    uc = cb_ref[...] + cw[CONV_WIDTH - 1:CONV_WIDTH, :] * ubuf[SUBLANE:SUBLANE + tc, :]
    for k in range(CONV_WIDTH - 1):
        off = SUBLANE - (CONV_WIDTH - 1) + k
        uc = uc + cw[k:k + 1, :] * ubuf[off:off + tc, :]
    ubuf[0:SUBLANE, :] = ubuf[tc:tc + SUBLANE, :]

    ucb = uc.astype(BF16)
    rp = []
    xp = []
    for n in range(nb):
        blk = ucb[:, n * bw:(n + 1) * bw]
        rp.append(jnp.dot(blk, wa_ref[n], preferred_element_type=F32))
        xp.append(jnp.dot(blk, wx_ref[n], preferred_element_type=F32))
    rg = _sigmoid(jnp.concatenate(rp, axis=1) + ba_ref[...])
    ig = _sigmoid(jnp.concatenate(xp, axis=1) + bx_ref[...])
    nl = -lam_ref[...]
    sp = jnp.maximum(nl, 0.0) + jnp.log1p(jnp.exp(-jnp.abs(nl)))
    log_a = (-LRU_C * rg) * sp
    a_s[...] = jnp.exp(log_a)
    th = jnp.tanh(log_a)
    one_minus_a2 = (-2.0 * th) / (1.0 - th)
    b_s[...] = jnp.sqrt(one_minus_a2) * (ig * uc)

    def step(t, h):
        h = a_s[pl.ds(t, 1), :] * h + b_s[pl.ds(t, 1), :]
        h_s[pl.ds(t, 1), :] = h
        return h

    h_last = lax.fori_loop(0, tc, step, hcar[0:1, :], unroll=8)
    hcar[0:1, :] = h_last
    y_ref[...] = (_gelu_tanh(g_ref[...]) * h_s[...]).astype(y_ref.dtype)


def _rglru(z2d, batch, seq, u_blk, g_blk, conv_w, conv_b, wa, ba, wx, bx, lam):
    r = conv_w.shape[1]
    tc = min(TC_RNN, seq)
    nt = seq // tc
    nb, bw, _ = wa.shape
    row = lambda v: v.reshape(1, r)
    full = lambda shape: pl.BlockSpec(shape, lambda b, t: (0,) * len(shape))
    return pl.pallas_call(
        _rglru_kernel,
        grid=(batch, nt),
        in_specs=[pl.BlockSpec((tc, r), lambda b, t: (b * nt + t, u_blk)),
                  pl.BlockSpec((tc, r), lambda b, t: (b * nt + t, g_blk)),
                  full((CONV_WIDTH, r)), full((1, r)),
                  full((nb, bw, bw)), full((1, r)),
                  full((nb, bw, bw)), full((1, r)), full((1, r))],
        out_specs=pl.BlockSpec((tc, r), lambda b, t: (b * nt + t, 0)),
        out_shape=jax.ShapeDtypeStruct((batch * seq, r), BF16),
        scratch_shapes=[pltpu.VMEM((tc + SUBLANE, r), F32), pltpu.VMEM((tc, r), F32),
                        pltpu.VMEM((tc, r), F32), pltpu.VMEM((tc, r), F32),
                        pltpu.VMEM((SUBLANE, r), F32)],
        compiler_params=_cparams(("arbitrary", "arbitrary")),
        name="rglru",
    )(z2d, z2d, conv_w, row(conv_b), wa.astype(BF16), row(ba), wx.astype(BF16), row(bx), row(lam))


def _compress_kernel(xk_ref, xv_ref, pek_ref, w1k_ref, w2k_ref, pev_ref, w1v_ref, w2v_ref, kc_ref, vc_ref):
    def one(x_ref, pe_ref, w1_ref, w2_ref, o_ref):
        blocks = (x_ref[...] + pe_ref[...]).astype(BF16)
        hid = _gelu_tanh(jnp.dot(blocks, w1_ref[...], preferred_element_type=F32))
        o_ref[...] = jnp.dot(hid.astype(BF16), w2_ref[...], preferred_element_type=F32).astype(o_ref.dtype)

    one(xk_ref, pek_ref, w1k_ref, w2k_ref, kc_ref)
    one(xv_ref, pev_ref, w1v_ref, w2v_ref, vc_ref)


def _compress(xk, xv, pe_k, w1_k, w2_k, pe_v, w1_v, w2_v):
    bg, nc, kd = xk.shape
    dh = w2_k.shape[1]
    hid = w2_k.shape[0]
    x_spec = pl.BlockSpec((None, nc, kd), lambda i: (i, 0, 0))
    full = lambda shape: pl.BlockSpec(shape, lambda i: (0,) * len(shape))
    o_spec = pl.BlockSpec((None, nc, dh), lambda i: (i, 0, 0))
    prep = lambda pe, w1, w2: (pe.reshape(1, kd), w1.reshape(kd, hid).astype(BF16), w2.astype(BF16))
    return pl.pallas_call(
        _compress_kernel,
        grid=(bg,),
        in_specs=[x_spec, x_spec, full((1, kd)), full((kd, hid)), full((hid, dh)),
                  full((1, kd)), full((kd, hid)), full((hid, dh))],
        out_specs=[o_spec, o_spec],
        out_shape=[jax.ShapeDtypeStruct((bg, nc, dh), BF16)] * 2,
        compiler_params=_cparams(("arbitrary",)),
        name="nsa_compress",
    )(xk, xv, *prep(pe_k, w1_k, w2_k), *prep(pe_v, w1_v, w2_v))


def _col_max(x):
    return jnp.max(x, axis=0, keepdims=True)


def _attn_kernel(qT_ref, kc_ref, vcT_ref, ks_ref, kw_ref, vsT_ref, vwT_ref, gT_ref,
                 tnear_ref, cfar_ref, cbias_ref, ovl_ref, o_ref,
                 qbd, sel_s, m_s, l_s, acc_s):
    i = pl.program_id(1)
    g_, hg, dh = N_KV_GROUPS, N_HEADS // N_KV_GROUPS, HEAD_DIM
    tq = qT_ref.shape[1]
    gw = hg * tq
    nw = g_ * gw
    n_sel = sel_s.shape[0]
    per = tq // SEL_BLOCK

    qT = (qT_ref[...].astype(F32) * (HEAD_DIM ** -0.5)).astype(BF16)
    zero_blk = jnp.zeros((dh, gw), BF16)
    rows = []
    for g in range(g_):
        qcat = jnp.concatenate([qT[(g * hg + h) * dh:(g * hg + h + 1) * dh, :] for h in range(hg)], axis=1)
        rows.append(jnp.concatenate([zero_blk] * g + [qcat] + [zero_blk] * (g_ - 1 - g), axis=1))
    qbd[...] = jnp.concatenate(rows, axis=0)

    def lane_q(shape):
        return lax.broadcasted_iota(jnp.int32, shape, 1) % tq

    def pv(vT_ref, start, nrows, p):
        pb = p.astype(BF16)
        outs = []
        for g in range(g_):
            if start is None:
                v = vT_ref[g * dh:(g + 1) * dh, :]
            else:
                v = vT_ref[g * dh:(g + 1) * dh, pl.ds(start, nrows)]
            outs.append(jnp.dot(v, pb[:, g * gw:(g + 1) * gw], preferred_element_type=F32))
        return jnp.concatenate(outs, axis=1)

    nc = kc_ref.shape[0]
    near_t = jnp.maximum(i - 1, 0)
    near_start = pl.multiple_of(near_t * tq, tq)
    far_end = near_t
    wr = WINDOW - tq
    win_start = pl.multiple_of(jnp.maximum(i - WINDOW // tq, 0) * tq, tq)
    lhs = jnp.concatenate([kc_ref[...], ks_ref[pl.ds(near_start, 2 * tq), :],
                           kw_ref[pl.ds(near_start, 2 * tq), :], kw_ref[pl.ds(win_start, wr), :]], axis=0)
    s_all = jnp.dot(lhs, qbd[...], preferred_element_type=F32)
    r_slc, r_win, r_far = nc, nc + 2 * tq, nc + 4 * tq

    sc = s_all[0:nc] + cbias_ref[...]
    nrow = lax.broadcasted_iota(jnp.int32, (nc, nw), 0)
    cmask = (nrow * CMP_STRIDE + (CMP_BLOCK - 1)) <= i * tq + lane_q((nc, nw))
    sc = jnp.where(cmask, sc, NEG)
    pc = jnp.where(cmask, jnp.exp(sc - _col_max(sc)), 0.0)
    lc = jnp.sum(pc, axis=0, keepdims=True)
    pc = pc * jnp.where(lc > 0.0, 1.0 / lc, 0.0)
    o_cmp = pv(vcT_ref, None, nc, pc)

    psum = jnp.concatenate(
        [sum(pc[:, g * gw + h * tq:g * gw + (h + 1) * tq] for h in range(hg)) for g in range(g_)], axis=1)
    imp = jnp.dot(ovl_ref[...], psum, preferred_element_type=F32, precision=lax.Precision.HIGHEST)
    jrow = lax.broadcasted_iota(jnp.int32, (n_sel, g_ * tq), 0)
    tq_abs = i * tq + lane_q((n_sel, g_ * tq))
    cur = tq_abs // SEL_BLOCK
    forced = (jrow == 0) | (jrow == cur) | (jrow == cur - 1)
    valid = jrow * SEL_BLOCK <= tq_abs
    work = jnp.where(forced, jnp.inf, jnp.where(valid, imp, -jnp.inf))
    jrow_f = jrow.astype(F32)
    sel = jnp.zeros((n_sel, g_ * tq), F32)
    for _ in range(min(SEL_TOP_N, n_sel)):
        mx = _col_max(work)
        first = jnp.min(jnp.where(work == mx, jrow_f, float(n_sel)), axis=0, keepdims=True)
        pick = jrow_f == first
        sel = jnp.where(pick, 1.0, sel)
        work = jnp.where(pick, -jnp.inf, work)
    sel_s[...] = jnp.where(sel > 0.5, 0.0, NEG)

    def sel_add(first_blk, n_blk):
        parts = []
        for c in range(n_blk):
            rowv = sel_s[pl.ds(first_blk + c, 1), :]
            rowv = jnp.concatenate([rowv[:, g * tq:(g + 1) * tq] for g in range(g_) for _ in range(hg)], axis=1)
            parts.append(jnp.broadcast_to(rowv, (SEL_BLOCK, nw)))
        return jnp.concatenate(parts, axis=0)

    def scores(k_ref, start, nrows, bias):
        return jnp.dot(k_ref[pl.ds(start, nrows), :], qbd[...], preferred_element_type=F32) + bias

    def flash_init(s, vT_ref, start, nrows):
        m = _col_max(s)
        p = jnp.exp(s - m)
        m_s[...] = m
        l_s[...] = jnp.sum(p, axis=0, keepdims=True)
        acc_s[...] = pv(vT_ref, start, nrows, p)

    def flash_update(s, vT_ref, start, nrows):
        m_old = m_s[...]
        m_new = jnp.maximum(m_old, _col_max(s))
        alpha = jnp.exp(m_old - m_new)
        p = jnp.exp(s - m_new)
        m_s[...] = m_new
        l_s[...] = alpha * l_s[...] + jnp.sum(p, axis=0, keepdims=True)
        acc_s[...] = alpha * acc_s[...] + pv(vT_ref, start, nrows, p)

    bias_off = pl.multiple_of(jnp.where(i == 0, tq, 0), tq)
    near_key = near_start + lax.broadcasted_iota(jnp.int32, (2 * tq, nw), 0)
    near_add = tnear_ref[pl.ds(bias_off, 2 * tq), :] + jnp.where(near_key <= i * tq + lane_q((2 * tq, nw)), 0.0, NEG)

    flash_init(s_all[r_slc:r_win] + near_add + sel_add(near_t * per, 2 * per), vsT_ref, near_start, 2 * tq)
    fr = FAR_TILES * tq

    def far_slc(c, carry):
        hi_t = far_end - c * FAR_TILES
        st_t = jnp.maximum(hi_t - FAR_TILES, 0)
        start = pl.multiple_of(st_t * tq, tq)
        key_abs = start + lax.broadcasted_iota(jnp.int32, (fr, nw), 0)
        add = jnp.where(key_abs < hi_t * tq, 0.0, NEG) + sel_add(st_t * per, FAR_TILES * per)
        flash_update(scores(ks_ref, start, fr, cfar_ref[...]) + add, vsT_ref, start, fr)
        return carry

    lax.fori_loop(0, (far_end + FAR_TILES - 1) // FAR_TILES, far_slc, 0)
    o_slc = acc_s[...] * (1.0 / l_s[...])

    flash_init(s_all[r_win:r_far] + near_add, vwT_ref, near_start, 2 * tq)
    key_abs = win_start + lax.broadcasted_iota(jnp.int32, (wr, nw), 0)
    keep = (key_abs < far_end * tq) & (key_abs > i * tq + lane_q((wr, nw)) - WINDOW)
    flash_update(s_all[r_far:] + cfar_ref[...] + jnp.where(keep, 0.0, NEG), vwT_ref, win_start, wr)
    o_win = acc_s[...] * (1.0 / l_s[...])

    gate = _sigmoid(gT_ref[...])

    def grow(j):
        return jnp.concatenate([gate[h * 3 + j:h * 3 + j + 1, :] for h in range(N_HEADS)], axis=1)

    o_t = grow(0) * o_cmp + grow(1) * o_slc + grow(2) * o_win
    o_hd = jnp.concatenate([o_t[:, h * tq:(h + 1) * tq] for h in range(N_HEADS)], axis=0)
    o_ref[...] = o_hd.T.astype(o_ref.dtype)


def _t5_bucket_np(dist):
    n = np.maximum(dist, 0)
    max_exact = N_BUCKETS // 2
    nf = np.maximum(n, 1).astype(np.float64)
    large = max_exact + (np.log(nf / max_exact) / math.log(MAX_DISTANCE / max_exact)
                         * (N_BUCKETS - max_exact)).astype(np.int64)
    large = np.minimum(large, N_BUCKETS - 1)
    return np.where(n < max_exact, n, large).astype(np.int32)


def _bias_tables(rel_bias, seq):
    tq = min(TQ, seq)
    assert MAX_DISTANCE <= tq, "tiles two or more behind the diagonal must all fall in the last bucket"
    nh = N_HEADS
    nc = _round_up((seq - CMP_BLOCK) // CMP_STRIDE + 1, LANE)
    off_max = (nc - 1) * CMP_STRIDE + CMP_BLOCK - 1
    fd = rel_bias.astype(F32)[_t5_bucket_np(np.arange(-off_max, seq))]
    fdT = fd.T
    kk = np.arange(tq)[:, None]
    qq = np.arange(tq)[None, :]

    def toeplitz(offset):
        idx = off_max + np.maximum(offset + qq - kk, 0)
        return jnp.transpose(fd[idx], (0, 2, 1)).reshape(tq, nh * tq)

    tnear = jnp.concatenate([toeplitz(tq), toeplitz(0), toeplitz(0)], axis=0)
    cfar = jnp.broadcast_to(rel_bias.astype(F32)[N_BUCKETS - 1][:, None], (nh, tq)).reshape(1, nh * tq)
    rows = [lax.slice_in_dim(fdT, off_max - (n * CMP_STRIDE + CMP_BLOCK - 1),
                             off_max - (n * CMP_STRIDE + CMP_BLOCK - 1) + seq, axis=1) for n in range(nc)]
    cb = jnp.stack(rows, axis=0).reshape(nc, nh, seq // tq, tq)
    cbias = jnp.transpose(cb, (2, 0, 1, 3)).reshape(seq // tq, nc, nh * tq)
    n_sel = seq // SEL_BLOCK
    cmp_start = np.arange(nc) * CMP_STRIDE
    sel_start = np.arange(n_sel) * SEL_BLOCK
    ovl = ((cmp_start[None, :] < sel_start[:, None] + SEL_BLOCK)
           & (cmp_start[None, :] + CMP_BLOCK > sel_start[:, None])).astype(np.float32)
    return tnear, cfar, cbias, jnp.asarray(ovl)


def _attention(qT, kc, vcT, zb3, ks_blk, kw_blk, vsT, vwT, gT, tables, batch, seq):
    tnear, cfar, cbias, ovl = tables
    g_, dh = N_KV_GROUPS, HEAD_DIM
    tq = min(TQ, seq)
    assert seq % tq == 0 and seq >= FAR_TILES * tq and WINDOW % tq == 0 and tq % SEL_BLOCK == 0
    nqt = seq // tq
    nw = N_HEADS * tq
    nc = kc.shape[1]
    n_sel = seq // SEL_BLOCK
    kvw = g_ * dh
    per_b = lambda shape: pl.BlockSpec((None,) + shape, lambda b, i: (b,) + (0,) * len(shape))
    full = lambda shape: pl.BlockSpec(shape, lambda b, i: (0,) * len(shape))
    return pl.pallas_call(
        _attn_kernel,
        grid=(batch, nqt),
        in_specs=[pl.BlockSpec((None, N_HEADS * dh, tq), lambda b, i: (b, 0, i)),
                  per_b((nc, kvw)),
                  per_b((kvw, nc)),
                  pl.BlockSpec((None, seq, kvw), lambda b, i: (b, 0, ks_blk)),
                  pl.BlockSpec((None, seq, kvw), lambda b, i: (b, 0, kw_blk)),
                  per_b((kvw, seq)), per_b((kvw, seq)),
                  pl.BlockSpec((None, 3 * N_HEADS, tq), lambda b, i: (b, 0, i)),
                  full((3 * tq, nw)), full((1, nw)),
                  pl.BlockSpec((None, nc, nw), lambda b, i: (i, 0, 0)),
                  full((n_sel, nc))],
        out_specs=pl.BlockSpec((None, tq, N_HEADS * dh), lambda b, i: (b, i, 0)),
        out_shape=jax.ShapeDtypeStruct((batch, seq, N_HEADS * dh), BF16),
        scratch_shapes=[pltpu.VMEM((kvw, nw), BF16), pltpu.VMEM((n_sel, g_ * tq), F32),
                        pltpu.VMEM((1, nw), F32), pltpu.VMEM((1, nw), F32), pltpu.VMEM((dh, nw), F32)],
        compiler_params=_cparams(("arbitrary", "arbitrary")),
        name="nsa_attention",
    )(qT, kc, vcT, zb3, zb3, vsT, vwT, gT, tnear, cfar, cbias, ovl)


def _merge_kernel(yr_ref, ya_ref, wr_ref, wa_ref, ma_ref, mb_ref, o_ref):
    pr = jnp.dot(yr_ref[...], wr_ref[...], preferred_element_type=F32)
    pa = jnp.dot(ya_ref[...], wa_ref[...], preferred_element_type=F32)
    o_ref[...] = (_sigmoid(ma_ref[...]) * pr + _sigmoid(mb_ref[...]) * pa).astype(o_ref.dtype)


def _merge(y_rnn, y_att, w_ur, w_ua, z2d, ma_blk, mb_blk):
    t, r = y_rnn.shape
    a = y_att.shape[1]
    d = w_ur.shape[1]
    tm = min(TM_MERGE, t)
    tn = min(TN_MERGE, d)
    nj = d // tn
    return pl.pallas_call(
        _merge_kernel,
        grid=(t // tm, nj),
        in_specs=[pl.BlockSpec((tm, r), lambda i, j: (i, 0)),
                  pl.BlockSpec((tm, a), lambda i, j: (i, 0)),
                  pl.BlockSpec((r, tn), lambda i, j: (0, j)),
                  pl.BlockSpec((a, tn), lambda i, j: (0, j)),
                  pl.BlockSpec((tm, tn), lambda i, j: (i, ma_blk * nj + j)),
                  pl.BlockSpec((tm, tn), lambda i, j: (i, mb_blk * nj + j))],
        out_specs=pl.BlockSpec((tm, tn), lambda i, j: (i, j)),
        out_shape=jax.ShapeDtypeStruct((t, d), BF16),
        compiler_params=_cparams(("arbitrary", "arbitrary")),
        name="merge",
    )(y_rnn, y_att, w_ur, w_ua, z2d, z2d)


def _outproj_kernel(mg_ref, w_ref, x_ref, g1_ref, nw_ref, sc_ref, sh_ref, x1_ref, h2_ref):
    x1 = x_ref[...] + g1_ref[...] * jnp.dot(mg_ref[...], w_ref[...], preferred_element_type=F32)
    x1_ref[...] = x1
    ms = jnp.mean(x1 * x1, axis=-1, keepdims=True)
    y = (x1 * lax.rsqrt(ms + EPS)) * nw_ref[...]
    h2_ref[...] = y * (1.0 + sc_ref[...]) + sh_ref[...]


def _outproj(merged, w_out, x2d, modr, mod_base, seq, norm_w):
    t, d = x2d.shape
    tm = min(TM_OUT, seq)

    def mod_idx(k):
        return lambda i: (mod_base + ((i * tm) // seq) * 6 + k, 0, 0)

    row_spec = pl.BlockSpec((tm, d), lambda i: (i, 0))
    return pl.pallas_call(
        _outproj_kernel,
        grid=(t // tm,),
        in_specs=[row_spec, pl.BlockSpec((d, d), lambda i: (0, 0)), row_spec,
                  pl.BlockSpec((None, 1, d), mod_idx(2)),
                  pl.BlockSpec((1, d), lambda i: (0, 0)),
                  pl.BlockSpec((None, 1, d), mod_idx(4)),
                  pl.BlockSpec((None, 1, d), mod_idx(3))],
        out_specs=[row_spec, row_spec],
        out_shape=[jax.ShapeDtypeStruct((t, d), F32)] * 2,
        compiler_params=_cparams(("arbitrary",)),
        name="outproj",
    )(merged, w_out, x2d, modr, norm_w.reshape(1, d), modr, modr)


def _router_kernel(h_ref, rw_ref, rb_ref, idx_ref, wt_ref, rank_ref, cnt_ref, carry):
    ne = rw_ref.shape[0]
    tm = h_ref.shape[0]
    per = ne // N_EXPERT_GROUPS
    assert per == SUBLANE, "one expert group per sublane tile"

    @pl.when(pl.program_id(0) == 0)
    def _():
        carry[...] = jnp.zeros_like(carry)

    logits = lax.dot_general(rw_ref[...], h_ref[...].astype(BF16), (((1,), (1,)), ((), ())),
                             preferred_element_type=F32)
    scores = _sigmoid(logits)
    biased = scores + rb_ref[...]
    erow = lax.broadcasted_iota(jnp.int32, (ne, tm), 0).astype(F32)
    grow = lax.broadcasted_iota(jnp.int32, (ne, tm), 0) // per

    gparts = []
    sub = lax.broadcasted_iota(jnp.int32, (per, tm), 0).astype(F32)
    for gi in range(N_EXPERT_GROUPS):
        xg = biased[gi * per:(gi + 1) * per, :]
        m1 = _col_max(xg)
        f1 = jnp.min(jnp.where(xg == m1, sub, float(per)), axis=0, keepdims=True)
        m2 = _col_max(jnp.where(sub == f1, -jnp.inf, xg))
        gparts.append(jnp.broadcast_to(m1 + m2, (per, tm)))
    gscore = jnp.concatenate(gparts, axis=0)

    kparts = []
    for gi in range(N_EXPERT_GROUPS):
        gs = gscore[gi * per:gi * per + 1, :]
        beats = (gscore > gs) | ((gscore == gs) & (grow < gi))
        nbeat = jnp.sum(beats.astype(F32), axis=0, keepdims=True)
        kparts.append(jnp.broadcast_to(nbeat < float(TOPK_EXPERT_GROUPS * per), (per, tm)))
    gkeep = jnp.concatenate(kparts, axis=0)

    work = jnp.where(gkeep, biased, -jnp.inf)
    picks = []
    chosen = jnp.zeros((ne, tm), F32)
    for _ in range(EXPERT_TOP_K):
        mx = _col_max(work)
        first = jnp.min(jnp.where(work == mx, erow, float(ne)), axis=0, keepdims=True)
        pick = erow == first
        picks.append(pick)
        chosen = jnp.where(pick, 1.0, chosen)
        work = jnp.where(pick, -jnp.inf, work)

    tri = (lax.broadcasted_iota(jnp.int32, (tm, tm), 0) < lax.broadcasted_iota(jnp.int32, (tm, tm), 1))
    before = jnp.dot(chosen.astype(BF16), tri.astype(BF16), preferred_element_type=F32)
    pos = before + carry[:, 0:1]
    new_carry = carry[:, 0:1] + jnp.sum(chosen, axis=1, keepdims=True)
    carry[...] = jnp.broadcast_to(new_carry, carry.shape)
    cnt_ref[...] = carry[...]

    krow = lax.broadcasted_iota(jnp.int32, (EXPERT_TOP_K, tm), 0)
    idx_o = jnp.zeros((EXPERT_TOP_K, tm), F32)
    wt_o = jnp.zeros((EXPERT_TOP_K, tm), F32)
    rk_o = jnp.zeros((EXPERT_TOP_K, tm), F32)
    for k, pick in enumerate(picks):
        sel = lambda v: jnp.sum(jnp.where(pick, v, 0.0), axis=0, keepdims=True)
        idx_o = jnp.where(krow == k, sel(erow), idx_o)
        wt_o = jnp.where(krow == k, sel(scores), wt_o)
        rk_o = jnp.where(krow == k, sel(pos), rk_o)
    wsum = jnp.sum(wt_o, axis=0, keepdims=True)
    idx_ref[...] = idx_o.astype(jnp.int32)
    wt_ref[...] = (ROUTED_SCALE * wt_o) / wsum
    rank_ref[...] = rk_o.astype(jnp.int32)


def _router(h2, router_w, router_bias):
    t, d = h2.shape
    ne = router_w.shape[1]
    tm = min(TM_ROUTE, t)
    k_spec = pl.BlockSpec((EXPERT_TOP_K, tm), lambda i: (0, i))
    return pl.pallas_call(
        _router_kernel,
        grid=(t // tm,),
        in_specs=[pl.BlockSpec((tm, d), lambda i: (i, 0)),
                  pl.BlockSpec((ne, d), lambda i: (0, 0)),
                  pl.BlockSpec((ne, 1), lambda i: (0, 0))],
        out_specs=[k_spec, k_spec, k_spec, pl.BlockSpec((ne, LANE), lambda i: (0, 0))],
        out_shape=[jax.ShapeDtypeStruct((EXPERT_TOP_K, t), jnp.int32),
                   jax.ShapeDtypeStruct((EXPERT_TOP_K, t), F32),
                   jax.ShapeDtypeStruct((EXPERT_TOP_K, t), jnp.int32),
                   jax.ShapeDtypeStruct((ne, LANE), F32)],
        scratch_shapes=[pltpu.VMEM((ne, LANE), F32)],
        compiler_params=_cparams(("arbitrary",)),
        name="moe_router",
    )(h2, router_w.T.astype(BF16), router_bias.reshape(ne, 1).astype(F32))


def _row_copy(src_ref, src_row, dst_ref, dst_row, sem):
    return pltpu.make_async_copy(src_ref.at[pl.ds(src_row, 1)], dst_ref.at[pl.ds(dst_row, 1)], sem)


def _pack_bf16_pairs(x):
    half = x.shape[1] // 2
    bits = lax.bitcast_convert_type(x.astype(BF16).astype(F32), jnp.uint32)
    return bits[:, :half] | (bits[:, half:] >> 16)


def _unpack_bf16_pairs(w):
    hi = lax.bitcast_convert_type(w & jnp.uint32(0xFFFF0000), F32)
    lo = lax.bitcast_convert_type(w << 16, F32)
    return hi, lo


def _row_tiles(words):
    return (words // LANE, LANE)


def _dispatch_kernel(start_ref, cnt_ref, dest_ref, h_ref, xs_ref, hp, zblk, sem):
    tm = h_ref.shape[0]
    ne = start_ref.shape[0] - 1
    rb = zblk.shape[0]
    n_blocks = xs_ref.shape[0] // rb
    hp[...] = _pack_bf16_pairs(h_ref[...]).reshape(hp.shape)

    def pad_rows(e, fn):
        lo = start_ref[e] + cnt_ref[e]
        lax.fori_loop(lo, start_ref[e + 1], lambda r, c: fn(_row_copy(zblk, 0, xs_ref, r, sem)) or c, 0)

    def tail_blocks(fn):
        def body(b, c):
            fn(pltpu.make_async_copy(zblk, xs_ref.at[pl.ds(pl.multiple_of(b * rb, rb), rb)], sem))
            return c
        lax.fori_loop(start_ref[ne] // rb, n_blocks, body, 0)

    @pl.when(pl.program_id(0) == 0)
    def _():
        zblk[...] = jnp.zeros_like(zblk)
        lax.fori_loop(0, ne, lambda e, c: pad_rows(e, lambda cp: cp.start()) or c, 0)
        tail_blocks(lambda cp: cp.start())
        lax.fori_loop(0, ne, lambda e, c: pad_rows(e, lambda cp: cp.wait()) or c, 0)
        tail_blocks(lambda cp: cp.wait())

    def token_rows(fn):
        def body(r, c):
            for k in range(EXPERT_TOP_K):
                fn(_row_copy(hp, r, xs_ref, dest_ref[k, r], sem), k)
            return c
        lax.fori_loop(0, tm, body, 0)

    token_rows(lambda cp, k: cp.start(priority=k % DMA_PRIORITIES))
    token_rows(lambda cp, k: cp.wait())


def _dispatch(starts, cnt, dest, h2, n_rows):
    t, d = h2.shape
    tm = min(TM_DISP, t)
    return pl.pallas_call(
        _dispatch_kernel,
        grid_spec=pltpu.PrefetchScalarGridSpec(
            num_scalar_prefetch=2,
            grid=(t // tm,),
            in_specs=[pl.BlockSpec((EXPERT_TOP_K, tm), lambda i, *_: (0, i), memory_space=pltpu.SMEM),
                      pl.BlockSpec((tm, d), lambda i, *_: (i, 0))],
            out_specs=pl.BlockSpec(memory_space=pl.ANY),
            scratch_shapes=[pltpu.VMEM((tm,) + _row_tiles(d // 2), jnp.uint32),
                            pltpu.VMEM((min(ROW_BLOCK, n_rows),) + _row_tiles(d // 2), jnp.uint32),
                            pltpu.SemaphoreType.DMA(())]),
        out_shape=jax.ShapeDtypeStruct((n_rows,) + _row_tiles(d // 2), jnp.uint32),
        compiler_params=_cparams(("arbitrary",)),
        name="moe_dispatch",
    )(starts, cnt, dest, h2)


def _mlp_kernel(be_ref, nb_ref, x_ref, wg_ref, wu_ref, wd_ref, y_ref, wg_s, wu_s, wd_s, *, packed):
    i = pl.program_id(0)

    @pl.when((i == 0) | (be_ref[i] != be_ref[jnp.maximum(i - 1, 0)]))
    def _():
        wg_s[...] = wg_ref[...].astype(BF16)
        wu_s[...] = wu_ref[...].astype(BF16)
        wd_s[...] = wd_ref[...].astype(BF16)

    @pl.when(i < nb_ref[0])
    def _():
        if packed:
            xw = x_ref[...]
            xw = xw.reshape(xw.shape[0], xw.shape[1] * xw.shape[2])
            x = jnp.concatenate(_unpack_bf16_pairs(xw), axis=1).astype(BF16)
        else:
            x = x_ref[...].astype(BF16)
        gt = jnp.dot(x, wg_s[...], preferred_element_type=F32)
        up = jnp.dot(x, wu_s[...], preferred_element_type=F32)
        hb = ((gt * _sigmoid(gt)) * up).astype(BF16)
        y = jnp.dot(hb, wd_s[...], preferred_element_type=F32)
        y_ref[...] = _pack_bf16_pairs(y).reshape(y_ref.shape) if packed else y

    @pl.when(i >= nb_ref[0])
    def _():
        y_ref[...] = jnp.zeros_like(y_ref)


def _grouped_mlp(block_expert, n_used, xs, layer, w_gate, w_up, w_down, packed, name):
    n_rows, row_shape = xs.shape[0], xs.shape[1:]
    zeros = (0,) * len(row_shape)
    d, hid = w_gate.shape[2], w_gate.shape[3]
    rb = min(ROW_BLOCK, n_rows)
    w_spec = lambda shape: pl.BlockSpec((None, None) + shape, lambda i, be, nb: (layer, be[i], 0, 0))
    return pl.pallas_call(
        functools.partial(_mlp_kernel, packed=packed),
        grid_spec=pltpu.PrefetchScalarGridSpec(
            num_scalar_prefetch=2,
            grid=(n_rows // rb,),
            in_specs=[pl.BlockSpec((rb,) + row_shape, lambda i, be, nb: (jnp.minimum(i, nb[0] - 1),) + zeros),
                      w_spec((d, hid)), w_spec((d, hid)), w_spec((hid, d))],
            out_specs=pl.BlockSpec((rb,) + row_shape, lambda i, be, nb: (i,) + zeros),
            scratch_shapes=[pltpu.VMEM((d, hid), BF16), pltpu.VMEM((d, hid), BF16), pltpu.VMEM((hid, d), BF16)]),
        out_shape=jax.ShapeDtypeStruct(xs.shape, xs.dtype),
        compiler_params=_cparams(("arbitrary",)),
        name=name,
    )(block_expert, n_used, xs, w_gate, w_up, w_down)


def _combine_kernel(dest_ref, wt_ref, ysh_ref, x_ref, g2_ref, fn_ref, ys_ref, o_ref, ybuf, sem, *, final):
    tm, d = x_ref.shape

    def token_rows(fn):
        def body(r, c):
            for k in range(EXPERT_TOP_K):
                fn(_row_copy(ys_ref, dest_ref[k, r], ybuf.at[k], r, sem), k)
            return c
        lax.fori_loop(0, tm, body, 0)

    token_rows(lambda cp, k: cp.start(priority=k % DMA_PRIORITIES))
    token_rows(lambda cp, k: cp.wait())

    wt = wt_ref[...]
    acc_hi = jnp.zeros((tm, d // 2), F32)
    acc_lo = jnp.zeros((tm, d // 2), F32)
    for k in range(EXPERT_TOP_K):
        hi, lo = _unpack_bf16_pairs(ybuf[k].reshape(tm, d // 2))
        acc_hi = acc_hi + wt[:, k:k + 1] * hi
        acc_lo = acc_lo + wt[:, k:k + 1] * lo
    acc = ysh_ref[...] + jnp.concatenate([acc_hi, acc_lo], axis=1)
    xn = x_ref[...] + g2_ref[...] * acc
    if final:
        ms = jnp.mean(xn * xn, axis=-1, keepdims=True)
        xn = (xn * lax.rsqrt(ms + EPS)) * fn_ref[...]
    o_ref[...] = xn


def _combine(dest, wts_t, y_sorted, y_shared, x2d, modr, mod_base, seq, final_norm, final):
    t, d = x2d.shape
    tm = min(TM_COMB, seq)
    row_spec = pl.BlockSpec((tm, d), lambda i: (i, 0))
    return pl.pallas_call(
        functools.partial(_combine_kernel, final=final),
        grid=(t // tm,),
        in_specs=[pl.BlockSpec((EXPERT_TOP_K, tm), lambda i: (0, i), memory_space=pltpu.SMEM),
                  pl.BlockSpec((tm, EXPERT_TOP_K), lambda i: (i, 0)),
                  row_spec, row_spec,
                  pl.BlockSpec((None, 1, d), lambda i: (mod_base + ((i * tm) // seq) * 6 + 5, 0, 0)),
                  pl.BlockSpec((1, d), lambda i: (0, 0)),
                  pl.BlockSpec(memory_space=pl.ANY)],
        out_specs=row_spec,
        out_shape=jax.ShapeDtypeStruct((t, d), F32),
        scratch_shapes=[pltpu.VMEM((EXPERT_TOP_K, tm) + _row_tiles(d // 2), jnp.uint32),
                        pltpu.SemaphoreType.DMA(())],
        compiler_params=_cparams(("arbitrary",)),
        name="moe_combine",
    )(dest, wts_t, y_shared, x2d, modr, final_norm.reshape(1, d), y_sorted)


def _layout(d_model):
    r, a, kvw = RNN_WIDTH, N_HEADS * HEAD_DIM, N_KV_GROUPS * HEAD_DIM
    off = {}
    off["ma"], off["mb"] = 0, d_model
    off["u"] = 2 * d_model
    off["g"] = off["u"] + r
    off["kc"] = off["g"] + r
    off["vc"] = off["kc"] + kvw
    off["nf"] = off["vc"] + kvw
    off["q"] = 0
    off["ks"], off["vs"], off["kw"], off["vw"] = a, a + kvw, a + 2 * kvw, a + 3 * kvw
    off["gn"] = a + 4 * kvw
    off["nb"] = off["gn"] + TN_IN
    assert off["nf"] % TN_IN == 0 and off["gn"] % TN_IN == 0
    return off


def _pack_w_in(w_in_l, d_model):
    r, a, kvw = RNN_WIDTH, N_HEADS * HEAD_DIM, N_KV_GROUPS * HEAD_DIM
    off = _layout(d_model)
    s_u, s_g, s_q = 0, r, 2 * r
    s_kv = s_q + a
    s_gn = s_kv + 6 * kvw
    s_ma = s_gn + 3 * N_HEADS
    s_mb = s_ma + d_model
    cols = [w_in_l[:, s_ma:s_mb], w_in_l[:, s_mb:s_mb + d_model], w_in_l[:, s_u:s_g], w_in_l[:, s_g:s_q],
            w_in_l[:, s_kv:s_kv + 2 * kvw],
            w_in_l[:, s_q:s_kv], w_in_l[:, s_kv + 2 * kvw:s_gn], w_in_l[:, s_gn:s_ma]]
    w = jnp.concatenate(cols, axis=1)
    return jnp.pad(w, ((0, 0), (0, off["nf"] + off["nb"] - w.shape[1]))).astype(BF16)


def _moe_plan(idx, rank, counts, n_tok):
    cnt = counts[:, 0].astype(jnp.int32)
    padded = (cnt + ROW_BLOCK - 1) // ROW_BLOCK * ROW_BLOCK
    ends = jnp.cumsum(padded)
    starts = jnp.concatenate([jnp.zeros((1,), jnp.int32), ends]).astype(jnp.int32)
    onehot = idx[:, :, None] == jnp.arange(N_EXPERTS, dtype=jnp.int32)[None, None, :]
    dest = rank + jnp.sum(jnp.where(onehot, starts[None, None, :N_EXPERTS], 0), axis=2)
    n_blocks = (n_tok * EXPERT_TOP_K) // ROW_BLOCK + N_EXPERTS
    blk_start = jnp.arange(n_blocks, dtype=jnp.int32) * ROW_BLOCK
    owner = jnp.sum((ends[None, :] <= blk_start[:, None]).astype(jnp.int32), axis=1)
    block_expert = jnp.minimum(owner, N_EXPERTS - 1).astype(jnp.int32)
    n_used = (ends[-1] // ROW_BLOCK).astype(jnp.int32).reshape(1)
    return starts, cnt, dest.astype(jnp.int32), block_expert, n_used, n_blocks * ROW_BLOCK


def kernel(x, c, rel_bias, final_norm, ada_w, ada_b, norm_mix, norm_ffn, w_in, conv_w, conv_b, lru_wa, lru_ba, lru_wx, lru_bx, lru_lambda, cmp_pe_k, cmp_w1_k, cmp_w2_k, cmp_pe_v, cmp_w1_v, cmp_w2_v, w_up_rnn, w_up_att, w_out, router_w, router_bias, exp_w_gate, exp_w_up, exp_w_down, sh_w_gate, sh_w_up, sh_w_down):
    batch, seq, d = x.shape
    n_tok = batch * seq
    depth = ada_w.shape[0]
    g_, dh = N_KV_GROUPS, HEAD_DIM
    a_w, kvw, r = N_HEADS * HEAD_DIM, N_KV_GROUPS * HEAD_DIM, RNN_WIDTH
    off = _layout(d)
    assert seq % CMP_STRIDE == 0 and CMP_BLOCK == 2 * CMP_STRIDE

    mod = _adaln_mod(c, ada_w, ada_b)
    modr = mod.reshape(depth * batch * 6, 1, d)
    tables = _bias_tables(rel_bias, seq)
    nc = tables[2].shape[1]
    n_chunk = seq // CMP_STRIDE

    x2d = x.reshape(n_tok, d)
    for l in range(depth):
        mod_base = l * batch * 6
        zf, zb, zg = _inproj(x2d, norm_mix[l], modr, mod_base, seq, _pack_w_in(w_in[l], d), off["nf"])
        y_rnn = _rglru(zf, batch, seq, off["u"] // r, off["g"] // r, conv_w[l], conv_b[l],
                       lru_wa[l], lru_ba[l], lru_wx[l], lru_bx[l], lru_lambda[l])

        zb3 = zb.reshape(batch, seq, -1)
        zcol = lambda z3, name, w: z3[:, :, off[name]:off[name] + w]

        def unfold(v):
            ch = jnp.transpose(v.reshape(batch, n_chunk, CMP_STRIDE, g_, dh), (0, 3, 1, 2, 4))
            ch = ch.reshape(batch * g_, n_chunk, CMP_STRIDE * dh)
            blocks = jnp.concatenate([ch[:, :-1], ch[:, 1:]], axis=2)
            return jnp.pad(blocks, ((0, 0), (0, nc - (n_chunk - 1)), (0, 0)))

        zf3 = zf.reshape(batch, seq, -1)
        kc, vc = _compress(unfold(zcol(zf3, "kc", kvw)), unfold(zcol(zf3, "vc", kvw)),
                           cmp_pe_k[l], cmp_w1_k[l], cmp_w2_k[l], cmp_pe_v[l], cmp_w1_v[l], cmp_w2_v[l])
        kc = jnp.transpose(kc.reshape(batch, g_, nc, dh), (0, 2, 1, 3)).reshape(batch, nc, kvw)
        vcT = jnp.swapaxes(vc.reshape(batch, g_, nc, dh), 2, 3).reshape(batch, kvw, nc)
        qT = jnp.swapaxes(zcol(zb3, "q", a_w), 1, 2)
        vsT = jnp.swapaxes(zcol(zb3, "vs", kvw), 1, 2)
        vwT = jnp.swapaxes(zcol(zb3, "vw", kvw), 1, 2)
        gT = jnp.swapaxes(zg.reshape(batch, seq, LANE)[:, :, :3 * N_HEADS], 1, 2)
        y_att = _attention(qT, kc, vcT, zb3, off["ks"] // kvw, off["kw"] // kvw, vsT, vwT, gT, tables,
                           batch, seq).reshape(n_tok, a_w)

        merged = _merge(y_rnn, y_att, w_up_rnn[l].astype(BF16), w_up_att[l].astype(BF16), zf,
                        off["ma"] // d, off["mb"] // d)
        x1, h2 = _outproj(merged, w_out[l].astype(BF16), x2d, modr, mod_base, seq, norm_ffn[l])

        idx, wts, rank, counts = _router(h2, router_w[l], router_bias[l])
        starts, cnt, dest, block_expert, n_used, n_rows = _moe_plan(idx, rank, counts, n_tok)
        xs = _dispatch(starts, cnt, dest, h2, n_rows)
        y_sorted = _grouped_mlp(block_expert, n_used, xs, l, exp_w_gate, exp_w_up, exp_w_down, True, "moe_experts")
        sh_blocks = n_tok // min(ROW_BLOCK, n_tok)
        y_shared = _grouped_mlp(jnp.zeros((sh_blocks,), jnp.int32), jnp.full((1,), sh_blocks, jnp.int32), h2, l,
                                sh_w_gate[:, None], sh_w_up[:, None], sh_w_down[:, None], False, "moe_shared")
        x2d = _combine(dest, wts.T, y_sorted, y_shared, x1, modr, mod_base, seq, final_norm, l == depth - 1)
    return x2d.reshape(batch, seq, d)
```

```python
import functools
import math

import numpy as np
import jax
import jax.numpy as jnp
from jax import lax
from jax.experimental import pallas as pl
from jax.experimental.pallas import tpu as pltpu

DEPTH = 2
RNN_WIDTH = 1024
RNN_BLOCKS = 8
CONV_WIDTH = 4
LRU_C = 8.0
N_HEADS = 16
N_KV_GROUPS = 4
HEAD_DIM = 64
CMP_BLOCK = 32
CMP_STRIDE = 16
CMP_HIDDEN = 128
SEL_BLOCK = 64
SEL_TOP_N = 8
WINDOW = 512
N_BUCKETS = 32
MAX_DISTANCE = 128
N_EXPERTS = 64
EXPERT_TOP_K = 8
N_EXPERT_GROUPS = 8
TOPK_EXPERT_GROUPS = 4
ROUTED_SCALE = 2.5
EPS = 1e-6
NEG = -1e30

LANE = 128
SUBLANE = 8
VMEM_LIMIT = 52 * 1024 * 1024
DMA_PRIORITIES = 2

TM_IN = 1024
TN_IN = 512
TN_MOD = 1024
TC_RNN = 256
TQ = 128
FAR_TILES = 4
TM_MERGE = 512
TN_MERGE = 1024
TM_OUT = 256
TM_ROUTE = 256
TM_DISP = 128
ROW_BLOCK = 256
TM_COMB = 128

F32 = jnp.float32
BF16 = jnp.bfloat16


def _cparams(sem):
    return pltpu.CompilerParams(dimension_semantics=sem, vmem_limit_bytes=VMEM_LIMIT)


def _round_up(a, b):
    return (a + b - 1) // b * b


def _tile(n, pref):
    if n <= pref:
        return n
    t = pref // LANE * LANE
    while n % t:
        t -= LANE
    return t


def _gelu_tanh(x):
    return x * (0.5 * (1.0 + jnp.tanh(math.sqrt(2.0 / math.pi) * (x + 0.044715 * (x * x * x)))))


def _sigmoid(x):
    return jax.nn.sigmoid(x)


def _mod_kernel(c_ref, w_ref, b_ref, o_ref):
    c = c_ref[...]
    ca = (c * _sigmoid(c)).astype(BF16)
    o_ref[...] = jnp.dot(ca, w_ref[...].astype(BF16), preferred_element_type=F32) + b_ref[...]


def _adaln_mod(c, ada_w, ada_b):
    nl, d, n6 = ada_w.shape
    b = c.shape[0]
    tn = _tile(n6, TN_MOD)
    return pl.pallas_call(
        _mod_kernel,
        grid=(nl, n6 // tn),
        in_specs=[pl.BlockSpec((b, d), lambda l, j: (0, 0)),
                  pl.BlockSpec((None, d, tn), lambda l, j: (l, 0, j)),
                  pl.BlockSpec((None, 1, tn), lambda l, j: (l, 0, j))],
        out_specs=pl.BlockSpec((None, b, tn), lambda l, j: (l, 0, j)),
        out_shape=jax.ShapeDtypeStruct((nl, b, n6), F32),
        compiler_params=_cparams(("arbitrary", "arbitrary")),
        name="adaln_mod",
    )(c, ada_w, ada_b.reshape(nl, 1, n6))


def _inproj_kernel(x_ref, nw_ref, sc_ref, sh_ref, w_ref, zf_ref, zb_ref, zg_ref, h_ref, *, nf, nj):
    j = pl.program_id(1)

    @pl.when(j == 0)
    def _():
        x = x_ref[...]
        ms = jnp.mean(x * x, axis=-1, keepdims=True)
        y = (x * lax.rsqrt(ms + EPS)) * nw_ref[...]
        h_ref[...] = (y * (1.0 + sc_ref[...]) + sh_ref[...]).astype(BF16)

    res = jnp.dot(h_ref[...], w_ref[...], preferred_element_type=F32)

    @pl.when(j < nf)
    def _():
        zf_ref[...] = res

    @pl.when(j >= nf)
    def _():
        zb_ref[...] = res.astype(BF16)

    @pl.when(j == nj - 1)
    def _():
        zg_ref[...] = res[:, 0:LANE]


def _inproj(x2d, norm_w, modr, mod_base, seq, w_p, n_f32):
    t, d = x2d.shape
    n_p = w_p.shape[1]
    tm = min(TM_IN, seq)
    tn = TN_IN
    assert n_f32 % tn == 0 and n_p % tn == 0
    nf, nj = n_f32 // tn, n_p // tn

    def mod_idx(k):
        return lambda i, j: (mod_base + ((i * tm) // seq) * 6 + k, 0, 0)

    return pl.pallas_call(
        functools.partial(_inproj_kernel, nf=nf, nj=nj),
        grid=(t // tm, nj),
        in_specs=[pl.BlockSpec((tm, d), lambda i, j: (i, 0)),
                  pl.BlockSpec((1, d), lambda i, j: (0, 0)),
                  pl.BlockSpec((None, 1, d), mod_idx(1)),
                  pl.BlockSpec((None, 1, d), mod_idx(0)),
                  pl.BlockSpec((d, tn), lambda i, j: (0, j))],
        out_specs=[pl.BlockSpec((tm, tn), lambda i, j: (i, jnp.minimum(j, nf - 1))),
                   pl.BlockSpec((tm, tn), lambda i, j: (i, jnp.maximum(j - nf, 0))),
                   pl.BlockSpec((tm, LANE), lambda i, j: (i, 0))],
        out_shape=[jax.ShapeDtypeStruct((t, n_f32), F32), jax.ShapeDtypeStruct((t, n_p - n_f32), BF16),
                   jax.ShapeDtypeStruct((t, LANE), F32)],
        scratch_shapes=[pltpu.VMEM((tm, d), BF16)],
        compiler_params=_cparams(("arbitrary", "arbitrary")),
        name="inproj",
    )(x2d, norm_w.reshape(1, d), modr, modr, w_p)


def _rglru_kernel(u_ref, g_ref, cw_ref, cb_ref, wa_ref, ba_ref, wx_ref, bx_ref, lam_ref, y_ref,
                  ubuf, a_s, b_s, h_s, hcar):
    tc, r = u_ref.shape
    nb = wa_ref.shape[0]
    bw = r // nb

    @pl.when(pl.program_id(1) == 0)
    def _():
        ubuf[0:SUBLANE, :] = jnp.zeros((SUBLANE, r), F32)
        hcar[...] = jnp.zeros_like(hcar)

    ubuf[SUBLANE:SUBLANE + tc, :] = u_ref[...]
    cw = cw_ref[...]
    uc = cb_ref[...] + cw[CONV_WIDTH - 1:CONV_WIDTH, :] * ubuf[SUBLANE:SUBLANE + tc, :]
    for k in range(CONV_WIDTH - 1):
        off = SUBLANE - (CONV_WIDTH - 1) + k
        uc = uc + cw[k:k + 1, :] * ubuf[off:off + tc, :]
    ubuf[0:SUBLANE, :] = ubuf[tc:tc + SUBLANE, :]

    ucb = uc.astype(BF16)
    rp = []
    xp = []
    for n in range(nb):
        blk = ucb[:, n * bw:(n + 1) * bw]
        rp.append(jnp.dot(blk, wa_ref[n], preferred_element_type=F32))
        xp.append(jnp.dot(blk, wx_ref[n], preferred_element_type=F32))
    rg = _sigmoid(jnp.concatenate(rp, axis=1) + ba_ref[...])
    ig = _sigmoid(jnp.concatenate(xp, axis=1) + bx_ref[...])
    nl = -lam_ref[...]
    sp = jnp.maximum(nl, 0.0) + jnp.log1p(jnp.exp(-jnp.abs(nl)))
    log_a = (-LRU_C * rg) * sp
    a_s[...] = jnp.exp(log_a)
    th = jnp.tanh(log_a)
    one_minus_a2 = (-2.0 * th) / (1.0 - th)
    b_s[...] = jnp.sqrt(one_minus_a2) * (ig * uc)

    def step(t, h):
        h = a_s[pl.ds(t, 1), :] * h + b_s[pl.ds(t, 1), :]
        h_s[pl.ds(t, 1), :] = h
        return h

    h_last = lax.fori_loop(0, tc, step, hcar[0:1, :], unroll=8)
    hcar[0:1, :] = h_last
    y_ref[...] = (_gelu_tanh(g_ref[...]) * h_s[...]).astype(y_ref.dtype)


def _rglru(z2d, batch, seq, u_blk, g_blk, conv_w, conv_b, wa, ba, wx, bx, lam):
    r = conv_w.shape[1]
    tc = min(TC_RNN, seq)
    nt = seq // tc
    nb, bw, _ = wa.shape
    row = lambda v: v.reshape(1, r)
    full = lambda shape: pl.BlockSpec(shape, lambda b, t: (0,) * len(shape))
    return pl.pallas_call(
        _rglru_kernel,
        grid=(batch, nt),
        in_specs=[pl.BlockSpec((tc, r), lambda b, t: (b * nt + t, u_blk)),
                  pl.BlockSpec((tc, r), lambda b, t: (b * nt + t, g_blk)),
                  full((CONV_WIDTH, r)), full((1, r)),
                  full((nb, bw, bw)), full((1, r)),
                  full((nb, bw, bw)), full((1, r)), full((1, r))],
        out_specs=pl.BlockSpec((tc, r), lambda b, t: (b * nt + t, 0)),
        out_shape=jax.ShapeDtypeStruct((batch * seq, r), BF16),
        scratch_shapes=[pltpu.VMEM((tc + SUBLANE, r), F32), pltpu.VMEM((tc, r), F32),
                        pltpu.VMEM((tc, r), F32), pltpu.VMEM((tc, r), F32),
                        pltpu.VMEM((SUBLANE, r), F32)],
        compiler_params=_cparams(("arbitrary", "arbitrary")),
        name="rglru",
    )(z2d, z2d, conv_w, row(conv_b), wa.astype(BF16), row(ba), wx.astype(BF16), row(bx), row(lam))


def _compress_kernel(xk_ref, xv_ref, pek_ref, w1k_ref, w2k_ref, pev_ref, w1v_ref, w2v_ref, kc_ref, vc_ref):
    def one(x_ref, pe_ref, w1_ref, w2_ref, o_ref):
        blocks = (x_ref[...] + pe_ref[...]).astype(BF16)
        hid = _gelu_tanh(jnp.dot(blocks, w1_ref[...], preferred_element_type=F32))
        o_ref[...] = jnp.dot(hid.astype(BF16), w2_ref[...], preferred_element_type=F32).astype(o_ref.dtype)

    one(xk_ref, pek_ref, w1k_ref, w2k_ref, kc_ref)
    one(xv_ref, pev_ref, w1v_ref, w2v_ref, vc_ref)


def _compress(xk, xv, pe_k, w1_k, w2_k, pe_v, w1_v, w2_v):
    bg, nc, kd = xk.shape
    dh = w2_k.shape[1]
    hid = w2_k.shape[0]
    x_spec = pl.BlockSpec((None, nc, kd), lambda i: (i, 0, 0))
    full = lambda shape: pl.BlockSpec(shape, lambda i: (0,) * len(shape))
    o_spec = pl.BlockSpec((None, nc, dh), lambda i: (i, 0, 0))
    prep = lambda pe, w1, w2: (pe.reshape(1, kd), w1.reshape(kd, hid).astype(BF16), w2.astype(BF16))
    return pl.pallas_call(
        _compress_kernel,
        grid=(bg,),
        in_specs=[x_spec, x_spec, full((1, kd)), full((kd, hid)), full((hid, dh)),
                  full((1, kd)), full((kd, hid)), full((hid, dh))],
        out_specs=[o_spec, o_spec],
        out_shape=[jax.ShapeDtypeStruct((bg, nc, dh), BF16)] * 2,
        compiler_params=_cparams(("arbitrary",)),
        name="nsa_compress",
    )(xk, xv, *prep(pe_k, w1_k, w2_k), *prep(pe_v, w1_v, w2_v))


def _col_max(x):
    return jnp.max(x, axis=0, keepdims=True)


def _attn_kernel(qT_ref, kc_ref, vcT_ref, ks_ref, kw_ref, vsT_ref, vwT_ref, gT_ref,
                 tnear_ref, cfar_ref, cbias_ref, ovl_ref, o_ref,
                 qbd, sel_s, m_s, l_s, acc_s):
    i = pl.program_id(1)
    g_, hg, dh = N_KV_GROUPS, N_HEADS // N_KV_GROUPS, HEAD_DIM
    tq = qT_ref.shape[1]
    gw = hg * tq
    nw = g_ * gw
    n_sel = sel_s.shape[0]
    per = tq // SEL_BLOCK

    qT = (qT_ref[...].astype(F32) * (HEAD_DIM ** -0.5)).astype(BF16)
    zero_blk = jnp.zeros((dh, gw), BF16)
    rows = []
    for g in range(g_):
        qcat = jnp.concatenate([qT[(g * hg + h) * dh:(g * hg + h + 1) * dh, :] for h in range(hg)], axis=1)
        rows.append(jnp.concatenate([zero_blk] * g + [qcat] + [zero_blk] * (g_ - 1 - g), axis=1))
    qbd[...] = jnp.concatenate(rows, axis=0)

    def lane_q(shape):
        return lax.broadcasted_iota(jnp.int32, shape, 1) % tq

    def pv(vT_ref, start, nrows, p):
        pb = p.astype(BF16)
        outs = []
        for g in range(g_):
            if start is None:
                v = vT_ref[g * dh:(g + 1) * dh, :]
            else:
                v = vT_ref[g * dh:(g + 1) * dh, pl.ds(start, nrows)]
            outs.append(jnp.dot(v, pb[:, g * gw:(g + 1) * gw], preferred_element_type=F32))
        return jnp.concatenate(outs, axis=1)

    nc = kc_ref.shape[0]
    near_t = jnp.maximum(i - 1, 0)
    near_start = pl.multiple_of(near_t * tq, tq)
    far_end = near_t
    wr = WINDOW - tq
    win_start = pl.multiple_of(jnp.maximum(i - WINDOW // tq, 0) * tq, tq)
    lhs = jnp.concatenate([kc_ref[...], ks_ref[pl.ds(near_start, 2 * tq), :],
                           kw_ref[pl.ds(near_start, 2 * tq), :], kw_ref[pl.ds(win_start, wr), :]], axis=0)
    s_all = jnp.dot(lhs, qbd[...], preferred_element_type=F32)
    r_slc, r_win, r_far = nc, nc + 2 * tq, nc + 4 * tq

    sc = s_all[0:nc] + cbias_ref[...]
    nrow = lax.broadcasted_iota(jnp.int32, (nc, nw), 0)
    cmask = (nrow * CMP_STRIDE + (CMP_BLOCK - 1)) <= i * tq + lane_q((nc, nw))
    sc = jnp.where(cmask, sc, NEG)
    pc = jnp.where(cmask, jnp.exp(sc - _col_max(sc)), 0.0)
    lc = jnp.sum(pc, axis=0, keepdims=True)
    pc = pc * jnp.where(lc > 0.0, 1.0 / lc, 0.0)
    o_cmp = pv(vcT_ref, None, nc, pc)

    psum = jnp.concatenate(
        [sum(pc[:, g * gw + h * tq:g * gw + (h + 1) * tq] for h in range(hg)) for g in range(g_)], axis=1)
    imp = jnp.dot(ovl_ref[...], psum, preferred_element_type=F32, precision=lax.Precision.HIGHEST)
    jrow = lax.broadcasted_iota(jnp.int32, (n_sel, g_ * tq), 0)
    tq_abs = i * tq + lane_q((n_sel, g_ * tq))
    cur = tq_abs // SEL_BLOCK
    forced = (jrow == 0) | (jrow == cur) | (jrow == cur - 1)
    valid = jrow * SEL_BLOCK <= tq_abs
    work = jnp.where(forced, jnp.inf, jnp.where(valid, imp, -jnp.inf))
    jrow_f = jrow.astype(F32)
    sel = jnp.zeros((n_sel, g_ * tq), F32)
    for _ in range(min(SEL_TOP_N, n_sel)):
        mx = _col_max(work)
        first = jnp.min(jnp.where(work == mx, jrow_f, float(n_sel)), axis=0, keepdims=True)
        pick = jrow_f == first
        sel = jnp.where(pick, 1.0, sel)
        work = jnp.where(pick, -jnp.inf, work)
    sel_s[...] = jnp.where(sel > 0.5, 0.0, NEG)

    def sel_add(first_blk, n_blk, limit_blk=None, bias_row=None):
        parts = []
        for c in range(n_blk):
            rowv = sel_s[pl.ds(first_blk + c, 1), :]
            if limit_blk is not None:
                rowv = rowv + jnp.where(first_blk + c < limit_blk, 0.0, NEG)
            rowv = jnp.concatenate([rowv[:, g * tq:(g + 1) * tq] for g in range(g_) for _ in range(hg)], axis=1)
            if bias_row is not None:
                rowv = rowv + bias_row
            parts.append(jnp.broadcast_to(rowv, (SEL_BLOCK, nw)))
        return jnp.concatenate(parts, axis=0)

    def scores(k_ref, start, nrows):
        return jnp.dot(k_ref[pl.ds(start, nrows), :], qbd[...], preferred_element_type=F32)

    def flash_init(s, vT_ref, start, nrows):
        m = _col_max(s)
        p = jnp.exp(s - m)
        m_s[...] = m
        l_s[...] = jnp.sum(p, axis=0, keepdims=True)
        acc_s[...] = pv(vT_ref, start, nrows, p)

    def flash_update(s, vT_ref, start, nrows):
        m_old = m_s[...]
        m_new = jnp.maximum(m_old, _col_max(s))
        alpha = jnp.exp(m_old - m_new)
        p = jnp.exp(s - m_new)
        m_s[...] = m_new
        l_s[...] = alpha * l_s[...] + jnp.sum(p, axis=0, keepdims=True)
        acc_s[...] = alpha * acc_s[...] + pv(vT_ref, start, nrows, p)

    bias_off = pl.multiple_of(jnp.where(i == 0, tq, 0), tq)
    near_add = tnear_ref[pl.ds(bias_off, 2 * tq), :]

    flash_init(s_all[r_slc:r_win] + near_add + sel_add(near_t * per, 2 * per), vsT_ref, near_start, 2 * tq)
    fr = FAR_TILES * tq

    def far_slc(c, carry):
        hi_t = far_end - c * FAR_TILES
        st_t = jnp.maximum(hi_t - FAR_TILES, 0)
        start = pl.multiple_of(st_t * tq, tq)
        add = sel_add(st_t * per, FAR_TILES * per, hi_t * per, cfar_ref[...])
        flash_update(scores(ks_ref, start, fr) + add, vsT_ref, start, fr)
        return carry

    lax.fori_loop(0, (far_end + FAR_TILES - 1) // FAR_TILES, far_slc, 0)
    o_slc = acc_s[...] * (1.0 / l_s[...])

    flash_init(s_all[r_win:r_far] + near_add, vwT_ref, near_start, 2 * tq)
    nwin = WINDOW // tq
    parts = []
    for c in range(wr // tq):
        tile_ok = win_start // tq + c < far_end
        part = s_all[r_far + c * tq:r_far + (c + 1) * tq] + (cfar_ref[...] + jnp.where(tile_ok, 0.0, NEG))
        if c == 0:
            krow = lax.broadcasted_iota(jnp.int32, (tq, nw), 0)
            part = part + jnp.where(krow > lane_q((tq, nw)) - jnp.where(i < nwin, tq, 0), 0.0, NEG)
        parts.append(part)
    flash_update(jnp.concatenate(parts, axis=0), vwT_ref, win_start, wr)
    o_win = acc_s[...] * (1.0 / l_s[...])

    gate = _sigmoid(gT_ref[...])

    def grow(j):
        return jnp.concatenate([gate[h * 3 + j:h * 3 + j + 1, :] for h in range(N_HEADS)], axis=1)

    o_t = grow(0) * o_cmp + grow(1) * o_slc + grow(2) * o_win
    o_hd = jnp.concatenate([o_t[:, h * tq:(h + 1) * tq] for h in range(N_HEADS)], axis=0)
    o_ref[...] = o_hd.T.astype(o_ref.dtype)


def _t5_bucket_np(dist):
    n = np.maximum(dist, 0)
    max_exact = N_BUCKETS // 2
    nf = np.maximum(n, 1).astype(np.float64)
    large = max_exact + (np.log(nf / max_exact) / math.log(MAX_DISTANCE / max_exact)
                         * (N_BUCKETS - max_exact)).astype(np.int64)
    large = np.minimum(large, N_BUCKETS - 1)
    return np.where(n < max_exact, n, large).astype(np.int32)


def _bias_tables(rel_bias, seq):
    tq = min(TQ, seq)
    assert MAX_DISTANCE <= tq, "tiles two or more behind the diagonal must all fall in the last bucket"
    nh = N_HEADS
    nc = _round_up((seq - CMP_BLOCK) // CMP_STRIDE + 1, LANE)
    off_max = (nc - 1) * CMP_STRIDE + CMP_BLOCK - 1
    fd = rel_bias.astype(F32)[_t5_bucket_np(np.arange(-off_max, seq))]
    fdT = fd.T
    kk = np.arange(tq)[:, None]
    qq = np.arange(tq)[None, :]

    def toeplitz(offset):
        idx = off_max + np.maximum(offset + qq - kk, 0)
        return jnp.transpose(fd[idx], (0, 2, 1)).reshape(tq, nh * tq)

    future = jnp.asarray(np.tile(np.where(kk <= qq, 0.0, NEG).astype(np.float32), (1, nh)))
    tnear = jnp.concatenate([toeplitz(tq), toeplitz(0) + future, jnp.full((tq, nh * tq), NEG, F32)], axis=0)
    cfar = jnp.broadcast_to(rel_bias.astype(F32)[N_BUCKETS - 1][:, None], (nh, tq)).reshape(1, nh * tq)
    rows = [lax.slice_in_dim(fdT, off_max - (n * CMP_STRIDE + CMP_BLOCK - 1),
                             off_max - (n * CMP_STRIDE + CMP_BLOCK - 1) + seq, axis=1) for n in range(nc)]
    cb = jnp.stack(rows, axis=0).reshape(nc, nh, seq // tq, tq)
    cbias = jnp.transpose(cb, (2, 0, 1, 3)).reshape(seq // tq, nc, nh * tq)
    n_sel = seq // SEL_BLOCK
    cmp_start = np.arange(nc) * CMP_STRIDE
    sel_start = np.arange(n_sel) * SEL_BLOCK
    ovl = ((cmp_start[None, :] < sel_start[:, None] + SEL_BLOCK)
           & (cmp_start[None, :] + CMP_BLOCK > sel_start[:, None])).astype(np.float32)
    return tnear, cfar, cbias, jnp.asarray(ovl)


def _attention(qT, kc, vcT, zb3, ks_blk, kw_blk, vsT, vwT, gT, tables, batch, seq):
    tnear, cfar, cbias, ovl = tables
    g_, dh = N_KV_GROUPS, HEAD_DIM
    tq = min(TQ, seq)
    assert seq % tq == 0 and seq >= FAR_TILES * tq and WINDOW % tq == 0 and tq % SEL_BLOCK == 0
    nqt = seq // tq
    nw = N_HEADS * tq
    nc = kc.shape[1]
    n_sel = seq // SEL_BLOCK
    kvw = g_ * dh
    per_b = lambda shape: pl.BlockSpec((None,) + shape, lambda b, i: (b,) + (0,) * len(shape))
    full = lambda shape: pl.BlockSpec(shape, lambda b, i: (0,) * len(shape))
    return pl.pallas_call(
        _attn_kernel,
        grid=(batch, nqt),
        in_specs=[pl.BlockSpec((None, N_HEADS * dh, tq), lambda b, i: (b, 0, i)),
                  per_b((nc, kvw)),
                  per_b((kvw, nc)),
                  pl.BlockSpec((None, seq, kvw), lambda b, i: (b, 0, ks_blk)),
                  pl.BlockSpec((None, seq, kvw), lambda b, i: (b, 0, kw_blk)),
                  per_b((kvw, seq)), per_b((kvw, seq)),
                  pl.BlockSpec((None, 3 * N_HEADS, tq), lambda b, i: (b, 0, i)),
                  full((3 * tq, nw)), full((1, nw)),
                  pl.BlockSpec((None, nc, nw), lambda b, i: (i, 0, 0)),
                  full((n_sel, nc))],
        out_specs=pl.BlockSpec((None, tq, N_HEADS * dh), lambda b, i: (b, i, 0)),
        out_shape=jax.ShapeDtypeStruct((batch, seq, N_HEADS * dh), BF16),
        scratch_shapes=[pltpu.VMEM((kvw, nw), BF16), pltpu.VMEM((n_sel, g_ * tq), F32),
                        pltpu.VMEM((1, nw), F32), pltpu.VMEM((1, nw), F32), pltpu.VMEM((dh, nw), F32)],
        compiler_params=_cparams(("arbitrary", "arbitrary")),
        name="nsa_attention",
    )(qT, kc, vcT, zb3, zb3, vsT, vwT, gT, tnear, cfar, cbias, ovl)


def _merge_kernel(yr_ref, ya_ref, wr_ref, wa_ref, ma_ref, mb_ref, o_ref):
    pr = jnp.dot(yr_ref[...], wr_ref[...], preferred_element_type=F32)
    pa = jnp.dot(ya_ref[...], wa_ref[...], preferred_element_type=F32)
    o_ref[...] = (_sigmoid(ma_ref[...]) * pr + _sigmoid(mb_ref[...]) * pa).astype(o_ref.dtype)


def _merge(y_rnn, y_att, w_ur, w_ua, z2d, ma_blk, mb_blk):
    t, r = y_rnn.shape
    a = y_att.shape[1]
    d = w_ur.shape[1]
    tm = min(TM_MERGE, t)
    tn = min(TN_MERGE, d)
    nj = d // tn
    return pl.pallas_call(
        _merge_kernel,
        grid=(t // tm, nj),
        in_specs=[pl.BlockSpec((tm, r), lambda i, j: (i, 0)),
                  pl.BlockSpec((tm, a), lambda i, j: (i, 0)),
                  pl.BlockSpec((r, tn), lambda i, j: (0, j)),
                  pl.BlockSpec((a, tn), lambda i, j: (0, j)),
                  pl.BlockSpec((tm, tn), lambda i, j: (i, ma_blk * nj + j)),
                  pl.BlockSpec((tm, tn), lambda i, j: (i, mb_blk * nj + j))],
        out_specs=pl.BlockSpec((tm, tn), lambda i, j: (i, j)),
        out_shape=jax.ShapeDtypeStruct((t, d), BF16),
        compiler_params=_cparams(("arbitrary", "arbitrary")),
        name="merge",
    )(y_rnn, y_att, w_ur, w_ua, z2d, z2d)


def _outproj_kernel(mg_ref, w_ref, x_ref, g1_ref, nw_ref, sc_ref, sh_ref, x1_ref, h2_ref):
    x1 = x_ref[...] + g1_ref[...] * jnp.dot(mg_ref[...], w_ref[...], preferred_element_type=F32)
    x1_ref[...] = x1
    ms = jnp.mean(x1 * x1, axis=-1, keepdims=True)
    y = (x1 * lax.rsqrt(ms + EPS)) * nw_ref[...]
    h2_ref[...] = y * (1.0 + sc_ref[...]) + sh_ref[...]


def _outproj(merged, w_out, x2d, modr, mod_base, seq, norm_w):
    t, d = x2d.shape
    tm = min(TM_OUT, seq)

    def mod_idx(k):
        return lambda i: (mod_base + ((i * tm) // seq) * 6 + k, 0, 0)

    row_spec = pl.BlockSpec((tm, d), lambda i: (i, 0))
    return pl.pallas_call(
        _outproj_kernel,
        grid=(t // tm,),
        in_specs=[row_spec, pl.BlockSpec((d, d), lambda i: (0, 0)), row_spec,
                  pl.BlockSpec((None, 1, d), mod_idx(2)),
                  pl.BlockSpec((1, d), lambda i: (0, 0)),
                  pl.BlockSpec((None, 1, d), mod_idx(4)),
                  pl.BlockSpec((None, 1, d), mod_idx(3))],
        out_specs=[row_spec, row_spec],
        out_shape=[jax.ShapeDtypeStruct((t, d), F32)] * 2,
        compiler_params=_cparams(("arbitrary",)),
        name="outproj",
    )(merged, w_out, x2d, modr, norm_w.reshape(1, d), modr, modr)


def _router_kernel(h_ref, rw_ref, rb_ref, idx_ref, wt_ref, rank_ref, cnt_ref, carry):
    ne = rw_ref.shape[0]
    tm = h_ref.shape[0]
    per = ne // N_EXPERT_GROUPS
    assert per == SUBLANE, "one expert group per sublane tile"

    @pl.when(pl.program_id(0) == 0)
    def _():
        carry[...] = jnp.zeros_like(carry)

    logits = lax.dot_general(rw_ref[...], h_ref[...].astype(BF16), (((1,), (1,)), ((), ())),
                             preferred_element_type=F32)
    scores = _sigmoid(logits)
    biased = scores + rb_ref[...]
    erow = lax.broadcasted_iota(jnp.int32, (ne, tm), 0).astype(F32)
    grow = lax.broadcasted_iota(jnp.int32, (ne, tm), 0) // per

    gparts = []
    sub = lax.broadcasted_iota(jnp.int32, (per, tm), 0).astype(F32)
    for gi in range(N_EXPERT_GROUPS):
        xg = biased[gi * per:(gi + 1) * per, :]
        m1 = _col_max(xg)
        f1 = jnp.min(jnp.where(xg == m1, sub, float(per)), axis=0, keepdims=True)
        m2 = _col_max(jnp.where(sub == f1, -jnp.inf, xg))
        gparts.append(jnp.broadcast_to(m1 + m2, (per, tm)))
    gscore = jnp.concatenate(gparts, axis=0)

    kparts = []
    for gi in range(N_EXPERT_GROUPS):
        gs = gscore[gi * per:gi * per + 1, :]
        beats = (gscore > gs) | ((gscore == gs) & (grow < gi))
        nbeat = jnp.sum(beats.astype(F32), axis=0, keepdims=True)
        kparts.append(jnp.broadcast_to(nbeat < float(TOPK_EXPERT_GROUPS * per), (per, tm)))
    gkeep = jnp.concatenate(kparts, axis=0)

    work = jnp.where(gkeep, biased, -jnp.inf)
    picks = []
    chosen = jnp.zeros((ne, tm), F32)
    for _ in range(EXPERT_TOP_K):
        mx = _col_max(work)
        first = jnp.min(jnp.where(work == mx, erow, float(ne)), axis=0, keepdims=True)
        pick = erow == first
        picks.append(pick)
        chosen = jnp.where(pick, 1.0, chosen)
        work = jnp.where(pick, -jnp.inf, work)

    tri = (lax.broadcasted_iota(jnp.int32, (tm, tm), 0) < lax.broadcasted_iota(jnp.int32, (tm, tm), 1))
    before = jnp.dot(chosen.astype(BF16), tri.astype(BF16), preferred_element_type=F32)
    pos = before + carry[:, 0:1]
    new_carry = carry[:, 0:1] + jnp.sum(chosen, axis=1, keepdims=True)
    carry[...] = jnp.broadcast_to(new_carry, carry.shape)
    cnt_ref[...] = carry[...]

    krow = lax.broadcasted_iota(jnp.int32, (EXPERT_TOP_K, tm), 0)
    idx_o = jnp.zeros((EXPERT_TOP_K, tm), F32)
    wt_o = jnp.zeros((EXPERT_TOP_K, tm), F32)
    rk_o = jnp.zeros((EXPERT_TOP_K, tm), F32)
    for k, pick in enumerate(picks):
        sel = lambda v: jnp.sum(jnp.where(pick, v, 0.0), axis=0, keepdims=True)
        idx_o = jnp.where(krow == k, sel(erow), idx_o)
        wt_o = jnp.where(krow == k, sel(scores), wt_o)
        rk_o = jnp.where(krow == k, sel(pos), rk_o)
    wsum = jnp.sum(wt_o, axis=0, keepdims=True)
    idx_ref[...] = idx_o.astype(jnp.int32)
    wt_ref[...] = (ROUTED_SCALE * wt_o) / wsum
    rank_ref[...] = rk_o.astype(jnp.int32)


def _router(h2, router_w, router_bias):
    t, d = h2.shape
    ne = router_w.shape[1]
    tm = min(TM_ROUTE, t)
    k_spec = pl.BlockSpec((EXPERT_TOP_K, tm), lambda i: (0, i))
    return pl.pallas_call(
        _router_kernel,
        grid=(t // tm,),
        in_specs=[pl.BlockSpec((tm, d), lambda i: (i, 0)),
                  pl.BlockSpec((ne, d), lambda i: (0, 0)),
                  pl.BlockSpec((ne, 1), lambda i: (0, 0))],
        out_specs=[k_spec, k_spec, k_spec, pl.BlockSpec((ne, LANE), lambda i: (0, 0))],
        out_shape=[jax.ShapeDtypeStruct((EXPERT_TOP_K, t), jnp.int32),
                   jax.ShapeDtypeStruct((EXPERT_TOP_K, t), F32),
                   jax.ShapeDtypeStruct((EXPERT_TOP_K, t), jnp.int32),
                   jax.ShapeDtypeStruct((ne, LANE), F32)],
        scratch_shapes=[pltpu.VMEM((ne, LANE), F32)],
        compiler_params=_cparams(("arbitrary",)),
        name="moe_router",
    )(h2, router_w.T.astype(BF16), router_bias.reshape(ne, 1).astype(F32))


def _row_copy(src_ref, src_row, dst_ref, dst_row, sem):
    return pltpu.make_async_copy(src_ref.at[pl.ds(src_row, 1)], dst_ref.at[pl.ds(dst_row, 1)], sem)


def _pack_bf16_pairs(x):
    half = x.shape[1] // 2
    bits = lax.bitcast_convert_type(x.astype(BF16).astype(F32), jnp.uint32)
    return bits[:, :half] | (bits[:, half:] >> 16)


def _unpack_bf16_pairs(w):
    hi = lax.bitcast_convert_type(w & jnp.uint32(0xFFFF0000), F32)
    lo = lax.bitcast_convert_type(w << 16, F32)
    return hi, lo


def _row_tiles(words):
    return (words // LANE, LANE)


def _dispatch_kernel(start_ref, cnt_ref, dest_ref, h_ref, xs_ref, hp, zblk, sem):
    tm = h_ref.shape[0]
    ne = start_ref.shape[0] - 1
    rb = zblk.shape[0]
    n_blocks = xs_ref.shape[0] // rb
    hp[...] = _pack_bf16_pairs(h_ref[...]).reshape(hp.shape)

    def pad_rows(e, fn):
        lo = start_ref[e] + cnt_ref[e]
        lax.fori_loop(lo, start_ref[e + 1], lambda r, c: fn(_row_copy(zblk, 0, xs_ref, r, sem)) or c, 0)

    def tail_blocks(fn):
        def body(b, c):
            fn(pltpu.make_async_copy(zblk, xs_ref.at[pl.ds(pl.multiple_of(b * rb, rb), rb)], sem))
            return c
        lax.fori_loop(start_ref[ne] // rb, n_blocks, body, 0)

    @pl.when(pl.program_id(0) == 0)
    def _():
        zblk[...] = jnp.zeros_like(zblk)
        lax.fori_loop(0, ne, lambda e, c: pad_rows(e, lambda cp: cp.start()) or c, 0)
        tail_blocks(lambda cp: cp.start())
        lax.fori_loop(0, ne, lambda e, c: pad_rows(e, lambda cp: cp.wait()) or c, 0)
        tail_blocks(lambda cp: cp.wait())

    def token_rows(fn):
        def body(r, c):
            for k in range(EXPERT_TOP_K):
                fn(_row_copy(hp, r, xs_ref, dest_ref[k, r], sem), k)
            return c
        lax.fori_loop(0, tm, body, 0)

    token_rows(lambda cp, k: cp.start(priority=k % DMA_PRIORITIES))
    token_rows(lambda cp, k: cp.wait())


def _dispatch(starts, cnt, dest, h2, n_rows):
    t, d = h2.shape
    tm = min(TM_DISP, t)
    return pl.pallas_call(
        _dispatch_kernel,
        grid_spec=pltpu.PrefetchScalarGridSpec(
            num_scalar_prefetch=2,
            grid=(t // tm,),
            in_specs=[pl.BlockSpec((EXPERT_TOP_K, tm), lambda i, *_: (0, i), memory_space=pltpu.SMEM),
                      pl.BlockSpec((tm, d), lambda i, *_: (i, 0))],
            out_specs=pl.BlockSpec(memory_space=pl.ANY),
            scratch_shapes=[pltpu.VMEM((tm,) + _row_tiles(d // 2), jnp.uint32),
                            pltpu.VMEM((min(ROW_BLOCK, n_rows),) + _row_tiles(d // 2), jnp.uint32),
                            pltpu.SemaphoreType.DMA(())]),
        out_shape=jax.ShapeDtypeStruct((n_rows,) + _row_tiles(d // 2), jnp.uint32),
        compiler_params=_cparams(("arbitrary",)),
        name="moe_dispatch",
    )(starts, cnt, dest, h2)


def _mlp_kernel(be_ref, nb_ref, x_ref, wg_ref, wu_ref, wd_ref, y_ref, wg_s, wu_s, wd_s, *, packed):
    i = pl.program_id(0)

    @pl.when((i == 0) | (be_ref[i] != be_ref[jnp.maximum(i - 1, 0)]))
    def _():
        wg_s[...] = wg_ref[...].astype(BF16)
        wu_s[...] = wu_ref[...].astype(BF16)
        wd_s[...] = wd_ref[...].astype(BF16)

    @pl.when(i < nb_ref[0])
    def _():
        if packed:
            xw = x_ref[...]
            xw = xw.reshape(xw.shape[0], xw.shape[1] * xw.shape[2])
            x = jnp.concatenate(_unpack_bf16_pairs(xw), axis=1).astype(BF16)
        else:
            x = x_ref[...].astype(BF16)
        gt = jnp.dot(x, wg_s[...], preferred_element_type=F32)
        up = jnp.dot(x, wu_s[...], preferred_element_type=F32)
        hb = ((gt * _sigmoid(gt)) * up).astype(BF16)
        y = jnp.dot(hb, wd_s[...], preferred_element_type=F32)
        y_ref[...] = _pack_bf16_pairs(y).reshape(y_ref.shape) if packed else y

    @pl.when(i >= nb_ref[0])
    def _():
        y_ref[...] = jnp.zeros_like(y_ref)


def _grouped_mlp(block_expert, n_used, xs, layer, w_gate, w_up, w_down, packed, name):
    n_rows, row_shape = xs.shape[0], xs.shape[1:]
    zeros = (0,) * len(row_shape)
    d, hid = w_gate.shape[2], w_gate.shape[3]
    rb = min(ROW_BLOCK, n_rows)
    w_spec = lambda shape: pl.BlockSpec((None, None) + shape, lambda i, be, nb: (layer, be[i], 0, 0))
    return pl.pallas_call(
        functools.partial(_mlp_kernel, packed=packed),
        grid_spec=pltpu.PrefetchScalarGridSpec(
            num_scalar_prefetch=2,
            grid=(n_rows // rb,),
            in_specs=[pl.BlockSpec((rb,) + row_shape, lambda i, be, nb: (jnp.minimum(i, nb[0] - 1),) + zeros),
                      w_spec((d, hid)), w_spec((d, hid)), w_spec((hid, d))],
            out_specs=pl.BlockSpec((rb,) + row_shape, lambda i, be, nb: (i,) + zeros),
            scratch_shapes=[pltpu.VMEM((d, hid), BF16), pltpu.VMEM((d, hid), BF16), pltpu.VMEM((hid, d), BF16)]),
        out_shape=jax.ShapeDtypeStruct(xs.shape, xs.dtype),
        compiler_params=_cparams(("arbitrary",)),
        name=name,
    )(block_expert, n_used, xs, w_gate, w_up, w_down)


def _combine_kernel(dest_ref, wt_ref, ysh_ref, x_ref, g2_ref, fn_ref, ys_ref, o_ref, ybuf, sem, *, final):
    tm, d = x_ref.shape

    def token_rows(fn):
        def body(r, c):
            for k in range(EXPERT_TOP_K):
                fn(_row_copy(ys_ref, dest_ref[k, r], ybuf.at[k], r, sem), k)
            return c
        lax.fori_loop(0, tm, body, 0)

    token_rows(lambda cp, k: cp.start(priority=k % DMA_PRIORITIES))
    token_rows(lambda cp, k: cp.wait())

    wt = wt_ref[...]
    acc_hi = jnp.zeros((tm, d // 2), F32)
    acc_lo = jnp.zeros((tm, d // 2), F32)
    for k in range(EXPERT_TOP_K):
        hi, lo = _unpack_bf16_pairs(ybuf[k].reshape(tm, d // 2))
        acc_hi = acc_hi + wt[:, k:k + 1] * hi
        acc_lo = acc_lo + wt[:, k:k + 1] * lo
    acc = ysh_ref[...] + jnp.concatenate([acc_hi, acc_lo], axis=1)
    xn = x_ref[...] + g2_ref[...] * acc
    if final:
        ms = jnp.mean(xn * xn, axis=-1, keepdims=True)
        xn = (xn * lax.rsqrt(ms + EPS)) * fn_ref[...]
    o_ref[...] = xn


def _combine(dest, wts_t, y_sorted, y_shared, x2d, modr, mod_base, seq, final_norm, final):
    t, d = x2d.shape
    tm = min(TM_COMB, seq)
    row_spec = pl.BlockSpec((tm, d), lambda i: (i, 0))
    return pl.pallas_call(
        functools.partial(_combine_kernel, final=final),
        grid=(t // tm,),
        in_specs=[pl.BlockSpec((EXPERT_TOP_K, tm), lambda i: (0, i), memory_space=pltpu.SMEM),
                  pl.BlockSpec((tm, EXPERT_TOP_K), lambda i: (i, 0)),
                  row_spec, row_spec,
                  pl.BlockSpec((None, 1, d), lambda i: (mod_base + ((i * tm) // seq) * 6 + 5, 0, 0)),
                  pl.BlockSpec((1, d), lambda i: (0, 0)),
                  pl.BlockSpec(memory_space=pl.ANY)],
        out_specs=row_spec,
        out_shape=jax.ShapeDtypeStruct((t, d), F32),
        scratch_shapes=[pltpu.VMEM((EXPERT_TOP_K, tm) + _row_tiles(d // 2), jnp.uint32),
                        pltpu.SemaphoreType.DMA(())],
        compiler_params=_cparams(("arbitrary",)),
        name="moe_combine",
    )(dest, wts_t, y_shared, x2d, modr, final_norm.reshape(1, d), y_sorted)


def _layout(d_model):
    r, a, kvw = RNN_WIDTH, N_HEADS * HEAD_DIM, N_KV_GROUPS * HEAD_DIM
    off = {}
    off["ma"], off["mb"] = 0, d_model
    off["u"] = 2 * d_model
    off["g"] = off["u"] + r
    off["kc"] = off["g"] + r
    off["vc"] = off["kc"] + kvw
    off["nf"] = off["vc"] + kvw
    off["q"] = 0
    off["ks"], off["vs"], off["kw"], off["vw"] = a, a + kvw, a + 2 * kvw, a + 3 * kvw
    off["gn"] = a + 4 * kvw
    off["nb"] = off["gn"] + TN_IN
    assert off["nf"] % TN_IN == 0 and off["gn"] % TN_IN == 0
    return off


def _pack_w_in(w_in_l, d_model):
    r, a, kvw = RNN_WIDTH, N_HEADS * HEAD_DIM, N_KV_GROUPS * HEAD_DIM
    off = _layout(d_model)
    s_u, s_g, s_q = 0, r, 2 * r
    s_kv = s_q + a
    s_gn = s_kv + 6 * kvw
    s_ma = s_gn + 3 * N_HEADS
    s_mb = s_ma + d_model
    cols = [w_in_l[:, s_ma:s_mb], w_in_l[:, s_mb:s_mb + d_model], w_in_l[:, s_u:s_g], w_in_l[:, s_g:s_q],
            w_in_l[:, s_kv:s_kv + 2 * kvw],
            w_in_l[:, s_q:s_kv], w_in_l[:, s_kv + 2 * kvw:s_gn], w_in_l[:, s_gn:s_ma]]
    w = jnp.concatenate(cols, axis=1)
    return jnp.pad(w, ((0, 0), (0, off["nf"] + off["nb"] - w.shape[1]))).astype(BF16)


def _moe_plan(idx, rank, counts, n_tok):
    cnt = counts[:, 0].astype(jnp.int32)
    padded = (cnt + ROW_BLOCK - 1) // ROW_BLOCK * ROW_BLOCK
    ends = jnp.cumsum(padded)
    starts = jnp.concatenate([jnp.zeros((1,), jnp.int32), ends]).astype(jnp.int32)
    onehot = idx[:, :, None] == jnp.arange(N_EXPERTS, dtype=jnp.int32)[None, None, :]
    dest = rank + jnp.sum(jnp.where(onehot, starts[None, None, :N_EXPERTS], 0), axis=2)
    n_blocks = (n_tok * EXPERT_TOP_K) // ROW_BLOCK + N_EXPERTS
    blk_start = jnp.arange(n_blocks, dtype=jnp.int32) * ROW_BLOCK
    owner = jnp.sum((ends[None, :] <= blk_start[:, None]).astype(jnp.int32), axis=1)
    block_expert = jnp.minimum(owner, N_EXPERTS - 1).astype(jnp.int32)
    n_used = (ends[-1] // ROW_BLOCK).astype(jnp.int32).reshape(1)
    return starts, cnt, dest.astype(jnp.int32), block_expert, n_used, n_blocks * ROW_BLOCK


def kernel(x, c, rel_bias, final_norm, ada_w, ada_b, norm_mix, norm_ffn, w_in, conv_w, conv_b, lru_wa, lru_ba, lru_wx, lru_bx, lru_lambda, cmp_pe_k, cmp_w1_k, cmp_w2_k, cmp_pe_v, cmp_w1_v, cmp_w2_v, w_up_rnn, w_up_att, w_out, router_w, router_bias, exp_w_gate, exp_w_up, exp_w_down, sh_w_gate, sh_w_up, sh_w_down):
    batch, seq, d = x.shape
    n_tok = batch * seq
    depth = ada_w.shape[0]
    g_, dh = N_KV_GROUPS, HEAD_DIM
    a_w, kvw, r = N_HEADS * HEAD_DIM, N_KV_GROUPS * HEAD_DIM, RNN_WIDTH
    off = _layout(d)
    assert seq % CMP_STRIDE == 0 and CMP_BLOCK == 2 * CMP_STRIDE

    mod = _adaln_mod(c, ada_w, ada_b)
    modr = mod.reshape(depth * batch * 6, 1, d)
    tables = _bias_tables(rel_bias, seq)
    nc = tables[2].shape[1]
    n_chunk = seq // CMP_STRIDE

    x2d = x.reshape(n_tok, d)
    for l in range(depth):
        mod_base = l * batch * 6
        zf, zb, zg = _inproj(x2d, norm_mix[l], modr, mod_base, seq, _pack_w_in(w_in[l], d), off["nf"])
        y_rnn = _rglru(zf, batch, seq, off["u"] // r, off["g"] // r, conv_w[l], conv_b[l],
                       lru_wa[l], lru_ba[l], lru_wx[l], lru_bx[l], lru_lambda[l])

        zb3 = zb.reshape(batch, seq, -1)
        zcol = lambda z3, name, w: z3[:, :, off[name]:off[name] + w]

        def unfold(v):
            ch = jnp.transpose(v.reshape(batch, n_chunk, CMP_STRIDE, g_, dh), (0, 3, 1, 2, 4))
            ch = ch.reshape(batch * g_, n_chunk, CMP_STRIDE * dh)
            blocks = jnp.concatenate([ch[:, :-1], ch[:, 1:]], axis=2)
            return jnp.pad(blocks, ((0, 0), (0, nc - (n_chunk - 1)), (0, 0)))

        zf3 = zf.reshape(batch, seq, -1)
        kc, vc = _compress(unfold(zcol(zf3, "kc", kvw)), unfold(zcol(zf3, "vc", kvw)),
                           cmp_pe_k[l], cmp_w1_k[l], cmp_w2_k[l], cmp_pe_v[l], cmp_w1_v[l], cmp_w2_v[l])
        kc = jnp.transpose(kc.reshape(batch, g_, nc, dh), (0, 2, 1, 3)).reshape(batch, nc, kvw)
        vcT = jnp.swapaxes(vc.reshape(batch, g_, nc, dh), 2, 3).reshape(batch, kvw, nc)
        qT = jnp.swapaxes(zcol(zb3, "q", a_w), 1, 2)
        vsT = jnp.swapaxes(zcol(zb3, "vs", kvw), 1, 2)
        vwT = jnp.swapaxes(zcol(zb3, "vw", kvw), 1, 2)
        gT = jnp.swapaxes(zg.reshape(batch, seq, LANE)[:, :, :3 * N_HEADS], 1, 2)
        y_att = _attention(qT, kc, vcT, zb3, off["ks"] // kvw, off["kw"] // kvw, vsT, vwT, gT, tables,
                           batch, seq).reshape(n_tok, a_w)

        merged = _merge(y_rnn, y_att, w_up_rnn[l].astype(BF16), w_up_att[l].astype(BF16), zf,
                        off["ma"] // d, off["mb"] // d)
        x1, h2 = _outproj(merged, w_out[l].astype(BF16), x2d, modr, mod_base, seq, norm_ffn[l])

        idx, wts, rank, counts = _router(h2, router_w[l], router_bias[l])
        starts, cnt, dest, block_expert, n_used, n_rows = _moe_plan(idx, rank, counts, n_tok)
        xs = _dispatch(starts, cnt, dest, h2, n_rows)
        y_sorted = _grouped_mlp(block_expert, n_used, xs, l, exp_w_gate, exp_w_up, exp_w_down, True, "moe_experts")
        sh_blocks = n_tok // min(ROW_BLOCK, n_tok)
        y_shared = _grouped_mlp(jnp.zeros((sh_blocks,), jnp.int32), jnp.full((1,), sh_blocks, jnp.int32), h2, l,
                                sh_w_gate[:, None], sh_w_up[:, None], sh_w_down[:, None], False, "moe_shared")
        x2d = _combine(dest, wts.T, y_sorted, y_shared, x1, modr, mod_base, seq, final_norm, l == depth - 1)
    return x2d.reshape(batch, seq, d)
```

```python
import functools
import math

import numpy as np
import jax
import jax.numpy as jnp
from jax import lax
from jax.experimental import pallas as pl
from jax.experimental.pallas import tpu as pltpu

DEPTH = 2
RNN_WIDTH = 1024
RNN_BLOCKS = 8
CONV_WIDTH = 4
LRU_C = 8.0
N_HEADS = 16
N_KV_GROUPS = 4
HEAD_DIM = 64
CMP_BLOCK = 32
CMP_STRIDE = 16
CMP_HIDDEN = 128
SEL_BLOCK = 64
SEL_TOP_N = 8
WINDOW = 512
N_BUCKETS = 32
MAX_DISTANCE = 128
N_EXPERTS = 64
EXPERT_TOP_K = 8
N_EXPERT_GROUPS = 8
TOPK_EXPERT_GROUPS = 4
ROUTED_SCALE = 2.5
EPS = 1e-6
NEG = -1e30

LANE = 128
SUBLANE = 8
VMEM_LIMIT = 52 * 1024 * 1024
DMA_PRIORITIES = 2

TM_IN = 1024
TN_IN = 512
TN_MOD = 1024
TC_RNN = 256
TQ = 128
FAR_TILES = 4
TM_MERGE = 512
TN_MERGE = 2048
TM_OUT = 256
TM_ROUTE = 256
TM_DISP = 128
ROW_BLOCK = 256
TM_COMB = 128

F32 = jnp.float32
BF16 = jnp.bfloat16


def _cparams(sem):
    return pltpu.CompilerParams(dimension_semantics=sem, vmem_limit_bytes=VMEM_LIMIT)


def _round_up(a, b):
    return (a + b - 1) // b * b


def _tile(n, pref):
    if n <= pref:
        return n
    t = pref // LANE * LANE
    while n % t:
        t -= LANE
    return t


def _gelu_tanh(x):
    return x * (0.5 * (1.0 + jnp.tanh(math.sqrt(2.0 / math.pi) * (x + 0.044715 * (x * x * x)))))


def _sigmoid(x):
    return jax.nn.sigmoid(x)


def _mod_kernel(c_ref, w_ref, b_ref, o_ref):
    c = c_ref[...]
    ca = (c * _sigmoid(c)).astype(BF16)
    o_ref[...] = jnp.dot(ca, w_ref[...].astype(BF16), preferred_element_type=F32) + b_ref[...]


def _adaln_mod(c, ada_w, ada_b):
    nl, d, n6 = ada_w.shape
    b = c.shape[0]
    tn = _tile(n6, TN_MOD)
    return pl.pallas_call(
        _mod_kernel,
        grid=(nl, n6 // tn),
        in_specs=[pl.BlockSpec((b, d), lambda l, j: (0, 0)),
                  pl.BlockSpec((None, d, tn), lambda l, j: (l, 0, j)),
                  pl.BlockSpec((None, 1, tn), lambda l, j: (l, 0, j))],
        out_specs=pl.BlockSpec((None, b, tn), lambda l, j: (l, 0, j)),
        out_shape=jax.ShapeDtypeStruct((nl, b, n6), F32),
        compiler_params=_cparams(("arbitrary", "arbitrary")),
        name="adaln_mod",
    )(c, ada_w, ada_b.reshape(nl, 1, n6))


def _inproj_kernel(x_ref, nw_ref, sc_ref, sh_ref, w_ref, zf_ref, zb_ref, zg_ref, h_ref, *, nf, nj):
    j = pl.program_id(1)

    @pl.when(j == 0)
    def _():
        x = x_ref[...]
        ms = jnp.mean(x * x, axis=-1, keepdims=True)
        y = (x * lax.rsqrt(ms + EPS)) * nw_ref[...]
        h_ref[...] = (y * (1.0 + sc_ref[...]) + sh_ref[...]).astype(BF16)

    res = jnp.dot(h_ref[...], w_ref[...], preferred_element_type=F32)

    @pl.when(j < nf)
    def _():
        zf_ref[...] = res

    @pl.when(j >= nf)
    def _():
        zb_ref[...] = res.astype(BF16)

    @pl.when(j == nj - 1)
    def _():
        zg_ref[...] = res[:, 0:LANE]


def _inproj(x2d, norm_w, modr, mod_base, seq, w_p, n_f32):
    t, d = x2d.shape
    n_p = w_p.shape[1]
    tm = min(TM_IN, seq)
    tn = TN_IN
    assert n_f32 % tn == 0 and n_p % tn == 0
    nf, nj = n_f32 // tn, n_p // tn

    def mod_idx(k):
        return lambda i, j: (mod_base + ((i * tm) // seq) * 6 + k, 0, 0)

    return pl.pallas_call(
        functools.partial(_inproj_kernel, nf=nf, nj=nj),
        grid=(t // tm, nj),
        in_specs=[pl.BlockSpec((tm, d), lambda i, j: (i, 0)),
                  pl.BlockSpec((1, d), lambda i, j: (0, 0)),
                  pl.BlockSpec((None, 1, d), mod_idx(1)),
                  pl.BlockSpec((None, 1, d), mod_idx(0)),
                  pl.BlockSpec((d, tn), lambda i, j: (0, j))],
        out_specs=[pl.BlockSpec((tm, tn), lambda i, j: (i, jnp.minimum(j, nf - 1))),
                   pl.BlockSpec((tm, tn), lambda i, j: (i, jnp.maximum(j - nf, 0))),
                   pl.BlockSpec((tm, LANE), lambda i, j: (i, 0))],
        out_shape=[jax.ShapeDtypeStruct((t, n_f32), F32), jax.ShapeDtypeStruct((t, n_p - n_f32), BF16),
                   jax.ShapeDtypeStruct((t, LANE), F32)],
        scratch_shapes=[pltpu.VMEM((tm, d), BF16)],
        compiler_params=_cparams(("arbitrary", "arbitrary")),
        name="inproj",
    )(x2d, norm_w.reshape(1, d), modr, modr, w_p)


def _rglru_kernel(u_ref, g_ref, cw_ref, cb_ref, wa_ref, ba_ref, wx_ref, bx_ref, lam_ref, y_ref,
                  ubuf, a_s, b_s, h_s, hcar):
    tc, r = u_ref.shape
    nb = wa_ref.shape[0]
    bw = r // nb

    @pl.when(pl.program_id(1) == 0)
    def _():
        ubuf[0:SUBLANE, :] = jnp.zeros((SUBLANE, r), F32)
        hcar[...] = jnp.zeros_like(hcar)

    ubuf[SUBLANE:SUBLANE + tc, :] = u_ref[...]
    cw = cw_ref[...]
    uc = cb_ref[...] + cw[CONV_WIDTH - 1:CONV_WIDTH, :] * ubuf[SUBLANE:SUBLANE + tc, :]
    for k in range(CONV_WIDTH - 1):
        off = SUBLANE - (CONV_WIDTH - 1) + k
        uc = uc + cw[k:k + 1, :] * ubuf[off:off + tc, :]
    ubuf[0:SUBLANE, :] = ubuf[tc:tc + SUBLANE, :]

    ucb = uc.astype(BF16)
    rp = []
    xp = []
    for n in range(nb):
        blk = ucb[:, n * bw:(n + 1) * bw]
        rp.append(jnp.dot(blk, wa_ref[n], preferred_element_type=F32))
        xp.append(jnp.dot(blk, wx_ref[n], preferred_element_type=F32))
    rg = _sigmoid(jnp.concatenate(rp, axis=1) + ba_ref[...])
    ig = _sigmoid(jnp.concatenate(xp, axis=1) + bx_ref[...])
    nl = -lam_ref[...]
    sp = jnp.maximum(nl, 0.0) + jnp.log1p(jnp.exp(-jnp.abs(nl)))
    log_a = (-LRU_C * rg) * sp
    a_s[...] = jnp.exp(log_a)
    th = jnp.tanh(log_a)
    one_minus_a2 = (-2.0 * th) / (1.0 - th)
    b_s[...] = jnp.sqrt(one_minus_a2) * (ig * uc)

    def step(t, h):
        h = a_s[pl.ds(t, 1), :] * h + b_s[pl.ds(t, 1), :]
        h_s[pl.ds(t, 1), :] = h
        return h

    h_last = lax.fori_loop(0, tc, step, hcar[0:1, :], unroll=8)
    hcar[0:1, :] = h_last
    y_ref[...] = (_gelu_tanh(g_ref[...]) * h_s[...]).astype(y_ref.dtype)


def _rglru(z2d, batch, seq, u_blk, g_blk, conv_w, conv_b, wa, ba, wx, bx, lam):
    r = conv_w.shape[1]
    tc = min(TC_RNN, seq)
    nt = seq // tc
    nb, bw, _ = wa.shape
    row = lambda v: v.reshape(1, r)
    full = lambda shape: pl.BlockSpec(shape, lambda b, t: (0,) * len(shape))
    return pl.pallas_call(
        _rglru_kernel,
        grid=(batch, nt),
        in_specs=[pl.BlockSpec((tc, r), lambda b, t: (b * nt + t, u_blk)),
                  pl.BlockSpec((tc, r), lambda b, t: (b * nt + t, g_blk)),
                  full((CONV_WIDTH, r)), full((1, r)),
                  full((nb, bw, bw)), full((1, r)),
                  full((nb, bw, bw)), full((1, r)), full((1, r))],
        out_specs=pl.BlockSpec((tc, r), lambda b, t: (b * nt + t, 0)),
        out_shape=jax.ShapeDtypeStruct((batch * seq, r), BF16),
        scratch_shapes=[pltpu.VMEM((tc + SUBLANE, r), F32), pltpu.VMEM((tc, r), F32),
                        pltpu.VMEM((tc, r), F32), pltpu.VMEM((tc, r), F32),
                        pltpu.VMEM((SUBLANE, r), F32)],
        compiler_params=_cparams(("arbitrary", "arbitrary")),
        name="rglru",
    )(z2d, z2d, conv_w, row(conv_b), wa.astype(BF16), row(ba), wx.astype(BF16), row(bx), row(lam))


def _compress_kernel(xk_ref, xv_ref, pek_ref, w1k_ref, w2k_ref, pev_ref, w1v_ref, w2v_ref, kc_ref, vc_ref):
    def one(x_ref, pe_ref, w1_ref, w2_ref, o_ref):
        blocks = (x_ref[...] + pe_ref[...]).astype(BF16)
        hid = _gelu_tanh(jnp.dot(blocks, w1_ref[...], preferred_element_type=F32))
        o_ref[...] = jnp.dot(hid.astype(BF16), w2_ref[...], preferred_element_type=F32).astype(o_ref.dtype)

    one(xk_ref, pek_ref, w1k_ref, w2k_ref, kc_ref)
    one(xv_ref, pev_ref, w1v_ref, w2v_ref, vc_ref)


def _compress(xk, xv, pe_k, w1_k, w2_k, pe_v, w1_v, w2_v):
    bg, nc, kd = xk.shape
    dh = w2_k.shape[1]
    hid = w2_k.shape[0]
    x_spec = pl.BlockSpec((None, nc, kd), lambda i: (i, 0, 0))
    full = lambda shape: pl.BlockSpec(shape, lambda i: (0,) * len(shape))
    o_spec = pl.BlockSpec((None, nc, dh), lambda i: (i, 0, 0))
    prep = lambda pe, w1, w2: (pe.reshape(1, kd), w1.reshape(kd, hid).astype(BF16), w2.astype(BF16))
    return pl.pallas_call(
        _compress_kernel,
        grid=(bg,),
        in_specs=[x_spec, x_spec, full((1, kd)), full((kd, hid)), full((hid, dh)),
                  full((1, kd)), full((kd, hid)), full((hid, dh))],
        out_specs=[o_spec, o_spec],
        out_shape=[jax.ShapeDtypeStruct((bg, nc, dh), BF16)] * 2,
        compiler_params=_cparams(("arbitrary",)),
        name="nsa_compress",
    )(xk, xv, *prep(pe_k, w1_k, w2_k), *prep(pe_v, w1_v, w2_v))


def _col_max(x):
    return jnp.max(x, axis=0, keepdims=True)


def _attn_kernel(qT_ref, kc_ref, vcT_ref, ks_ref, kw_ref, vsT_ref, vwT_ref, gT_ref,
                 tnear_ref, cfar_ref, cbias_ref, ovl_ref, o_ref,
                 qbd, sel_s, m_s, l_s, acc_s):
    i = pl.program_id(1)
    g_, hg, dh = N_KV_GROUPS, N_HEADS // N_KV_GROUPS, HEAD_DIM
    tq = qT_ref.shape[1]
    gw = hg * tq
    nw = g_ * gw
    n_sel = sel_s.shape[0]
    per = tq // SEL_BLOCK

    qT = (qT_ref[...].astype(F32) * (HEAD_DIM ** -0.5)).astype(BF16)
    zero_blk = jnp.zeros((dh, gw), BF16)
    rows = []
    for g in range(g_):
        qcat = jnp.concatenate([qT[(g * hg + h) * dh:(g * hg + h + 1) * dh, :] for h in range(hg)], axis=1)
        rows.append(jnp.concatenate([zero_blk] * g + [qcat] + [zero_blk] * (g_ - 1 - g), axis=1))
    qbd[...] = jnp.concatenate(rows, axis=0)

    def lane_q(shape):
        return lax.broadcasted_iota(jnp.int32, shape, 1) % tq

    def pv(vT_ref, start, nrows, p):
        pb = p.astype(BF16)
        outs = []
        for g in range(g_):
            if start is None:
                v = vT_ref[g * dh:(g + 1) * dh, :]
            else:
                v = vT_ref[g * dh:(g + 1) * dh, pl.ds(start, nrows)]
            outs.append(jnp.dot(v, pb[:, g * gw:(g + 1) * gw], preferred_element_type=F32))
        return jnp.concatenate(outs, axis=1)

    nc = kc_ref.shape[0]
    near_t = jnp.maximum(i - 1, 0)
    near_start = pl.multiple_of(near_t * tq, tq)
    far_end = near_t
    wr = WINDOW - tq
    win_start = pl.multiple_of(jnp.maximum(i - WINDOW // tq, 0) * tq, tq)
    lhs = jnp.concatenate([kc_ref[...], ks_ref[pl.ds(near_start, 2 * tq), :],
                           kw_ref[pl.ds(near_start, 2 * tq), :], kw_ref[pl.ds(win_start, wr), :]], axis=0)
    s_all = jnp.dot(lhs, qbd[...], preferred_element_type=F32)
    r_slc, r_win, r_far = nc, nc + 2 * tq, nc + 4 * tq

    sc = s_all[0:nc] + cbias_ref[...]
    nrow = lax.broadcasted_iota(jnp.int32, (nc, nw), 0)
    cmask = (nrow * CMP_STRIDE + (CMP_BLOCK - 1)) <= i * tq + lane_q((nc, nw))
    sc = jnp.where(cmask, sc, NEG)
    pc = jnp.where(cmask, jnp.exp(sc - _col_max(sc)), 0.0)
    lc = jnp.sum(pc, axis=0, keepdims=True)
    pc = pc * jnp.where(lc > 0.0, 1.0 / lc, 0.0)
    o_cmp = pv(vcT_ref, None, nc, pc)

    psum = jnp.concatenate(
        [sum(pc[:, g * gw + h * tq:g * gw + (h + 1) * tq] for h in range(hg)) for g in range(g_)], axis=1)
    imp = jnp.dot(ovl_ref[...], psum, preferred_element_type=F32, precision=lax.Precision.HIGHEST)
    jrow = lax.broadcasted_iota(jnp.int32, (n_sel, g_ * tq), 0)
    tq_abs = i * tq + lane_q((n_sel, g_ * tq))
    cur = tq_abs // SEL_BLOCK
    forced = (jrow == 0) | (jrow == cur) | (jrow == cur - 1)
    valid = jrow * SEL_BLOCK <= tq_abs
    work = jnp.where(forced, jnp.inf, jnp.where(valid, imp, -jnp.inf))
    jrow_f = jrow.astype(F32)
    sel = jnp.zeros((n_sel, g_ * tq), F32)
    for _ in range(min(SEL_TOP_N, n_sel)):
        mx = _col_max(work)
        first = jnp.min(jnp.where(work == mx, jrow_f, float(n_sel)), axis=0, keepdims=True)
        pick = jrow_f == first
        sel = jnp.where(pick, 1.0, sel)
        work = jnp.where(pick, -jnp.inf, work)
    sel_s[...] = jnp.where(sel > 0.5, 0.0, NEG)

    def sel_add(first_blk, n_blk, limit_blk=None, bias_row=None):
        parts = []
        for c in range(n_blk):
            rowv = sel_s[pl.ds(first_blk + c, 1), :]
            if limit_blk is not None:
                rowv = rowv + jnp.where(first_blk + c < limit_blk, 0.0, NEG)
            rowv = jnp.concatenate([rowv[:, g * tq:(g + 1) * tq] for g in range(g_) for _ in range(hg)], axis=1)
            if bias_row is not None:
                rowv = rowv + bias_row
            parts.append(jnp.broadcast_to(rowv, (SEL_BLOCK, nw)))
        return jnp.concatenate(parts, axis=0)

    def scores(k_ref, start, nrows):
        return jnp.dot(k_ref[pl.ds(start, nrows), :], qbd[...], preferred_element_type=F32)

    def flash_init(s, vT_ref, start, nrows):
        m = _col_max(s)
        p = jnp.exp(s - m)
        m_s[...] = m
        l_s[...] = jnp.sum(p, axis=0, keepdims=True)
        acc_s[...] = pv(vT_ref, start, nrows, p)

    def flash_update(s, vT_ref, start, nrows):
        m_old = m_s[...]
        m_new = jnp.maximum(m_old, _col_max(s))
        alpha = jnp.exp(m_old - m_new)
        p = jnp.exp(s - m_new)
        m_s[...] = m_new
        l_s[...] = alpha * l_s[...] + jnp.sum(p, axis=0, keepdims=True)
        acc_s[...] = alpha * acc_s[...] + pv(vT_ref, start, nrows, p)

    bias_off = pl.multiple_of(jnp.where(i == 0, tq, 0), tq)
    near_add = tnear_ref[pl.ds(bias_off, 2 * tq), :]

    flash_init(s_all[r_slc:r_win] + near_add + sel_add(near_t * per, 2 * per), vsT_ref, near_start, 2 * tq)
    fr = FAR_TILES * tq

    def far_slc(c, carry):
        hi_t = far_end - c * FAR_TILES
        st_t = jnp.maximum(hi_t - FAR_TILES, 0)
        start = pl.multiple_of(st_t * tq, tq)
        add = sel_add(st_t * per, FAR_TILES * per, hi_t * per, cfar_ref[...])
        flash_update(scores(ks_ref, start, fr) + add, vsT_ref, start, fr)
        return carry

    lax.fori_loop(0, (far_end + FAR_TILES - 1) // FAR_TILES, far_slc, 0)
    o_slc = acc_s[...] * (1.0 / l_s[...])

    flash_init(s_all[r_win:r_far] + near_add, vwT_ref, near_start, 2 * tq)
    nwin = WINDOW // tq
    parts = []
    for c in range(wr // tq):
        tile_ok = win_start // tq + c < far_end
        part = s_all[r_far + c * tq:r_far + (c + 1) * tq] + (cfar_ref[...] + jnp.where(tile_ok, 0.0, NEG))
        if c == 0:
            krow = lax.broadcasted_iota(jnp.int32, (tq, nw), 0)
            part = part + jnp.where(krow > lane_q((tq, nw)) - jnp.where(i < nwin, tq, 0), 0.0, NEG)
        parts.append(part)
    flash_update(jnp.concatenate(parts, axis=0), vwT_ref, win_start, wr)
    o_win = acc_s[...] * (1.0 / l_s[...])

    gate = _sigmoid(gT_ref[...])

    def grow(j):
        return jnp.concatenate([gate[h * 3 + j:h * 3 + j + 1, :] for h in range(N_HEADS)], axis=1)

    o_t = grow(0) * o_cmp + grow(1) * o_slc + grow(2) * o_win
    o_hd = jnp.concatenate([o_t[:, h * tq:(h + 1) * tq] for h in range(N_HEADS)], axis=0)
    o_ref[...] = o_hd.T.astype(o_ref.dtype)


def _t5_bucket_np(dist):
    n = np.maximum(dist, 0)
    max_exact = N_BUCKETS // 2
    nf = np.maximum(n, 1).astype(np.float64)
    large = max_exact + (np.log(nf / max_exact) / math.log(MAX_DISTANCE / max_exact)
                         * (N_BUCKETS - max_exact)).astype(np.int64)
    large = np.minimum(large, N_BUCKETS - 1)
    return np.where(n < max_exact, n, large).astype(np.int32)


def _bias_tables(rel_bias, seq):
    tq = min(TQ, seq)
    assert MAX_DISTANCE <= tq, "tiles two or more behind the diagonal must all fall in the last bucket"
    nh = N_HEADS
    nc = _round_up((seq - CMP_BLOCK) // CMP_STRIDE + 1, LANE)
    off_max = (nc - 1) * CMP_STRIDE + CMP_BLOCK - 1
    fd = rel_bias.astype(F32)[_t5_bucket_np(np.arange(-off_max, seq))]
    fdT = fd.T
    kk = np.arange(tq)[:, None]
    qq = np.arange(tq)[None, :]

    def toeplitz(offset):
        idx = off_max + np.maximum(offset + qq - kk, 0)
        return jnp.transpose(fd[idx], (0, 2, 1)).reshape(tq, nh * tq)

    future = jnp.asarray(np.tile(np.where(kk <= qq, 0.0, NEG).astype(np.float32), (1, nh)))
    tnear = jnp.concatenate([toeplitz(tq), toeplitz(0) + future, jnp.full((tq, nh * tq), NEG, F32)], axis=0)
    cfar = jnp.broadcast_to(rel_bias.astype(F32)[N_BUCKETS - 1][:, None], (nh, tq)).reshape(1, nh * tq)
    rows = [lax.slice_in_dim(fdT, off_max - (n * CMP_STRIDE + CMP_BLOCK - 1),
                             off_max - (n * CMP_STRIDE + CMP_BLOCK - 1) + seq, axis=1) for n in range(nc)]
    cb = jnp.stack(rows, axis=0).reshape(nc, nh, seq // tq, tq)
    cbias = jnp.transpose(cb, (2, 0, 1, 3)).reshape(seq // tq, nc, nh * tq)
    n_sel = seq // SEL_BLOCK
    cmp_start = np.arange(nc) * CMP_STRIDE
    sel_start = np.arange(n_sel) * SEL_BLOCK
    ovl = ((cmp_start[None, :] < sel_start[:, None] + SEL_BLOCK)
           & (cmp_start[None, :] + CMP_BLOCK > sel_start[:, None])).astype(np.float32)
    return tnear, cfar, cbias, jnp.asarray(ovl)


def _attention(qT, kc, vcT, zb3, ks_blk, kw_blk, vsT, vwT, gT, tables, batch, seq):
    tnear, cfar, cbias, ovl = tables
    g_, dh = N_KV_GROUPS, HEAD_DIM
    tq = min(TQ, seq)
    assert seq % tq == 0 and seq >= FAR_TILES * tq and WINDOW % tq == 0 and tq % SEL_BLOCK == 0
    nqt = seq // tq
    nw = N_HEADS * tq
    nc = kc.shape[1]
    n_sel = seq // SEL_BLOCK
    kvw = g_ * dh
    per_b = lambda shape: pl.BlockSpec((None,) + shape, lambda b, i: (b,) + (0,) * len(shape))
    full = lambda shape: pl.BlockSpec(shape, lambda b, i: (0,) * len(shape))
    return pl.pallas_call(
        _attn_kernel,
        grid=(batch, nqt),
        in_specs=[pl.BlockSpec((None, N_HEADS * dh, tq), lambda b, i: (b, 0, i)),
                  per_b((nc, kvw)),
                  per_b((kvw, nc)),
                  pl.BlockSpec((None, seq, kvw), lambda b, i: (b, 0, ks_blk)),
                  pl.BlockSpec((None, seq, kvw), lambda b, i: (b, 0, kw_blk)),
                  per_b((kvw, seq)), per_b((kvw, seq)),
                  pl.BlockSpec((None, 3 * N_HEADS, tq), lambda b, i: (b, 0, i)),
                  full((3 * tq, nw)), full((1, nw)),
                  pl.BlockSpec((None, nc, nw), lambda b, i: (i, 0, 0)),
                  full((n_sel, nc))],
        out_specs=pl.BlockSpec((None, tq, N_HEADS * dh), lambda b, i: (b, i, 0)),
        out_shape=jax.ShapeDtypeStruct((batch, seq, N_HEADS * dh), BF16),
        scratch_shapes=[pltpu.VMEM((kvw, nw), BF16), pltpu.VMEM((n_sel, g_ * tq), F32),
                        pltpu.VMEM((1, nw), F32), pltpu.VMEM((1, nw), F32), pltpu.VMEM((dh, nw), F32)],
        compiler_params=_cparams(("arbitrary", "arbitrary")),
        name="nsa_attention",
    )(qT, kc, vcT, zb3, zb3, vsT, vwT, gT, tnear, cfar, cbias, ovl)


def _merge_kernel(yr_ref, ya_ref, wr_ref, wa_ref, ma_ref, mb_ref, o_ref):
    pr = jnp.dot(yr_ref[...], wr_ref[...], preferred_element_type=F32)
    pa = jnp.dot(ya_ref[...], wa_ref[...], preferred_element_type=F32)
    o_ref[...] = (_sigmoid(ma_ref[...]) * pr + _sigmoid(mb_ref[...]) * pa).astype(o_ref.dtype)


def _merge(y_rnn, y_att, w_ur, w_ua, z2d, ma_blk, mb_blk):
    t, r = y_rnn.shape
    a = y_att.shape[1]
    d = w_ur.shape[1]
    tm = min(TM_MERGE, t)
    tn = min(TN_MERGE, d)
    nj = d // tn
    return pl.pallas_call(
        _merge_kernel,
        grid=(t // tm, nj),
        in_specs=[pl.BlockSpec((tm, r), lambda i, j: (i, 0)),
                  pl.BlockSpec((tm, a), lambda i, j: (i, 0)),
                  pl.BlockSpec((r, tn), lambda i, j: (0, j)),
                  pl.BlockSpec((a, tn), lambda i, j: (0, j)),
                  pl.BlockSpec((tm, tn), lambda i, j: (i, ma_blk * nj + j)),
                  pl.BlockSpec((tm, tn), lambda i, j: (i, mb_blk * nj + j))],
        out_specs=pl.BlockSpec((tm, tn), lambda i, j: (i, j)),
        out_shape=jax.ShapeDtypeStruct((t, d), BF16),
        compiler_params=_cparams(("arbitrary", "arbitrary")),
        name="merge",
    )(y_rnn, y_att, w_ur, w_ua, z2d, z2d)


def _outproj_kernel(mg_ref, w_ref, x_ref, g1_ref, nw_ref, sc_ref, sh_ref, x1_ref, h2_ref):
    x1 = x_ref[...] + g1_ref[...] * jnp.dot(mg_ref[...], w_ref[...], preferred_element_type=F32)
    x1_ref[...] = x1
    ms = jnp.mean(x1 * x1, axis=-1, keepdims=True)
    y = (x1 * lax.rsqrt(ms + EPS)) * nw_ref[...]
    h2_ref[...] = y * (1.0 + sc_ref[...]) + sh_ref[...]


def _outproj(merged, w_out, x2d, modr, mod_base, seq, norm_w):
    t, d = x2d.shape
    tm = min(TM_OUT, seq)

    def mod_idx(k):
        return lambda i: (mod_base + ((i * tm) // seq) * 6 + k, 0, 0)

    row_spec = pl.BlockSpec((tm, d), lambda i: (i, 0))
    return pl.pallas_call(
        _outproj_kernel,
        grid=(t // tm,),
        in_specs=[row_spec, pl.BlockSpec((d, d), lambda i: (0, 0)), row_spec,
                  pl.BlockSpec((None, 1, d), mod_idx(2)),
                  pl.BlockSpec((1, d), lambda i: (0, 0)),
                  pl.BlockSpec((None, 1, d), mod_idx(4)),
                  pl.BlockSpec((None, 1, d), mod_idx(3))],
        out_specs=[row_spec, row_spec],
        out_shape=[jax.ShapeDtypeStruct((t, d), F32)] * 2,
        compiler_params=_cparams(("arbitrary",)),
        name="outproj",
    )(merged, w_out, x2d, modr, norm_w.reshape(1, d), modr, modr)


def _router_kernel(h_ref, rw_ref, rb_ref, idx_ref, wt_ref, rank_ref, cnt_ref, carry):
    ne = rw_ref.shape[0]
    tm = h_ref.shape[0]
    per = ne // N_EXPERT_GROUPS
    assert per == SUBLANE, "one expert group per sublane tile"

    @pl.when(pl.program_id(0) == 0)
    def _():
        carry[...] = jnp.zeros_like(carry)

    logits = lax.dot_general(rw_ref[...], h_ref[...].astype(BF16), (((1,), (1,)), ((), ())),
                             preferred_element_type=F32)
    scores = _sigmoid(logits)
    biased = scores + rb_ref[...]
    erow = lax.broadcasted_iota(jnp.int32, (ne, tm), 0).astype(F32)
    grow = lax.broadcasted_iota(jnp.int32, (ne, tm), 0) // per

    gparts = []
    sub = lax.broadcasted_iota(jnp.int32, (per, tm), 0).astype(F32)
    for gi in range(N_EXPERT_GROUPS):
        xg = biased[gi * per:(gi + 1) * per, :]
        m1 = _col_max(xg)
        f1 = jnp.min(jnp.where(xg == m1, sub, float(per)), axis=0, keepdims=True)
        m2 = _col_max(jnp.where(sub == f1, -jnp.inf, xg))
        gparts.append(jnp.broadcast_to(m1 + m2, (per, tm)))
    gscore = jnp.concatenate(gparts, axis=0)

    kparts = []
    for gi in range(N_EXPERT_GROUPS):
        gs = gscore[gi * per:gi * per + 1, :]
        beats = (gscore > gs) | ((gscore == gs) & (grow < gi))
        nbeat = jnp.sum(beats.astype(F32), axis=0, keepdims=True)
        kparts.append(jnp.broadcast_to(nbeat < float(TOPK_EXPERT_GROUPS * per), (per, tm)))
    gkeep = jnp.concatenate(kparts, axis=0)

    work = jnp.where(gkeep, biased, -jnp.inf)
    picks = []
    chosen = jnp.zeros((ne, tm), F32)
    for _ in range(EXPERT_TOP_K):
        mx = _col_max(work)
        first = jnp.min(jnp.where(work == mx, erow, float(ne)), axis=0, keepdims=True)
        pick = erow == first
        picks.append(pick)
        chosen = jnp.where(pick, 1.0, chosen)
        work = jnp.where(pick, -jnp.inf, work)

    tri = (lax.broadcasted_iota(jnp.int32, (tm, tm), 0) < lax.broadcasted_iota(jnp.int32, (tm, tm), 1))
    before = jnp.dot(chosen.astype(BF16), tri.astype(BF16), preferred_element_type=F32)
    pos = before + carry[:, 0:1]
    new_carry = carry[:, 0:1] + jnp.sum(chosen, axis=1, keepdims=True)
    carry[...] = jnp.broadcast_to(new_carry, carry.shape)
    cnt_ref[...] = carry[...]

    krow = lax.broadcasted_iota(jnp.int32, (EXPERT_TOP_K, tm), 0)
    idx_o = jnp.zeros((EXPERT_TOP_K, tm), F32)
    wt_o = jnp.zeros((EXPERT_TOP_K, tm), F32)
    rk_o = jnp.zeros((EXPERT_TOP_K, tm), F32)
    for k, pick in enumerate(picks):
        sel = lambda v: jnp.sum(jnp.where(pick, v, 0.0), axis=0, keepdims=True)
        idx_o = jnp.where(krow == k, sel(erow), idx_o)
        wt_o = jnp.where(krow == k, sel(scores), wt_o)
        rk_o = jnp.where(krow == k, sel(pos), rk_o)
    wsum = jnp.sum(wt_o, axis=0, keepdims=True)
    idx_ref[...] = idx_o.astype(jnp.int32)
    wt_ref[...] = (ROUTED_SCALE * wt_o) / wsum
    rank_ref[...] = rk_o.astype(jnp.int32)


def _router(h2, router_w, router_bias):
    t, d = h2.shape
    ne = router_w.shape[1]
    tm = min(TM_ROUTE, t)
    k_spec = pl.BlockSpec((EXPERT_TOP_K, tm), lambda i: (0, i))
    return pl.pallas_call(
        _router_kernel,
        grid=(t // tm,),
        in_specs=[pl.BlockSpec((tm, d), lambda i: (i, 0)),
                  pl.BlockSpec((ne, d), lambda i: (0, 0)),
                  pl.BlockSpec((ne, 1), lambda i: (0, 0))],
        out_specs=[k_spec, k_spec, k_spec, pl.BlockSpec((ne, LANE), lambda i: (0, 0))],
        out_shape=[jax.ShapeDtypeStruct((EXPERT_TOP_K, t), jnp.int32),
                   jax.ShapeDtypeStruct((EXPERT_TOP_K, t), F32),
                   jax.ShapeDtypeStruct((EXPERT_TOP_K, t), jnp.int32),
                   jax.ShapeDtypeStruct((ne, LANE), F32)],
        scratch_shapes=[pltpu.VMEM((ne, LANE), F32)],
        compiler_params=_cparams(("arbitrary",)),
        name="moe_router",
    )(h2, router_w.T.astype(BF16), router_bias.reshape(ne, 1).astype(F32))


def _row_copy(src_ref, src_row, dst_ref, dst_row, sem):
    return pltpu.make_async_copy(src_ref.at[pl.ds(src_row, 1)], dst_ref.at[pl.ds(dst_row, 1)], sem)


def _pack_bf16_pairs(x):
    half = x.shape[1] // 2
    bits = lax.bitcast_convert_type(x.astype(BF16).astype(F32), jnp.uint32)
    return bits[:, :half] | (bits[:, half:] >> 16)


def _unpack_bf16_pairs(w):
    hi = lax.bitcast_convert_type(w & jnp.uint32(0xFFFF0000), F32)
    lo = lax.bitcast_convert_type(w << 16, F32)
    return hi, lo


def _row_tiles(words):
    return (words // LANE, LANE)


def _dispatch_kernel(start_ref, cnt_ref, dest_ref, h_ref, xs_ref, hp, zblk, sem, zsem):
    i = pl.program_id(0)
    n_steps = pl.num_programs(0)
    slot = i % 2
    tm = h_ref.shape[0]
    ne = start_ref.shape[0] - 1
    rb = zblk.shape[0]
    n_blocks = xs_ref.shape[0] // rb
    hp[slot] = _pack_bf16_pairs(h_ref[...]).reshape(hp.shape[1:])

    def pad_rows(e, fn):
        lo = start_ref[e] + cnt_ref[e]
        lax.fori_loop(lo, start_ref[e + 1], lambda r, c: fn(_row_copy(zblk, 0, xs_ref, r, zsem)) or c, 0)

    def tail_blocks(fn):
        def body(b, c):
            fn(pltpu.make_async_copy(zblk, xs_ref.at[pl.ds(pl.multiple_of(b * rb, rb), rb)], zsem))
            return c
        lax.fori_loop(start_ref[ne] // rb, n_blocks, body, 0)

    @pl.when(i == 0)
    def _():
        zblk[...] = jnp.zeros_like(zblk)
        lax.fori_loop(0, ne, lambda e, c: pad_rows(e, lambda cp: cp.start()) or c, 0)
        tail_blocks(lambda cp: cp.start())
        lax.fori_loop(0, ne, lambda e, c: pad_rows(e, lambda cp: cp.wait()) or c, 0)
        tail_blocks(lambda cp: cp.wait())

    def token_rows(s, fn):
        def body(r, c):
            for k in range(EXPERT_TOP_K):
                fn(_row_copy(hp.at[s], r, xs_ref, dest_ref[k, r], sem.at[s]), k)
            return c
        lax.fori_loop(0, tm, body, 0)

    token_rows(slot, lambda cp, k: cp.start(priority=k % DMA_PRIORITIES))

    @pl.when(i > 0)
    def _():
        token_rows(1 - slot, lambda cp, k: cp.wait())

    @pl.when(i == n_steps - 1)
    def _():
        token_rows(slot, lambda cp, k: cp.wait())


def _dispatch(starts, cnt, dest, h2, n_rows):
    t, d = h2.shape
    tm = min(TM_DISP, t)
    return pl.pallas_call(
        _dispatch_kernel,
        grid_spec=pltpu.PrefetchScalarGridSpec(
            num_scalar_prefetch=2,
            grid=(t // tm,),
            in_specs=[pl.BlockSpec((EXPERT_TOP_K, tm), lambda i, *_: (0, i), memory_space=pltpu.SMEM),
                      pl.BlockSpec((tm, d), lambda i, *_: (i, 0))],
            out_specs=pl.BlockSpec(memory_space=pl.ANY),
            scratch_shapes=[pltpu.VMEM((2, tm) + _row_tiles(d // 2), jnp.uint32),
                            pltpu.VMEM((min(ROW_BLOCK, n_rows),) + _row_tiles(d // 2), jnp.uint32),
                            pltpu.SemaphoreType.DMA((2,)), pltpu.SemaphoreType.DMA(())]),
        out_shape=jax.ShapeDtypeStruct((n_rows,) + _row_tiles(d // 2), jnp.uint32),
        compiler_params=_cparams(("arbitrary",)),
        name="moe_dispatch",
    )(starts, cnt, dest, h2)


def _mlp_kernel(be_ref, nb_ref, x_ref, wg_ref, wu_ref, wd_ref, y_ref, wg_s, wu_s, wd_s, *, packed):
    i = pl.program_id(0)

    @pl.when((i == 0) | (be_ref[i] != be_ref[jnp.maximum(i - 1, 0)]))
    def _():
        wg_s[...] = wg_ref[...].astype(BF16)
        wu_s[...] = wu_ref[...].astype(BF16)
        wd_s[...] = wd_ref[...].astype(BF16)

    @pl.when(i < nb_ref[0])
    def _():
        if packed:
            xw = x_ref[...]
            xw = xw.reshape(xw.shape[0], xw.shape[1] * xw.shape[2])
            x = jnp.concatenate(_unpack_bf16_pairs(xw), axis=1).astype(BF16)
        else:
            x = x_ref[...].astype(BF16)
        gt = jnp.dot(x, wg_s[...], preferred_element_type=F32)
        up = jnp.dot(x, wu_s[...], preferred_element_type=F32)
        hb = ((gt * _sigmoid(gt)) * up).astype(BF16)
        y = jnp.dot(hb, wd_s[...], preferred_element_type=F32)
        y_ref[...] = _pack_bf16_pairs(y).reshape(y_ref.shape) if packed else y

    @pl.when(i >= nb_ref[0])
    def _():
        y_ref[...] = jnp.zeros_like(y_ref)


def _grouped_mlp(block_expert, n_used, xs, layer, w_gate, w_up, w_down, packed, name):
    n_rows, row_shape = xs.shape[0], xs.shape[1:]
    zeros = (0,) * len(row_shape)
    d, hid = w_gate.shape[2], w_gate.shape[3]
    rb = min(ROW_BLOCK, n_rows)
    w_spec = lambda shape: pl.BlockSpec((None, None) + shape, lambda i, be, nb: (layer, be[i], 0, 0))
    return pl.pallas_call(
        functools.partial(_mlp_kernel, packed=packed),
        grid_spec=pltpu.PrefetchScalarGridSpec(
            num_scalar_prefetch=2,
            grid=(n_rows // rb,),
            in_specs=[pl.BlockSpec((rb,) + row_shape, lambda i, be, nb: (jnp.minimum(i, nb[0] - 1),) + zeros),
                      w_spec((d, hid)), w_spec((d, hid)), w_spec((hid, d))],
            out_specs=pl.BlockSpec((rb,) + row_shape, lambda i, be, nb: (i,) + zeros),
            scratch_shapes=[pltpu.VMEM((d, hid), BF16), pltpu.VMEM((d, hid), BF16), pltpu.VMEM((hid, d), BF16)]),
        out_shape=jax.ShapeDtypeStruct(xs.shape, xs.dtype),
        compiler_params=_cparams(("arbitrary",)),
        name=name,
    )(block_expert, n_used, xs, w_gate, w_up, w_down)


def _combine_kernel(dest_ref, dnext_ref, wt_ref, ysh_ref, x_ref, g2_ref, fn_ref, ys_ref, o_ref, ybuf, sem, *,
                    final):
    i = pl.program_id(0)
    n_steps = pl.num_programs(0)
    slot = i % 2
    tm, d = x_ref.shape

    def token_rows(rows_ref, s, fn):
        def body(r, c):
            for k in range(EXPERT_TOP_K):
                fn(_row_copy(ys_ref, rows_ref[k, r], ybuf.at[s].at[k], r, sem.at[s]), k)
            return c
        lax.fori_loop(0, tm, body, 0)

    start = lambda cp, k: cp.start(priority=k % DMA_PRIORITIES)

    @pl.when(i == 0)
    def _():
        token_rows(dest_ref, 0, start)

    @pl.when(i + 1 < n_steps)
    def _():
        token_rows(dnext_ref, 1 - slot, start)

    token_rows(dest_ref, slot, lambda cp, k: cp.wait())

    wt = wt_ref[...]
    acc_hi = jnp.zeros((tm, d // 2), F32)
    acc_lo = jnp.zeros((tm, d // 2), F32)
    for k in range(EXPERT_TOP_K):
        hi, lo = _unpack_bf16_pairs(ybuf[slot, k].reshape(tm, d // 2))
        acc_hi = acc_hi + wt[:, k:k + 1] * hi
        acc_lo = acc_lo + wt[:, k:k + 1] * lo
    acc = ysh_ref[...] + jnp.concatenate([acc_hi, acc_lo], axis=1)
    xn = x_ref[...] + g2_ref[...] * acc
    if final:
        ms = jnp.mean(xn * xn, axis=-1, keepdims=True)
        xn = (xn * lax.rsqrt(ms + EPS)) * fn_ref[...]
    o_ref[...] = xn


def _combine(dest, wts_t, y_sorted, y_shared, x2d, modr, mod_base, seq, final_norm, final):
    t, d = x2d.shape
    tm = min(TM_COMB, seq)
    row_spec = pl.BlockSpec((tm, d), lambda i: (i, 0))
    n_steps = t // tm
    return pl.pallas_call(
        functools.partial(_combine_kernel, final=final),
        grid=(n_steps,),
        in_specs=[pl.BlockSpec((EXPERT_TOP_K, tm), lambda i: (0, i), memory_space=pltpu.SMEM),
                  pl.BlockSpec((EXPERT_TOP_K, tm), lambda i: (0, jnp.minimum(i + 1, n_steps - 1)),
                               memory_space=pltpu.SMEM),
                  pl.BlockSpec((tm, EXPERT_TOP_K), lambda i: (i, 0)),
                  row_spec, row_spec,
                  pl.BlockSpec((None, 1, d), lambda i: (mod_base + ((i * tm) // seq) * 6 + 5, 0, 0)),
                  pl.BlockSpec((1, d), lambda i: (0, 0)),
                  pl.BlockSpec(memory_space=pl.ANY)],
        out_specs=row_spec,
        out_shape=jax.ShapeDtypeStruct((t, d), F32),
        scratch_shapes=[pltpu.VMEM((2, EXPERT_TOP_K, tm) + _row_tiles(d // 2), jnp.uint32),
                        pltpu.SemaphoreType.DMA((2,))],
        compiler_params=_cparams(("arbitrary",)),
        name="moe_combine",
    )(dest, dest, wts_t, y_shared, x2d, modr, final_norm.reshape(1, d), y_sorted)


def _layout(d_model):
    r, a, kvw = RNN_WIDTH, N_HEADS * HEAD_DIM, N_KV_GROUPS * HEAD_DIM
    off = {}
    off["ma"], off["mb"] = 0, d_model
    off["u"] = 2 * d_model
    off["g"] = off["u"] + r
    off["kc"] = off["g"] + r
    off["vc"] = off["kc"] + kvw
    off["nf"] = off["vc"] + kvw
    off["q"] = 0
    off["ks"], off["vs"], off["kw"], off["vw"] = a, a + kvw, a + 2 * kvw, a + 3 * kvw
    off["gn"] = a + 4 * kvw
    off["nb"] = off["gn"] + TN_IN
    assert off["nf"] % TN_IN == 0 and off["gn"] % TN_IN == 0
    return off


def _pack_w_in(w_in_l, d_model):
    r, a, kvw = RNN_WIDTH, N_HEADS * HEAD_DIM, N_KV_GROUPS * HEAD_DIM
    off = _layout(d_model)
    s_u, s_g, s_q = 0, r, 2 * r
    s_kv = s_q + a
    s_gn = s_kv + 6 * kvw
    s_ma = s_gn + 3 * N_HEADS
    s_mb = s_ma + d_model
    cols = [w_in_l[:, s_ma:s_mb], w_in_l[:, s_mb:s_mb + d_model], w_in_l[:, s_u:s_g], w_in_l[:, s_g:s_q],
            w_in_l[:, s_kv:s_kv + 2 * kvw],
            w_in_l[:, s_q:s_kv], w_in_l[:, s_kv + 2 * kvw:s_gn], w_in_l[:, s_gn:s_ma]]
    w = jnp.concatenate(cols, axis=1)
    return jnp.pad(w, ((0, 0), (0, off["nf"] + off["nb"] - w.shape[1]))).astype(BF16)


def _moe_plan(idx, rank, counts, n_tok):
    cnt = counts[:, 0].astype(jnp.int32)
    padded = (cnt + ROW_BLOCK - 1) // ROW_BLOCK * ROW_BLOCK
    ends = jnp.cumsum(padded)
    starts = jnp.concatenate([jnp.zeros((1,), jnp.int32), ends]).astype(jnp.int32)
    onehot = idx[:, :, None] == jnp.arange(N_EXPERTS, dtype=jnp.int32)[None, None, :]
    dest = rank + jnp.sum(jnp.where(onehot, starts[None, None, :N_EXPERTS], 0), axis=2)
    n_blocks = (n_tok * EXPERT_TOP_K) // ROW_BLOCK + N_EXPERTS
    blk_start = jnp.arange(n_blocks, dtype=jnp.int32) * ROW_BLOCK
    owner = jnp.sum((ends[None, :] <= blk_start[:, None]).astype(jnp.int32), axis=1)
    block_expert = jnp.minimum(owner, N_EXPERTS - 1).astype(jnp.int32)
    n_used = (ends[-1] // ROW_BLOCK).astype(jnp.int32).reshape(1)
    return starts, cnt, dest.astype(jnp.int32), block_expert, n_used, n_blocks * ROW_BLOCK


def kernel(x, c, rel_bias, final_norm, ada_w, ada_b, norm_mix, norm_ffn, w_in, conv_w, conv_b, lru_wa, lru_ba, lru_wx, lru_bx, lru_lambda, cmp_pe_k, cmp_w1_k, cmp_w2_k, cmp_pe_v, cmp_w1_v, cmp_w2_v, w_up_rnn, w_up_att, w_out, router_w, router_bias, exp_w_gate, exp_w_up, exp_w_down, sh_w_gate, sh_w_up, sh_w_down):
    batch, seq, d = x.shape
    n_tok = batch * seq
    depth = ada_w.shape[0]
    g_, dh = N_KV_GROUPS, HEAD_DIM
    a_w, kvw, r = N_HEADS * HEAD_DIM, N_KV_GROUPS * HEAD_DIM, RNN_WIDTH
    off = _layout(d)
    assert seq % CMP_STRIDE == 0 and CMP_BLOCK == 2 * CMP_STRIDE

    mod = _adaln_mod(c, ada_w, ada_b)
    modr = mod.reshape(depth * batch * 6, 1, d)
    tables = _bias_tables(rel_bias, seq)
    nc = tables[2].shape[1]
    n_chunk = seq // CMP_STRIDE

    x2d = x.reshape(n_tok, d)
    for l in range(depth):
        mod_base = l * batch * 6
        zf, zb, zg = _inproj(x2d, norm_mix[l], modr, mod_base, seq, _pack_w_in(w_in[l], d), off["nf"])
        y_rnn = _rglru(zf, batch, seq, off["u"] // r, off["g"] // r, conv_w[l], conv_b[l],
                       lru_wa[l], lru_ba[l], lru_wx[l], lru_bx[l], lru_lambda[l])

        zb3 = zb.reshape(batch, seq, -1)
        zcol = lambda z3, name, w: z3[:, :, off[name]:off[name] + w]

        def unfold(v):
            ch = jnp.transpose(v.reshape(batch, n_chunk, CMP_STRIDE, g_, dh), (0, 3, 1, 2, 4))
            ch = ch.reshape(batch * g_, n_chunk, CMP_STRIDE * dh)
            blocks = jnp.concatenate([ch[:, :-1], ch[:, 1:]], axis=2)
            return jnp.pad(blocks, ((0, 0), (0, nc - (n_chunk - 1)), (0, 0)))

        zf3 = zf.reshape(batch, seq, -1)
        kc, vc = _compress(unfold(zcol(zf3, "kc", kvw)), unfold(zcol(zf3, "vc", kvw)),
                           cmp_pe_k[l], cmp_w1_k[l], cmp_w2_k[l], cmp_pe_v[l], cmp_w1_v[l], cmp_w2_v[l])
        kc = jnp.transpose(kc.reshape(batch, g_, nc, dh), (0, 2, 1, 3)).reshape(batch, nc, kvw)
        vcT = jnp.swapaxes(vc.reshape(batch, g_, nc, dh), 2, 3).reshape(batch, kvw, nc)
        qT = jnp.swapaxes(zcol(zb3, "q", a_w), 1, 2)
        vsT = jnp.swapaxes(zcol(zb3, "vs", kvw), 1, 2)
        vwT = jnp.swapaxes(zcol(zb3, "vw", kvw), 1, 2)
        gT = jnp.swapaxes(zg.reshape(batch, seq, LANE)[:, :, :3 * N_HEADS], 1, 2)
        y_att = _attention(qT, kc, vcT, zb3, off["ks"] // kvw, off["kw"] // kvw, vsT, vwT, gT, tables,
                           batch, seq).reshape(n_tok, a_w)

        merged = _merge(y_rnn, y_att, w_up_rnn[l].astype(BF16), w_up_att[l].astype(BF16), zf,
                        off["ma"] // d, off["mb"] // d)
        x1, h2 = _outproj(merged, w_out[l].astype(BF16), x2d, modr, mod_base, seq, norm_ffn[l])

        idx, wts, rank, counts = _router(h2, router_w[l], router_bias[l])
        starts, cnt, dest, block_expert, n_used, n_rows = _moe_plan(idx, rank, counts, n_tok)
        xs = _dispatch(starts, cnt, dest, h2, n_rows)
        y_sorted = _grouped_mlp(block_expert, n_used, xs, l, exp_w_gate, exp_w_up, exp_w_down, True, "moe_experts")
        sh_blocks = n_tok // min(ROW_BLOCK, n_tok)
        y_shared = _grouped_mlp(jnp.zeros((sh_blocks,), jnp.int32), jnp.full((1,), sh_blocks, jnp.int32), h2, l,
                                sh_w_gate[:, None], sh_w_up[:, None], sh_w_down[:, None], False, "moe_shared")
        x2d = _combine(dest, wts.T, y_sorted, y_shared, x1, modr, mod_base, seq, final_norm, l == depth - 1)
    return x2d.reshape(batch, seq, d)
```

```python
import functools
import math

import numpy as np
import jax
import jax.numpy as jnp
from jax import lax
from jax.experimental import pallas as pl
from jax.experimental.pallas import tpu as pltpu

DEPTH = 2
RNN_WIDTH = 1024
RNN_BLOCKS = 8
CONV_WIDTH = 4
LRU_C = 8.0
N_HEADS = 16
N_KV_GROUPS = 4
HEAD_DIM = 64
CMP_BLOCK = 32
CMP_STRIDE = 16
CMP_HIDDEN = 128
SEL_BLOCK = 64
SEL_TOP_N = 8
WINDOW = 512
N_BUCKETS = 32
MAX_DISTANCE = 128
N_EXPERTS = 64
EXPERT_TOP_K = 8
N_EXPERT_GROUPS = 8
TOPK_EXPERT_GROUPS = 4
ROUTED_SCALE = 2.5
EPS = 1e-6
NEG = -1e30

LANE = 128
SUBLANE = 8
VMEM_LIMIT = 52 * 1024 * 1024
DMA_PRIORITIES = 2

TM_IN = 1024
TN_IN = 512
TN_MOD = 1024
TC_RNN = 256
TQ = 128
FAR_TILES = 4
TM_MERGE = 512
TN_MERGE = 2048
TM_OUT = 512
TM_ROUTE = 256
TM_DISP = 128
ROW_BLOCK = 256
TM_COMB = 128

F32 = jnp.float32
BF16 = jnp.bfloat16


def _cparams(sem):
    return pltpu.CompilerParams(dimension_semantics=sem, vmem_limit_bytes=VMEM_LIMIT)


def _round_up(a, b):
    return (a + b - 1) // b * b


def _tile(n, pref):
    if n <= pref:
        return n
    t = pref // LANE * LANE
    while n % t:
        t -= LANE
    return t


def _gelu_tanh(x):
    return x * (0.5 * (1.0 + jnp.tanh(math.sqrt(2.0 / math.pi) * (x + 0.044715 * (x * x * x)))))


def _sigmoid(x):
    return jax.nn.sigmoid(x)


def _mod_kernel(c_ref, w_ref, b_ref, o_ref):
    c = c_ref[...]
    ca = (c * _sigmoid(c)).astype(BF16)
    o_ref[...] = jnp.dot(ca, w_ref[...].astype(BF16), preferred_element_type=F32) + b_ref[...]


def _adaln_mod(c, ada_w, ada_b):
    nl, d, n6 = ada_w.shape
    b = c.shape[0]
    tn = _tile(n6, TN_MOD)
    return pl.pallas_call(
        _mod_kernel,
        grid=(nl, n6 // tn),
        in_specs=[pl.BlockSpec((b, d), lambda l, j: (0, 0)),
                  pl.BlockSpec((None, d, tn), lambda l, j: (l, 0, j)),
                  pl.BlockSpec((None, 1, tn), lambda l, j: (l, 0, j))],
        out_specs=pl.BlockSpec((None, b, tn), lambda l, j: (l, 0, j)),
        out_shape=jax.ShapeDtypeStruct((nl, b, n6), F32),
        compiler_params=_cparams(("arbitrary", "arbitrary")),
        name="adaln_mod",
    )(c, ada_w, ada_b.reshape(nl, 1, n6))


def _inproj_kernel(x_ref, nw_ref, sc_ref, sh_ref, w_ref, zf_ref, zb_ref, zg_ref, h_ref, *, nf, nj):
    j = pl.program_id(1)

    @pl.when(j == 0)
    def _():
        x = x_ref[...]
        ms = jnp.mean(x * x, axis=-1, keepdims=True)
        y = (x * lax.rsqrt(ms + EPS)) * nw_ref[...]
        h_ref[...] = (y * (1.0 + sc_ref[...]) + sh_ref[...]).astype(BF16)

    res = jnp.dot(h_ref[...], w_ref[...], preferred_element_type=F32)

    @pl.when(j < nf)
    def _():
        zf_ref[...] = res

    @pl.when(j >= nf)
    def _():
        zb_ref[...] = res.astype(BF16)

    @pl.when(j == nj - 1)
    def _():
        zg_ref[...] = res[:, 0:LANE]


def _inproj(x2d, norm_w, modr, mod_base, seq, w_p, n_f32):
    t, d = x2d.shape
    n_p = w_p.shape[1]
    tm = min(TM_IN, seq)
    tn = TN_IN
    assert n_f32 % tn == 0 and n_p % tn == 0
    nf, nj = n_f32 // tn, n_p // tn

    def mod_idx(k):
        return lambda i, j: (mod_base + ((i * tm) // seq) * 6 + k, 0, 0)

    return pl.pallas_call(
        functools.partial(_inproj_kernel, nf=nf, nj=nj),
        grid=(t // tm, nj),
        in_specs=[pl.BlockSpec((tm, d), lambda i, j: (i, 0)),
                  pl.BlockSpec((1, d), lambda i, j: (0, 0)),
                  pl.BlockSpec((None, 1, d), mod_idx(1)),
                  pl.BlockSpec((None, 1, d), mod_idx(0)),
                  pl.BlockSpec((d, tn), lambda i, j: (0, j))],
        out_specs=[pl.BlockSpec((tm, tn), lambda i, j: (i, jnp.minimum(j, nf - 1))),
                   pl.BlockSpec((tm, tn), lambda i, j: (i, jnp.maximum(j - nf, 0))),
                   pl.BlockSpec((tm, LANE), lambda i, j: (i, 0))],
        out_shape=[jax.ShapeDtypeStruct((t, n_f32), F32), jax.ShapeDtypeStruct((t, n_p - n_f32), BF16),
                   jax.ShapeDtypeStruct((t, LANE), F32)],
        scratch_shapes=[pltpu.VMEM((tm, d), BF16)],
        compiler_params=_cparams(("arbitrary", "arbitrary")),
        name="inproj",
    )(x2d, norm_w.reshape(1, d), modr, modr, w_p)


def _rglru_kernel(u_ref, g_ref, cw_ref, cb_ref, wa_ref, ba_ref, wx_ref, bx_ref, lam_ref, y_ref,
                  ubuf, a_s, b_s, h_s, hcar):
    tc, r = u_ref.shape
    nb = wa_ref.shape[0]
    bw = r // nb

    @pl.when(pl.program_id(1) == 0)
    def _():
        ubuf[0:SUBLANE, :] = jnp.zeros((SUBLANE, r), F32)
        hcar[...] = jnp.zeros_like(hcar)

    ubuf[SUBLANE:SUBLANE + tc, :] = u_ref[...]
    cw = cw_ref[...]
    uc = cb_ref[...] + cw[CONV_WIDTH - 1:CONV_WIDTH, :] * ubuf[SUBLANE:SUBLANE + tc, :]
    for k in range(CONV_WIDTH - 1):
        off = SUBLANE - (CONV_WIDTH - 1) + k
        uc = uc + cw[k:k + 1, :] * ubuf[off:off + tc, :]
    ubuf[0:SUBLANE, :] = ubuf[tc:tc + SUBLANE, :]

    ucb = uc.astype(BF16)
    rp = []
    xp = []
    for n in range(nb):
        blk = ucb[:, n * bw:(n + 1) * bw]
        rp.append(jnp.dot(blk, wa_ref[n], preferred_element_type=F32))
        xp.append(jnp.dot(blk, wx_ref[n], preferred_element_type=F32))
    rg = _sigmoid(jnp.concatenate(rp, axis=1) + ba_ref[...])
    ig = _sigmoid(jnp.concatenate(xp, axis=1) + bx_ref[...])
    nl = -lam_ref[...]
    sp = jnp.maximum(nl, 0.0) + jnp.log1p(jnp.exp(-jnp.abs(nl)))
    log_a = (-LRU_C * rg) * sp
    a_s[...] = jnp.exp(log_a)
    th = jnp.tanh(log_a)
    one_minus_a2 = (-2.0 * th) / (1.0 - th)
    b_s[...] = jnp.sqrt(one_minus_a2) * (ig * uc)

    def step(t, h):
        h = a_s[pl.ds(t, 1), :] * h + b_s[pl.ds(t, 1), :]
        h_s[pl.ds(t, 1), :] = h
        return h

    h_last = lax.fori_loop(0, tc, step, hcar[0:1, :], unroll=8)
    hcar[0:1, :] = h_last
    y_ref[...] = (_gelu_tanh(g_ref[...]) * h_s[...]).astype(y_ref.dtype)


def _rglru(z2d, batch, seq, u_blk, g_blk, conv_w, conv_b, wa, ba, wx, bx, lam):
    r = conv_w.shape[1]
    tc = min(TC_RNN, seq)
    nt = seq // tc
    nb, bw, _ = wa.shape
    row = lambda v: v.reshape(1, r)
    full = lambda shape: pl.BlockSpec(shape, lambda b, t: (0,) * len(shape))
    return pl.pallas_call(
        _rglru_kernel,
        grid=(batch, nt),
        in_specs=[pl.BlockSpec((tc, r), lambda b, t: (b * nt + t, u_blk)),
                  pl.BlockSpec((tc, r), lambda b, t: (b * nt + t, g_blk)),
                  full((CONV_WIDTH, r)), full((1, r)),
                  full((nb, bw, bw)), full((1, r)),
                  full((nb, bw, bw)), full((1, r)), full((1, r))],
        out_specs=pl.BlockSpec((tc, r), lambda b, t: (b * nt + t, 0)),
        out_shape=jax.ShapeDtypeStruct((batch * seq, r), BF16),
        scratch_shapes=[pltpu.VMEM((tc + SUBLANE, r), F32), pltpu.VMEM((tc, r), F32),
                        pltpu.VMEM((tc, r), F32), pltpu.VMEM((tc, r), F32),
                        pltpu.VMEM((SUBLANE, r), F32)],
        compiler_params=_cparams(("arbitrary", "arbitrary")),
        name="rglru",
    )(z2d, z2d, conv_w, row(conv_b), wa.astype(BF16), row(ba), wx.astype(BF16), row(bx), row(lam))


def _compress_kernel(xk_ref, xv_ref, pek_ref, w1k_ref, w2k_ref, pev_ref, w1v_ref, w2v_ref, kc_ref, vc_ref):
    def one(x_ref, pe_ref, w1_ref, w2_ref, o_ref):
        blocks = (x_ref[...] + pe_ref[...]).astype(BF16)
        hid = _gelu_tanh(jnp.dot(blocks, w1_ref[...], preferred_element_type=F32))
        o_ref[...] = jnp.dot(hid.astype(BF16), w2_ref[...], preferred_element_type=F32).astype(o_ref.dtype)

    one(xk_ref, pek_ref, w1k_ref, w2k_ref, kc_ref)
    one(xv_ref, pev_ref, w1v_ref, w2v_ref, vc_ref)


def _compress(xk, xv, pe_k, w1_k, w2_k, pe_v, w1_v, w2_v):
    bg, nc, kd = xk.shape
    dh = w2_k.shape[1]
    hid = w2_k.shape[0]
    x_spec = pl.BlockSpec((None, nc, kd), lambda i: (i, 0, 0))
    full = lambda shape: pl.BlockSpec(shape, lambda i: (0,) * len(shape))
    o_spec = pl.BlockSpec((None, nc, dh), lambda i: (i, 0, 0))
    prep = lambda pe, w1, w2: (pe.reshape(1, kd), w1.reshape(kd, hid).astype(BF16), w2.astype(BF16))
    return pl.pallas_call(
        _compress_kernel,
        grid=(bg,),
        in_specs=[x_spec, x_spec, full((1, kd)), full((kd, hid)), full((hid, dh)),
                  full((1, kd)), full((kd, hid)), full((hid, dh))],
        out_specs=[o_spec, o_spec],
        out_shape=[jax.ShapeDtypeStruct((bg, nc, dh), BF16)] * 2,
        compiler_params=_cparams(("arbitrary",)),
        name="nsa_compress",
    )(xk, xv, *prep(pe_k, w1_k, w2_k), *prep(pe_v, w1_v, w2_v))


def _col_max(x):
    return jnp.max(x, axis=0, keepdims=True)


def _attn_kernel(qT_ref, kc_ref, vcT_ref, ks_ref, kw_ref, vsT_ref, vwT_ref, gT_ref,
                 tnear_ref, cfar_ref, cbias_ref, ovl_ref, o_ref,
                 qbd, sel_s, m_s, l_s, acc_s):
    i = pl.program_id(1)
    g_, hg, dh = N_KV_GROUPS, N_HEADS // N_KV_GROUPS, HEAD_DIM
    tq = qT_ref.shape[1]
    gw = hg * tq
    nw = g_ * gw
    n_sel = sel_s.shape[0]
    per = tq // SEL_BLOCK

    qT = (qT_ref[...].astype(F32) * (HEAD_DIM ** -0.5)).astype(BF16)
    zero_blk = jnp.zeros((dh, gw), BF16)
    rows = []
    for g in range(g_):
        qcat = jnp.concatenate([qT[(g * hg + h) * dh:(g * hg + h + 1) * dh, :] for h in range(hg)], axis=1)
        rows.append(jnp.concatenate([zero_blk] * g + [qcat] + [zero_blk] * (g_ - 1 - g), axis=1))
    qbd[...] = jnp.concatenate(rows, axis=0)

    def lane_q(shape):
        return lax.broadcasted_iota(jnp.int32, shape, 1) % tq

    def pv(vT_ref, start, nrows, p):
        pb = p.astype(BF16)
        outs = []
        for g in range(g_):
            if start is None:
                v = vT_ref[g * dh:(g + 1) * dh, :]
            else:
                v = vT_ref[g * dh:(g + 1) * dh, pl.ds(start, nrows)]
            outs.append(jnp.dot(v, pb[:, g * gw:(g + 1) * gw], preferred_element_type=F32))
        return jnp.concatenate(outs, axis=1)

    nc = kc_ref.shape[0]
    near_t = jnp.maximum(i - 1, 0)
    near_start = pl.multiple_of(near_t * tq, tq)
    far_end = near_t
    wr = WINDOW - tq
    win_start = pl.multiple_of(jnp.maximum(i - WINDOW // tq, 0) * tq, tq)
    lhs = jnp.concatenate([kc_ref[...], ks_ref[pl.ds(near_start, 2 * tq), :],
                           kw_ref[pl.ds(near_start, 2 * tq), :], kw_ref[pl.ds(win_start, wr), :]], axis=0)
    s_all = jnp.dot(lhs, qbd[...], preferred_element_type=F32)
    r_slc, r_win, r_far = nc, nc + 2 * tq, nc + 4 * tq

    sc = s_all[0:nc] + cbias_ref[...]
    nrow = lax.broadcasted_iota(jnp.int32, (nc, nw), 0)
    cmask = (nrow * CMP_STRIDE + (CMP_BLOCK - 1)) <= i * tq + lane_q((nc, nw))
    sc = jnp.where(cmask, sc, NEG)
    pc = jnp.where(cmask, jnp.exp(sc - _col_max(sc)), 0.0)
    lc = jnp.sum(pc, axis=0, keepdims=True)
    pc = pc * jnp.where(lc > 0.0, 1.0 / lc, 0.0)
    o_cmp = pv(vcT_ref, None, nc, pc)

    psum = jnp.concatenate(
        [sum(pc[:, g * gw + h * tq:g * gw + (h + 1) * tq] for h in range(hg)) for g in range(g_)], axis=1)
    imp = jnp.dot(ovl_ref[...], psum, preferred_element_type=F32, precision=lax.Precision.HIGHEST)
    jrow = lax.broadcasted_iota(jnp.int32, (n_sel, g_ * tq), 0)
    tq_abs = i * tq + lane_q((n_sel, g_ * tq))
    cur = tq_abs // SEL_BLOCK
    forced = (jrow == 0) | (jrow == cur) | (jrow == cur - 1)
    valid = jrow * SEL_BLOCK <= tq_abs
    work = jnp.where(forced, jnp.inf, jnp.where(valid, imp, -jnp.inf))
    jrow_f = jrow.astype(F32)
    sel = jnp.zeros((n_sel, g_ * tq), F32)
    for _ in range(min(SEL_TOP_N, n_sel)):
        mx = _col_max(work)
        first = jnp.min(jnp.where(work == mx, jrow_f, float(n_sel)), axis=0, keepdims=True)
        pick = jrow_f == first
        sel = jnp.where(pick, 1.0, sel)
        work = jnp.where(pick, -jnp.inf, work)
    sel_s[...] = jnp.where(sel > 0.5, 0.0, NEG)

    def sel_add(first_blk, n_blk, limit_blk=None, bias_row=None):
        parts = []
        for c in range(n_blk):
            rowv = sel_s[pl.ds(first_blk + c, 1), :]
            if limit_blk is not None:
                rowv = rowv + jnp.where(first_blk + c < limit_blk, 0.0, NEG)
            rowv = jnp.concatenate([rowv[:, g * tq:(g + 1) * tq] for g in range(g_) for _ in range(hg)], axis=1)
            if bias_row is not None:
                rowv = rowv + bias_row
            parts.append(jnp.broadcast_to(rowv, (SEL_BLOCK, nw)))
        return jnp.concatenate(parts, axis=0)

    def scores(k_ref, start, nrows):
        return jnp.dot(k_ref[pl.ds(start, nrows), :], qbd[...], preferred_element_type=F32)

    def flash_init(s, vT_ref, start, nrows):
        m = _col_max(s)
        p = jnp.exp(s - m)
        m_s[...] = m
        l_s[...] = jnp.sum(p, axis=0, keepdims=True)
        acc_s[...] = pv(vT_ref, start, nrows, p)

    def flash_update(s, vT_ref, start, nrows):
        m_old = m_s[...]
        m_new = jnp.maximum(m_old, _col_max(s))
        alpha = jnp.exp(m_old - m_new)
        p = jnp.exp(s - m_new)
        m_s[...] = m_new
        l_s[...] = alpha * l_s[...] + jnp.sum(p, axis=0, keepdims=True)
        acc_s[...] = alpha * acc_s[...] + pv(vT_ref, start, nrows, p)

    bias_off = pl.multiple_of(jnp.where(i == 0, tq, 0), tq)
    near_add = tnear_ref[pl.ds(bias_off, 2 * tq), :]

    flash_init(s_all[r_slc:r_win] + near_add + sel_add(near_t * per, 2 * per), vsT_ref, near_start, 2 * tq)
    fr = FAR_TILES * tq

    def far_slc(c, carry):
        hi_t = far_end - c * FAR_TILES
        st_t = jnp.maximum(hi_t - FAR_TILES, 0)
        start = pl.multiple_of(st_t * tq, tq)
        add = sel_add(st_t * per, FAR_TILES * per, hi_t * per, cfar_ref[...])
        flash_update(scores(ks_ref, start, fr) + add, vsT_ref, start, fr)
        return carry

    lax.fori_loop(0, (far_end + FAR_TILES - 1) // FAR_TILES, far_slc, 0)
    o_slc = acc_s[...] * (1.0 / l_s[...])

    flash_init(s_all[r_win:r_far] + near_add, vwT_ref, near_start, 2 * tq)
    nwin = WINDOW // tq
    parts = []
    for c in range(wr // tq):
        tile_ok = win_start // tq + c < far_end
        part = s_all[r_far + c * tq:r_far + (c + 1) * tq] + (cfar_ref[...] + jnp.where(tile_ok, 0.0, NEG))
        if c == 0:
            krow = lax.broadcasted_iota(jnp.int32, (tq, nw), 0)
            part = part + jnp.where(krow > lane_q((tq, nw)) - jnp.where(i < nwin, tq, 0), 0.0, NEG)
        parts.append(part)
    flash_update(jnp.concatenate(parts, axis=0), vwT_ref, win_start, wr)
    o_win = acc_s[...] * (1.0 / l_s[...])

    gate = _sigmoid(gT_ref[...])

    def grow(j):
        return jnp.concatenate([gate[h * 3 + j:h * 3 + j + 1, :] for h in range(N_HEADS)], axis=1)

    o_t = grow(0) * o_cmp + grow(1) * o_slc + grow(2) * o_win
    o_hd = jnp.concatenate([o_t[:, h * tq:(h + 1) * tq] for h in range(N_HEADS)], axis=0)
    o_ref[...] = o_hd.T.astype(o_ref.dtype)


def _t5_bucket_np(dist):
    n = np.maximum(dist, 0)
    max_exact = N_BUCKETS // 2
    nf = np.maximum(n, 1).astype(np.float64)
    large = max_exact + (np.log(nf / max_exact) / math.log(MAX_DISTANCE / max_exact)
                         * (N_BUCKETS - max_exact)).astype(np.int64)
    large = np.minimum(large, N_BUCKETS - 1)
    return np.where(n < max_exact, n, large).astype(np.int32)


def _bias_tables(rel_bias, seq):
    tq = min(TQ, seq)
    assert MAX_DISTANCE <= tq, "tiles two or more behind the diagonal must all fall in the last bucket"
    nh = N_HEADS
    nc = _round_up((seq - CMP_BLOCK) // CMP_STRIDE + 1, LANE)
    off_max = (nc - 1) * CMP_STRIDE + CMP_BLOCK - 1
    fd = rel_bias.astype(F32)[_t5_bucket_np(np.arange(-off_max, seq))]
    fdT = fd.T
    kk = np.arange(tq)[:, None]
    qq = np.arange(tq)[None, :]

    def toeplitz(offset):
        idx = off_max + np.maximum(offset + qq - kk, 0)
        return jnp.transpose(fd[idx], (0, 2, 1)).reshape(tq, nh * tq)

    future = jnp.asarray(np.tile(np.where(kk <= qq, 0.0, NEG).astype(np.float32), (1, nh)))
    tnear = jnp.concatenate([toeplitz(tq), toeplitz(0) + future, jnp.full((tq, nh * tq), NEG, F32)], axis=0)
    cfar = jnp.broadcast_to(rel_bias.astype(F32)[N_BUCKETS - 1][:, None], (nh, tq)).reshape(1, nh * tq)
    rows = [lax.slice_in_dim(fdT, off_max - (n * CMP_STRIDE + CMP_BLOCK - 1),
                             off_max - (n * CMP_STRIDE + CMP_BLOCK - 1) + seq, axis=1) for n in range(nc)]
    cb = jnp.stack(rows, axis=0).reshape(nc, nh, seq // tq, tq)
    cbias = jnp.transpose(cb, (2, 0, 1, 3)).reshape(seq // tq, nc, nh * tq)
    n_sel = seq // SEL_BLOCK
    cmp_start = np.arange(nc) * CMP_STRIDE
    sel_start = np.arange(n_sel) * SEL_BLOCK
    ovl = ((cmp_start[None, :] < sel_start[:, None] + SEL_BLOCK)
           & (cmp_start[None, :] + CMP_BLOCK > sel_start[:, None])).astype(np.float32)
    return tnear, cfar, cbias, jnp.asarray(ovl)


def _attention(qT, kc, vcT, zb3, ks_blk, kw_blk, vsT, vwT, gT, tables, batch, seq):
    tnear, cfar, cbias, ovl = tables
    g_, dh = N_KV_GROUPS, HEAD_DIM
    tq = min(TQ, seq)
    assert seq % tq == 0 and seq >= FAR_TILES * tq and WINDOW % tq == 0 and tq % SEL_BLOCK == 0
    nqt = seq // tq
    nw = N_HEADS * tq
    nc = kc.shape[1]
    n_sel = seq // SEL_BLOCK
    kvw = g_ * dh
    per_b = lambda shape: pl.BlockSpec((None,) + shape, lambda b, i: (b,) + (0,) * len(shape))
    full = lambda shape: pl.BlockSpec(shape, lambda b, i: (0,) * len(shape))
    return pl.pallas_call(
        _attn_kernel,
        grid=(batch, nqt),
        in_specs=[pl.BlockSpec((None, N_HEADS * dh, tq), lambda b, i: (b, 0, i)),
                  per_b((nc, kvw)),
                  per_b((kvw, nc)),
                  pl.BlockSpec((None, seq, kvw), lambda b, i: (b, 0, ks_blk)),
                  pl.BlockSpec((None, seq, kvw), lambda b, i: (b, 0, kw_blk)),
                  per_b((kvw, seq)), per_b((kvw, seq)),
                  pl.BlockSpec((None, 3 * N_HEADS, tq), lambda b, i: (b, 0, i)),
                  full((3 * tq, nw)), full((1, nw)),
                  pl.BlockSpec((None, nc, nw), lambda b, i: (i, 0, 0)),
                  full((n_sel, nc))],
        out_specs=pl.BlockSpec((None, tq, N_HEADS * dh), lambda b, i: (b, i, 0)),
        out_shape=jax.ShapeDtypeStruct((batch, seq, N_HEADS * dh), BF16),
        scratch_shapes=[pltpu.VMEM((kvw, nw), BF16), pltpu.VMEM((n_sel, g_ * tq), F32),
                        pltpu.VMEM((1, nw), F32), pltpu.VMEM((1, nw), F32), pltpu.VMEM((dh, nw), F32)],
        compiler_params=_cparams(("arbitrary", "arbitrary")),
        name="nsa_attention",
    )(qT, kc, vcT, zb3, zb3, vsT, vwT, gT, tnear, cfar, cbias, ovl)


def _merge_kernel(yr_ref, ya_ref, wr_ref, wa_ref, ma_ref, mb_ref, o_ref):
    pr = jnp.dot(yr_ref[...], wr_ref[...], preferred_element_type=F32)
    pa = jnp.dot(ya_ref[...], wa_ref[...], preferred_element_type=F32)
    o_ref[...] = (_sigmoid(ma_ref[...]) * pr + _sigmoid(mb_ref[...]) * pa).astype(o_ref.dtype)


def _merge(y_rnn, y_att, w_ur, w_ua, z2d, ma_blk, mb_blk):
    t, r = y_rnn.shape
    a = y_att.shape[1]
    d = w_ur.shape[1]
    tm = min(TM_MERGE, t)
    tn = min(TN_MERGE, d)
    nj = d // tn
    return pl.pallas_call(
        _merge_kernel,
        grid=(t // tm, nj),
        in_specs=[pl.BlockSpec((tm, r), lambda i, j: (i, 0)),
                  pl.BlockSpec((tm, a), lambda i, j: (i, 0)),
                  pl.BlockSpec((r, tn), lambda i, j: (0, j)),
                  pl.BlockSpec((a, tn), lambda i, j: (0, j)),
                  pl.BlockSpec((tm, tn), lambda i, j: (i, ma_blk * nj + j)),
                  pl.BlockSpec((tm, tn), lambda i, j: (i, mb_blk * nj + j))],
        out_specs=pl.BlockSpec((tm, tn), lambda i, j: (i, j)),
        out_shape=jax.ShapeDtypeStruct((t, d), BF16),
        compiler_params=_cparams(("arbitrary", "arbitrary")),
        name="merge",
    )(y_rnn, y_att, w_ur, w_ua, z2d, z2d)


def _outproj_kernel(mg_ref, w_ref, x_ref, g1_ref, nw_ref, sc_ref, sh_ref, x1_ref, h2_ref):
    x1 = x_ref[...] + g1_ref[...] * jnp.dot(mg_ref[...], w_ref[...], preferred_element_type=F32)
    x1_ref[...] = x1
    ms = jnp.mean(x1 * x1, axis=-1, keepdims=True)
    y = (x1 * lax.rsqrt(ms + EPS)) * nw_ref[...]
    h2_ref[...] = (y * (1.0 + sc_ref[...]) + sh_ref[...]).astype(h2_ref.dtype)


def _outproj(merged, w_out, x2d, modr, mod_base, seq, norm_w):
    t, d = x2d.shape
    tm = min(TM_OUT, seq)

    def mod_idx(k):
        return lambda i: (mod_base + ((i * tm) // seq) * 6 + k, 0, 0)

    row_spec = pl.BlockSpec((tm, d), lambda i: (i, 0))
    return pl.pallas_call(
        _outproj_kernel,
        grid=(t // tm,),
        in_specs=[row_spec, pl.BlockSpec((d, d), lambda i: (0, 0)), row_spec,
                  pl.BlockSpec((None, 1, d), mod_idx(2)),
                  pl.BlockSpec((1, d), lambda i: (0, 0)),
                  pl.BlockSpec((None, 1, d), mod_idx(4)),
                  pl.BlockSpec((None, 1, d), mod_idx(3))],
        out_specs=[row_spec, row_spec],
        out_shape=[jax.ShapeDtypeStruct((t, d), F32), jax.ShapeDtypeStruct((t, d), BF16)],
        compiler_params=_cparams(("arbitrary",)),
        name="outproj",
    )(merged, w_out, x2d, modr, norm_w.reshape(1, d), modr, modr)


def _router_kernel(h_ref, rw_ref, rb_ref, idx_ref, wt_ref, rank_ref, cnt_ref, carry):
    ne = rw_ref.shape[0]
    tm = h_ref.shape[0]
    per = ne // N_EXPERT_GROUPS
    assert per == SUBLANE, "one expert group per sublane tile"

    @pl.when(pl.program_id(0) == 0)
    def _():
        carry[...] = jnp.zeros_like(carry)

    logits = lax.dot_general(rw_ref[...], h_ref[...].astype(BF16), (((1,), (1,)), ((), ())),
                             preferred_element_type=F32)
    scores = _sigmoid(logits)
    biased = scores + rb_ref[...]
    erow = lax.broadcasted_iota(jnp.int32, (ne, tm), 0).astype(F32)
    grow = lax.broadcasted_iota(jnp.int32, (ne, tm), 0) // per

    gparts = []
    sub = lax.broadcasted_iota(jnp.int32, (per, tm), 0).astype(F32)
    for gi in range(N_EXPERT_GROUPS):
        xg = biased[gi * per:(gi + 1) * per, :]
        m1 = _col_max(xg)
        f1 = jnp.min(jnp.where(xg == m1, sub, float(per)), axis=0, keepdims=True)
        m2 = _col_max(jnp.where(sub == f1, -jnp.inf, xg))
        gparts.append(jnp.broadcast_to(m1 + m2, (per, tm)))
    gscore = jnp.concatenate(gparts, axis=0)

    kparts = []
    for gi in range(N_EXPERT_GROUPS):
        gs = gscore[gi * per:gi * per + 1, :]
        beats = (gscore > gs) | ((gscore == gs) & (grow < gi))
        nbeat = jnp.sum(beats.astype(F32), axis=0, keepdims=True)
        kparts.append(jnp.broadcast_to(nbeat < float(TOPK_EXPERT_GROUPS * per), (per, tm)))
    gkeep = jnp.concatenate(kparts, axis=0)

    work = jnp.where(gkeep, biased, -jnp.inf)
    picks = []
    chosen = jnp.zeros((ne, tm), F32)
    for _ in range(EXPERT_TOP_K):
        mx = _col_max(work)
        first = jnp.min(jnp.where(work == mx, erow, float(ne)), axis=0, keepdims=True)
        pick = erow == first
        picks.append(pick)
        chosen = jnp.where(pick, 1.0, chosen)
        work = jnp.where(pick, -jnp.inf, work)

    tri = (lax.broadcasted_iota(jnp.int32, (tm, tm), 0) < lax.broadcasted_iota(jnp.int32, (tm, tm), 1))
    before = jnp.dot(chosen.astype(BF16), tri.astype(BF16), preferred_element_type=F32)
    pos = before + carry[:, 0:1]
    new_carry = carry[:, 0:1] + jnp.sum(chosen, axis=1, keepdims=True)
    carry[...] = jnp.broadcast_to(new_carry, carry.shape)
    cnt_ref[...] = carry[...]

    krow = lax.broadcasted_iota(jnp.int32, (EXPERT_TOP_K, tm), 0)
    idx_o = jnp.zeros((EXPERT_TOP_K, tm), F32)
    wt_o = jnp.zeros((EXPERT_TOP_K, tm), F32)
    rk_o = jnp.zeros((EXPERT_TOP_K, tm), F32)
    for k, pick in enumerate(picks):
        sel = lambda v: jnp.sum(jnp.where(pick, v, 0.0), axis=0, keepdims=True)
        idx_o = jnp.where(krow == k, sel(erow), idx_o)
        wt_o = jnp.where(krow == k, sel(scores), wt_o)
        rk_o = jnp.where(krow == k, sel(pos), rk_o)
    wsum = jnp.sum(wt_o, axis=0, keepdims=True)
    idx_ref[...] = idx_o.astype(jnp.int32)
    wt_ref[...] = (ROUTED_SCALE * wt_o) / wsum
    rank_ref[...] = rk_o.astype(jnp.int32)


def _router(h2, router_w, router_bias):
    t, d = h2.shape
    ne = router_w.shape[1]
    tm = min(TM_ROUTE, t)
    k_spec = pl.BlockSpec((EXPERT_TOP_K, tm), lambda i: (0, i))
    return pl.pallas_call(
        _router_kernel,
        grid=(t // tm,),
        in_specs=[pl.BlockSpec((tm, d), lambda i: (i, 0)),
                  pl.BlockSpec((ne, d), lambda i: (0, 0)),
                  pl.BlockSpec((ne, 1), lambda i: (0, 0))],
        out_specs=[k_spec, k_spec, k_spec, pl.BlockSpec((ne, LANE), lambda i: (0, 0))],
        out_shape=[jax.ShapeDtypeStruct((EXPERT_TOP_K, t), jnp.int32),
                   jax.ShapeDtypeStruct((EXPERT_TOP_K, t), F32),
                   jax.ShapeDtypeStruct((EXPERT_TOP_K, t), jnp.int32),
                   jax.ShapeDtypeStruct((ne, LANE), F32)],
        scratch_shapes=[pltpu.VMEM((ne, LANE), F32)],
        compiler_params=_cparams(("arbitrary",)),
        name="moe_router",
    )(h2, router_w.T.astype(BF16), router_bias.reshape(ne, 1).astype(F32))


def _row_copy(src_ref, src_row, dst_ref, dst_row, sem):
    return pltpu.make_async_copy(src_ref.at[pl.ds(src_row, 1)], dst_ref.at[pl.ds(dst_row, 1)], sem)


def _pack_bf16_pairs(x):
    half = x.shape[1] // 2
    bits = lax.bitcast_convert_type(x.astype(BF16).astype(F32), jnp.uint32)
    return bits[:, :half] | (bits[:, half:] >> 16)


def _unpack_bf16_pairs(w):
    hi = lax.bitcast_convert_type(w & jnp.uint32(0xFFFF0000), F32)
    lo = lax.bitcast_convert_type(w << 16, F32)
    return hi, lo


def _row_tiles(words):
    return (words // LANE, LANE)


def _dispatch_kernel(start_ref, cnt_ref, dest_ref, h_ref, xs_ref, hp, zblk, sem, zsem):
    i = pl.program_id(0)
    n_steps = pl.num_programs(0)
    slot = i % 2
    tm = h_ref.shape[0]
    ne = start_ref.shape[0] - 1
    rb = zblk.shape[0]
    n_blocks = xs_ref.shape[0] // rb
    hp[slot] = _pack_bf16_pairs(h_ref[...]).reshape(hp.shape[1:])

    def pad_rows(e, fn):
        lo = start_ref[e] + cnt_ref[e]
        lax.fori_loop(lo, start_ref[e + 1], lambda r, c: fn(_row_copy(zblk, 0, xs_ref, r, zsem)) or c, 0)

    def tail_blocks(fn):
        def body(b, c):
            fn(pltpu.make_async_copy(zblk, xs_ref.at[pl.ds(pl.multiple_of(b * rb, rb), rb)], zsem))
            return c
        lax.fori_loop(start_ref[ne] // rb, n_blocks, body, 0)

    @pl.when(i == 0)
    def _():
        zblk[...] = jnp.zeros_like(zblk)
        lax.fori_loop(0, ne, lambda e, c: pad_rows(e, lambda cp: cp.start()) or c, 0)
        tail_blocks(lambda cp: cp.start())
        lax.fori_loop(0, ne, lambda e, c: pad_rows(e, lambda cp: cp.wait()) or c, 0)
        tail_blocks(lambda cp: cp.wait())

    def token_rows(s, fn):
        def body(r, c):
            for k in range(EXPERT_TOP_K):
                fn(_row_copy(hp.at[s], r, xs_ref, dest_ref[k, r], sem.at[s]), k)
            return c
        lax.fori_loop(0, tm, body, 0)

    token_rows(slot, lambda cp, k: cp.start(priority=k % DMA_PRIORITIES))

    @pl.when(i > 0)
    def _():
        token_rows(1 - slot, lambda cp, k: cp.wait())

    @pl.when(i == n_steps - 1)
    def _():
        token_rows(slot, lambda cp, k: cp.wait())


def _dispatch(starts, cnt, dest, h2, n_rows):
    t, d = h2.shape
    tm = min(TM_DISP, t)
    return pl.pallas_call(
        _dispatch_kernel,
        grid_spec=pltpu.PrefetchScalarGridSpec(
            num_scalar_prefetch=2,
            grid=(t // tm,),
            in_specs=[pl.BlockSpec((EXPERT_TOP_K, tm), lambda i, *_: (0, i), memory_space=pltpu.SMEM),
                      pl.BlockSpec((tm, d), lambda i, *_: (i, 0))],
            out_specs=pl.BlockSpec(memory_space=pl.ANY),
            scratch_shapes=[pltpu.VMEM((2, tm) + _row_tiles(d // 2), jnp.uint32),
                            pltpu.VMEM((min(ROW_BLOCK, n_rows),) + _row_tiles(d // 2), jnp.uint32),
                            pltpu.SemaphoreType.DMA((2,)), pltpu.SemaphoreType.DMA(())]),
        out_shape=jax.ShapeDtypeStruct((n_rows,) + _row_tiles(d // 2), jnp.uint32),
        compiler_params=_cparams(("arbitrary",)),
        name="moe_dispatch",
    )(starts, cnt, dest, h2)


def _mlp_kernel(be_ref, nb_ref, x_ref, wg_ref, wu_ref, wd_ref, y_ref, wg_s, wu_s, wd_s, *, packed):
    i = pl.program_id(0)

    @pl.when((i == 0) | (be_ref[i] != be_ref[jnp.maximum(i - 1, 0)]))
    def _():
        wg_s[...] = wg_ref[...].astype(BF16)
        wu_s[...] = wu_ref[...].astype(BF16)
        wd_s[...] = wd_ref[...].astype(BF16)

    @pl.when(i < nb_ref[0])
    def _():
        if packed:
            xw = x_ref[...]
            xw = xw.reshape(xw.shape[0], xw.shape[1] * xw.shape[2])
            x = jnp.concatenate(_unpack_bf16_pairs(xw), axis=1).astype(BF16)
        else:
            x = x_ref[...].astype(BF16)
        gt = jnp.dot(x, wg_s[...], preferred_element_type=F32)
        up = jnp.dot(x, wu_s[...], preferred_element_type=F32)
        hb = ((gt * _sigmoid(gt)) * up).astype(BF16)
        y = jnp.dot(hb, wd_s[...], preferred_element_type=F32)
        y_ref[...] = _pack_bf16_pairs(y).reshape(y_ref.shape) if packed else y

    @pl.when(i >= nb_ref[0])
    def _():
        y_ref[...] = jnp.zeros_like(y_ref)


def _grouped_mlp(block_expert, n_used, xs, layer, w_gate, w_up, w_down, packed, name):
    n_rows, row_shape = xs.shape[0], xs.shape[1:]
    zeros = (0,) * len(row_shape)
    d, hid = w_gate.shape[2], w_gate.shape[3]
    rb = min(ROW_BLOCK, n_rows)
    w_spec = lambda shape: pl.BlockSpec((None, None) + shape, lambda i, be, nb: (layer, be[i], 0, 0))
    return pl.pallas_call(
        functools.partial(_mlp_kernel, packed=packed),
        grid_spec=pltpu.PrefetchScalarGridSpec(
            num_scalar_prefetch=2,
            grid=(n_rows // rb,),
            in_specs=[pl.BlockSpec((rb,) + row_shape, lambda i, be, nb: (jnp.minimum(i, nb[0] - 1),) + zeros),
                      w_spec((d, hid)), w_spec((d, hid)), w_spec((hid, d))],
            out_specs=pl.BlockSpec((rb,) + row_shape, lambda i, be, nb: (i,) + zeros),
            scratch_shapes=[pltpu.VMEM((d, hid), BF16), pltpu.VMEM((d, hid), BF16), pltpu.VMEM((hid, d), BF16)]),
        out_shape=jax.ShapeDtypeStruct(xs.shape, xs.dtype if packed else F32),
        compiler_params=_cparams(("arbitrary",)),
        name=name,
    )(block_expert, n_used, xs, w_gate, w_up, w_down)


def _combine_kernel(dest_ref, dnext_ref, wt_ref, ysh_ref, x_ref, g2_ref, fn_ref, ys_ref, o_ref, ybuf, sem, *,
                    final):
    i = pl.program_id(0)
    n_steps = pl.num_programs(0)
    slot = i % 2
    tm, d = x_ref.shape

    def token_rows(rows_ref, s, fn):
        def body(r, c):
            for k in range(EXPERT_TOP_K):
                fn(_row_copy(ys_ref, rows_ref[k, r], ybuf.at[s].at[k], r, sem.at[s]), k)
            return c
        lax.fori_loop(0, tm, body, 0)

    start = lambda cp, k: cp.start(priority=k % DMA_PRIORITIES)

    @pl.when(i == 0)
    def _():
        token_rows(dest_ref, 0, start)

    @pl.when(i + 1 < n_steps)
    def _():
        token_rows(dnext_ref, 1 - slot, start)

    token_rows(dest_ref, slot, lambda cp, k: cp.wait())

    wt = wt_ref[...]
    acc_hi = jnp.zeros((tm, d // 2), F32)
    acc_lo = jnp.zeros((tm, d // 2), F32)
    for k in range(EXPERT_TOP_K):
        hi, lo = _unpack_bf16_pairs(ybuf[slot, k].reshape(tm, d // 2))
        acc_hi = acc_hi + wt[:, k:k + 1] * hi
        acc_lo = acc_lo + wt[:, k:k + 1] * lo
    acc = ysh_ref[...] + jnp.concatenate([acc_hi, acc_lo], axis=1)
    xn = x_ref[...] + g2_ref[...] * acc
    if final:
        ms = jnp.mean(xn * xn, axis=-1, keepdims=True)
        xn = (xn * lax.rsqrt(ms + EPS)) * fn_ref[...]
    o_ref[...] = xn


def _combine(dest, wts_t, y_sorted, y_shared, x2d, modr, mod_base, seq, final_norm, final):
    t, d = x2d.shape
    tm = min(TM_COMB, seq)
    row_spec = pl.BlockSpec((tm, d), lambda i: (i, 0))
    n_steps = t // tm
    return pl.pallas_call(
        functools.partial(_combine_kernel, final=final),
        grid=(n_steps,),
        in_specs=[pl.BlockSpec((EXPERT_TOP_K, tm), lambda i: (0, i), memory_space=pltpu.SMEM),
                  pl.BlockSpec((EXPERT_TOP_K, tm), lambda i: (0, jnp.minimum(i + 1, n_steps - 1)),
                               memory_space=pltpu.SMEM),
                  pl.BlockSpec((tm, EXPERT_TOP_K), lambda i: (i, 0)),
                  row_spec, row_spec,
                  pl.BlockSpec((None, 1, d), lambda i: (mod_base + ((i * tm) // seq) * 6 + 5, 0, 0)),
                  pl.BlockSpec((1, d), lambda i: (0, 0)),
                  pl.BlockSpec(memory_space=pl.ANY)],
        out_specs=row_spec,
        out_shape=jax.ShapeDtypeStruct((t, d), F32),
        scratch_shapes=[pltpu.VMEM((2, EXPERT_TOP_K, tm) + _row_tiles(d // 2), jnp.uint32),
                        pltpu.SemaphoreType.DMA((2,))],
        compiler_params=_cparams(("arbitrary",)),
        name="moe_combine",
    )(dest, dest, wts_t, y_shared, x2d, modr, final_norm.reshape(1, d), y_sorted)


def _layout(d_model):
    r, a, kvw = RNN_WIDTH, N_HEADS * HEAD_DIM, N_KV_GROUPS * HEAD_DIM
    off = {}
    off["ma"], off["mb"] = 0, d_model
    off["u"] = 2 * d_model
    off["g"] = off["u"] + r
    off["kc"] = off["g"] + r
    off["vc"] = off["kc"] + kvw
    off["nf"] = off["vc"] + kvw
    off["q"] = 0
    off["ks"], off["vs"], off["kw"], off["vw"] = a, a + kvw, a + 2 * kvw, a + 3 * kvw
    off["gn"] = a + 4 * kvw
    off["nb"] = off["gn"] + TN_IN
    assert off["nf"] % TN_IN == 0 and off["gn"] % TN_IN == 0
    return off


def _pack_w_in(w_in_l, d_model):
    r, a, kvw = RNN_WIDTH, N_HEADS * HEAD_DIM, N_KV_GROUPS * HEAD_DIM
    off = _layout(d_model)
    s_u, s_g, s_q = 0, r, 2 * r
    s_kv = s_q + a
    s_gn = s_kv + 6 * kvw
    s_ma = s_gn + 3 * N_HEADS
    s_mb = s_ma + d_model
    cols = [w_in_l[:, s_ma:s_mb], w_in_l[:, s_mb:s_mb + d_model], w_in_l[:, s_u:s_g], w_in_l[:, s_g:s_q],
            w_in_l[:, s_kv:s_kv + 2 * kvw],
            w_in_l[:, s_q:s_kv], w_in_l[:, s_kv + 2 * kvw:s_gn], w_in_l[:, s_gn:s_ma]]
    w = jnp.concatenate(cols, axis=1)
    return jnp.pad(w, ((0, 0), (0, off["nf"] + off["nb"] - w.shape[1]))).astype(BF16)


def _moe_plan(idx, rank, counts, n_tok):
    cnt = counts[:, 0].astype(jnp.int32)
    padded = (cnt + ROW_BLOCK - 1) // ROW_BLOCK * ROW_BLOCK
    ends = jnp.cumsum(padded)
    starts = jnp.concatenate([jnp.zeros((1,), jnp.int32), ends]).astype(jnp.int32)
    onehot = idx[:, :, None] == jnp.arange(N_EXPERTS, dtype=jnp.int32)[None, None, :]
    dest = rank + jnp.sum(jnp.where(onehot, starts[None, None, :N_EXPERTS], 0), axis=2)
    n_blocks = (n_tok * EXPERT_TOP_K) // ROW_BLOCK + N_EXPERTS
    blk_start = jnp.arange(n_blocks, dtype=jnp.int32) * ROW_BLOCK
    owner = jnp.sum((ends[None, :] <= blk_start[:, None]).astype(jnp.int32), axis=1)
    block_expert = jnp.minimum(owner, N_EXPERTS - 1).astype(jnp.int32)
    n_used = (ends[-1] // ROW_BLOCK).astype(jnp.int32).reshape(1)
    return starts, cnt, dest.astype(jnp.int32), block_expert, n_used, n_blocks * ROW_BLOCK


def kernel(x, c, rel_bias, final_norm, ada_w, ada_b, norm_mix, norm_ffn, w_in, conv_w, conv_b, lru_wa, lru_ba, lru_wx, lru_bx, lru_lambda, cmp_pe_k, cmp_w1_k, cmp_w2_k, cmp_pe_v, cmp_w1_v, cmp_w2_v, w_up_rnn, w_up_att, w_out, router_w, router_bias, exp_w_gate, exp_w_up, exp_w_down, sh_w_gate, sh_w_up, sh_w_down):
    batch, seq, d = x.shape
    n_tok = batch * seq
    depth = ada_w.shape[0]
    g_, dh = N_KV_GROUPS, HEAD_DIM
    a_w, kvw, r = N_HEADS * HEAD_DIM, N_KV_GROUPS * HEAD_DIM, RNN_WIDTH
    off = _layout(d)
    assert seq % CMP_STRIDE == 0 and CMP_BLOCK == 2 * CMP_STRIDE

    mod = _adaln_mod(c, ada_w, ada_b)
    modr = mod.reshape(depth * batch * 6, 1, d)
    tables = _bias_tables(rel_bias, seq)
    nc = tables[2].shape[1]
    n_chunk = seq // CMP_STRIDE

    x2d = x.reshape(n_tok, d)
    for l in range(depth):
        mod_base = l * batch * 6
        zf, zb, zg = _inproj(x2d, norm_mix[l], modr, mod_base, seq, _pack_w_in(w_in[l], d), off["nf"])
        y_rnn = _rglru(zf, batch, seq, off["u"] // r, off["g"] // r, conv_w[l], conv_b[l],
                       lru_wa[l], lru_ba[l], lru_wx[l], lru_bx[l], lru_lambda[l])

        zb3 = zb.reshape(batch, seq, -1)
        zcol = lambda z3, name, w: z3[:, :, off[name]:off[name] + w]

        def unfold(v):
            ch = jnp.transpose(v.reshape(batch, n_chunk, CMP_STRIDE, g_, dh), (0, 3, 1, 2, 4))
            ch = ch.reshape(batch * g_, n_chunk, CMP_STRIDE * dh)
            blocks = jnp.concatenate([ch[:, :-1], ch[:, 1:]], axis=2)
            return jnp.pad(blocks, ((0, 0), (0, nc - (n_chunk - 1)), (0, 0)))

        zf3 = zf.reshape(batch, seq, -1)
        kc, vc = _compress(unfold(zcol(zf3, "kc", kvw)), unfold(zcol(zf3, "vc", kvw)),
                           cmp_pe_k[l], cmp_w1_k[l], cmp_w2_k[l], cmp_pe_v[l], cmp_w1_v[l], cmp_w2_v[l])
        kc = jnp.transpose(kc.reshape(batch, g_, nc, dh), (0, 2, 1, 3)).reshape(batch, nc, kvw)
        vcT = jnp.swapaxes(vc.reshape(batch, g_, nc, dh), 2, 3).reshape(batch, kvw, nc)
        qT = jnp.swapaxes(zcol(zb3, "q", a_w), 1, 2)
        vsT = jnp.swapaxes(zcol(zb3, "vs", kvw), 1, 2)
        vwT = jnp.swapaxes(zcol(zb3, "vw", kvw), 1, 2)
        gT = jnp.swapaxes(zg.reshape(batch, seq, LANE)[:, :, :3 * N_HEADS], 1, 2)
        y_att = _attention(qT, kc, vcT, zb3, off["ks"] // kvw, off["kw"] // kvw, vsT, vwT, gT, tables,
                           batch, seq).reshape(n_tok, a_w)

        merged = _merge(y_rnn, y_att, w_up_rnn[l].astype(BF16), w_up_att[l].astype(BF16), zf,
                        off["ma"] // d, off["mb"] // d)
        x1, h2 = _outproj(merged, w_out[l].astype(BF16), x2d, modr, mod_base, seq, norm_ffn[l])

        idx, wts, rank, counts = _router(h2, router_w[l], router_bias[l])
        starts, cnt, dest, block_expert, n_used, n_rows = _moe_plan(idx, rank, counts, n_tok)
        xs = _dispatch(starts, cnt, dest, h2, n_rows)
        y_sorted = _grouped_mlp(block_expert, n_used, xs, l, exp_w_gate, exp_w_up, exp_w_down, True, "moe_experts")
        sh_blocks = n_tok // min(ROW_BLOCK, n_tok)
        y_shared = _grouped_mlp(jnp.zeros((sh_blocks,), jnp.int32), jnp.full((1,), sh_blocks, jnp.int32), h2, l,
                                sh_w_gate[:, None], sh_w_up[:, None], sh_w_down[:, None], False, "moe_shared")
        x2d = _combine(dest, wts.T, y_sorted, y_shared, x1, modr, mod_base, seq, final_norm, l == depth - 1)
    return x2d.reshape(batch, seq, d)
```

```python
import functools
import math

import numpy as np
import jax
import jax.numpy as jnp
from jax import lax
from jax.experimental import pallas as pl
from jax.experimental.pallas import tpu as pltpu

DEPTH = 2
RNN_WIDTH = 1024
RNN_BLOCKS = 8
CONV_WIDTH = 4
LRU_C = 8.0
N_HEADS = 16
N_KV_GROUPS = 4
HEAD_DIM = 64
CMP_BLOCK = 32
CMP_STRIDE = 16
CMP_HIDDEN = 128
SEL_BLOCK = 64
SEL_TOP_N = 8
WINDOW = 512
N_BUCKETS = 32
MAX_DISTANCE = 128
N_EXPERTS = 64
EXPERT_TOP_K = 8
N_EXPERT_GROUPS = 8
TOPK_EXPERT_GROUPS = 4
ROUTED_SCALE = 2.5
EPS = 1e-6
NEG = -1e30

LANE = 128
SUBLANE = 8
VMEM_LIMIT = 52 * 1024 * 1024
DMA_PRIORITIES = 2

TM_IN = 1024
TN_IN = 512
TN_MOD = 1024
TC_RNN = 256
TQ = 128
FAR_TILES = 4
ONES_ROWS = 16
TM_MERGE = 512
TN_MERGE = 2048
TM_OUT = 512
TM_ROUTE = 256
TM_DISP = 128
ROW_BLOCK = 256
TM_COMB = 128

F32 = jnp.float32
BF16 = jnp.bfloat16


def _cparams(sem):
    return pltpu.CompilerParams(dimension_semantics=sem, vmem_limit_bytes=VMEM_LIMIT)


def _round_up(a, b):
    return (a + b - 1) // b * b


def _tile(n, pref):
    if n <= pref:
        return n
    t = pref // LANE * LANE
    while n % t:
        t -= LANE
    return t


def _gelu_tanh(x):
    return x * (0.5 * (1.0 + jnp.tanh(math.sqrt(2.0 / math.pi) * (x + 0.044715 * (x * x * x)))))


def _sigmoid(x):
    return jax.nn.sigmoid(x)


def _mod_kernel(c_ref, w_ref, b_ref, o_ref):
    c = c_ref[...]
    ca = (c * _sigmoid(c)).astype(BF16)
    o_ref[...] = jnp.dot(ca, w_ref[...].astype(BF16), preferred_element_type=F32) + b_ref[...]


def _adaln_mod(c, ada_w, ada_b):
    nl, d, n6 = ada_w.shape
    b = c.shape[0]
    tn = _tile(n6, TN_MOD)
    return pl.pallas_call(
        _mod_kernel,
        grid=(nl, n6 // tn),
        in_specs=[pl.BlockSpec((b, d), lambda l, j: (0, 0)),
                  pl.BlockSpec((None, d, tn), lambda l, j: (l, 0, j)),
                  pl.BlockSpec((None, 1, tn), lambda l, j: (l, 0, j))],
        out_specs=pl.BlockSpec((None, b, tn), lambda l, j: (l, 0, j)),
        out_shape=jax.ShapeDtypeStruct((nl, b, n6), F32),
        compiler_params=_cparams(("arbitrary", "arbitrary")),
        name="adaln_mod",
    )(c, ada_w, ada_b.reshape(nl, 1, n6))


def _inproj_kernel(x_ref, nw_ref, sc_ref, sh_ref, w_ref, zf_ref, zb_ref, zg_ref, h_ref, *, nf, nj):
    j = pl.program_id(1)

    @pl.when(j == 0)
    def _():
        x = x_ref[...]
        ms = jnp.mean(x * x, axis=-1, keepdims=True)
        y = (x * lax.rsqrt(ms + EPS)) * nw_ref[...]
        h_ref[...] = (y * (1.0 + sc_ref[...]) + sh_ref[...]).astype(BF16)

    res = jnp.dot(h_ref[...], w_ref[...], preferred_element_type=F32)

    @pl.when(j < nf)
    def _():
        zf_ref[...] = res

    @pl.when(j >= nf)
    def _():
        zb_ref[...] = res.astype(BF16)

    @pl.when(j == nj - 1)
    def _():
        zg_ref[...] = res[:, 0:LANE]


def _inproj(x2d, norm_w, modr, mod_base, seq, w_p, n_f32):
    t, d = x2d.shape
    n_p = w_p.shape[1]
    tm = min(TM_IN, seq)
    tn = TN_IN
    assert n_f32 % tn == 0 and n_p % tn == 0
    nf, nj = n_f32 // tn, n_p // tn

    def mod_idx(k):
        return lambda i, j: (mod_base + ((i * tm) // seq) * 6 + k, 0, 0)

    return pl.pallas_call(
        functools.partial(_inproj_kernel, nf=nf, nj=nj),
        grid=(t // tm, nj),
        in_specs=[pl.BlockSpec((tm, d), lambda i, j: (i, 0)),
                  pl.BlockSpec((1, d), lambda i, j: (0, 0)),
                  pl.BlockSpec((None, 1, d), mod_idx(1)),
                  pl.BlockSpec((None, 1, d), mod_idx(0)),
                  pl.BlockSpec((d, tn), lambda i, j: (0, j))],
        out_specs=[pl.BlockSpec((tm, tn), lambda i, j: (i, jnp.minimum(j, nf - 1))),
                   pl.BlockSpec((tm, tn), lambda i, j: (i, jnp.maximum(j - nf, 0))),
                   pl.BlockSpec((tm, LANE), lambda i, j: (i, 0))],
        out_shape=[jax.ShapeDtypeStruct((t, n_f32), F32), jax.ShapeDtypeStruct((t, n_p - n_f32), BF16),
                   jax.ShapeDtypeStruct((t, LANE), F32)],
        scratch_shapes=[pltpu.VMEM((tm, d), BF16)],
        compiler_params=_cparams(("arbitrary", "arbitrary")),
        name="inproj",
    )(x2d, norm_w.reshape(1, d), modr, modr, w_p)


def _rglru_kernel(u_ref, g_ref, cw_ref, cb_ref, wa_ref, ba_ref, wx_ref, bx_ref, lam_ref, y_ref,
                  ubuf, a_s, b_s, h_s, hcar):
    tc, r = u_ref.shape
    nb = wa_ref.shape[0]
    bw = r // nb

    @pl.when(pl.program_id(1) == 0)
    def _():
        ubuf[0:SUBLANE, :] = jnp.zeros((SUBLANE, r), F32)
        hcar[...] = jnp.zeros_like(hcar)

    ubuf[SUBLANE:SUBLANE + tc, :] = u_ref[...]
    cw = cw_ref[...]
    uc = cb_ref[...] + cw[CONV_WIDTH - 1:CONV_WIDTH, :] * ubuf[SUBLANE:SUBLANE + tc, :]
    for k in range(CONV_WIDTH - 1):
        off = SUBLANE - (CONV_WIDTH - 1) + k
        uc = uc + cw[k:k + 1, :] * ubuf[off:off + tc, :]
    ubuf[0:SUBLANE, :] = ubuf[tc:tc + SUBLANE, :]

    ucb = uc.astype(BF16)
    rp = []
    xp = []
    for n in range(nb):
        blk = ucb[:, n * bw:(n + 1) * bw]
        rp.append(jnp.dot(blk, wa_ref[n], preferred_element_type=F32))
        xp.append(jnp.dot(blk, wx_ref[n], preferred_element_type=F32))
    rg = _sigmoid(jnp.concatenate(rp, axis=1) + ba_ref[...])
    ig = _sigmoid(jnp.concatenate(xp, axis=1) + bx_ref[...])
    nl = -lam_ref[...]
    sp = jnp.maximum(nl, 0.0) + jnp.log1p(jnp.exp(-jnp.abs(nl)))
    log_a = (-LRU_C * rg) * sp
    a_s[...] = jnp.exp(log_a)
    th = jnp.tanh(log_a)
    one_minus_a2 = (-2.0 * th) / (1.0 - th)
    b_s[...] = jnp.sqrt(one_minus_a2) * (ig * uc)

    def step(t, h):
        h = a_s[pl.ds(t, 1), :] * h + b_s[pl.ds(t, 1), :]
        h_s[pl.ds(t, 1), :] = h
        return h

    h_last = lax.fori_loop(0, tc, step, hcar[0:1, :], unroll=8)
    hcar[0:1, :] = h_last
    y_ref[...] = (_gelu_tanh(g_ref[...]) * h_s[...]).astype(y_ref.dtype)


def _rglru(z2d, batch, seq, u_blk, g_blk, conv_w, conv_b, wa, ba, wx, bx, lam):
    r = conv_w.shape[1]
    tc = min(TC_RNN, seq)
    nt = seq // tc
    nb, bw, _ = wa.shape
    row = lambda v: v.reshape(1, r)
    full = lambda shape: pl.BlockSpec(shape, lambda b, t: (0,) * len(shape))
    return pl.pallas_call(
        _rglru_kernel,
        grid=(batch, nt),
        in_specs=[pl.BlockSpec((tc, r), lambda b, t: (b * nt + t, u_blk)),
                  pl.BlockSpec((tc, r), lambda b, t: (b * nt + t, g_blk)),
                  full((CONV_WIDTH, r)), full((1, r)),
                  full((nb, bw, bw)), full((1, r)),
                  full((nb, bw, bw)), full((1, r)), full((1, r))],
        out_specs=pl.BlockSpec((tc, r), lambda b, t: (b * nt + t, 0)),
        out_shape=jax.ShapeDtypeStruct((batch * seq, r), BF16),
        scratch_shapes=[pltpu.VMEM((tc + SUBLANE, r), F32), pltpu.VMEM((tc, r), F32),
                        pltpu.VMEM((tc, r), F32), pltpu.VMEM((tc, r), F32),
                        pltpu.VMEM((SUBLANE, r), F32)],
        compiler_params=_cparams(("arbitrary", "arbitrary")),
        name="rglru",
    )(z2d, z2d, conv_w, row(conv_b), wa.astype(BF16), row(ba), wx.astype(BF16), row(bx), row(lam))


def _compress_kernel(xk_ref, xv_ref, pek_ref, w1k_ref, w2k_ref, pev_ref, w1v_ref, w2v_ref, kc_ref, vc_ref):
    def one(x_ref, pe_ref, w1_ref, w2_ref, o_ref):
        blocks = (x_ref[...] + pe_ref[...]).astype(BF16)
        hid = _gelu_tanh(jnp.dot(blocks, w1_ref[...], preferred_element_type=F32))
        o_ref[...] = jnp.dot(hid.astype(BF16), w2_ref[...], preferred_element_type=F32).astype(o_ref.dtype)

    one(xk_ref, pek_ref, w1k_ref, w2k_ref, kc_ref)
    one(xv_ref, pev_ref, w1v_ref, w2v_ref, vc_ref)


def _compress(xk, xv, pe_k, w1_k, w2_k, pe_v, w1_v, w2_v):
    bg, nc, kd = xk.shape
    dh = w2_k.shape[1]
    hid = w2_k.shape[0]
    x_spec = pl.BlockSpec((None, nc, kd), lambda i: (i, 0, 0))
    full = lambda shape: pl.BlockSpec(shape, lambda i: (0,) * len(shape))
    o_spec = pl.BlockSpec((None, nc, dh), lambda i: (i, 0, 0))
    prep = lambda pe, w1, w2: (pe.reshape(1, kd), w1.reshape(kd, hid).astype(BF16), w2.astype(BF16))
    return pl.pallas_call(
        _compress_kernel,
        grid=(bg,),
        in_specs=[x_spec, x_spec, full((1, kd)), full((kd, hid)), full((hid, dh)),
                  full((1, kd)), full((kd, hid)), full((hid, dh))],
        out_specs=[o_spec, o_spec],
        out_shape=[jax.ShapeDtypeStruct((bg, nc, dh), BF16)] * 2,
        compiler_params=_cparams(("arbitrary",)),
        name="nsa_compress",
    )(xk, xv, *prep(pe_k, w1_k, w2_k), *prep(pe_v, w1_v, w2_v))


def _col_max(x):
    return jnp.max(x, axis=0, keepdims=True)


def _attn_kernel(qT_ref, kc_ref, vcT_ref, ks_ref, kw_ref, vsT_ref, vwT_ref, gT_ref,
                 tnear_ref, cfar_ref, cbias_ref, ovl_ref, o_ref,
                 qbd, sel_s, m_s, acc_s):
    i = pl.program_id(1)
    g_, hg, dh = N_KV_GROUPS, N_HEADS // N_KV_GROUPS, HEAD_DIM
    tq = qT_ref.shape[1]
    gw = hg * tq
    nw = g_ * gw
    n_sel = sel_s.shape[0]
    per = tq // SEL_BLOCK

    qT = (qT_ref[...].astype(F32) * (HEAD_DIM ** -0.5)).astype(BF16)
    zero_blk = jnp.zeros((dh, gw), BF16)
    rows = []
    for g in range(g_):
        qcat = jnp.concatenate([qT[(g * hg + h) * dh:(g * hg + h + 1) * dh, :] for h in range(hg)], axis=1)
        rows.append(jnp.concatenate([zero_blk] * g + [qcat] + [zero_blk] * (g_ - 1 - g), axis=1))
    qbd[...] = jnp.concatenate(rows, axis=0)

    def lane_q(shape):
        return lax.broadcasted_iota(jnp.int32, shape, 1) % tq

    def pv(vT_ref, start, nrows, p):
        vr = vT_ref.shape[0] // g_
        pb = p.astype(BF16)
        outs = []
        for g in range(g_):
            if start is None:
                v = vT_ref[g * vr:(g + 1) * vr, :]
            else:
                v = vT_ref[g * vr:(g + 1) * vr, pl.ds(start, nrows)]
            outs.append(jnp.dot(v, pb[:, g * gw:(g + 1) * gw], preferred_element_type=F32))
        return jnp.concatenate(outs, axis=1)

    nc = kc_ref.shape[0]
    near_t = jnp.maximum(i - 1, 0)
    near_start = pl.multiple_of(near_t * tq, tq)
    far_end = near_t
    wr = WINDOW - tq
    win_start = pl.multiple_of(jnp.maximum(i - WINDOW // tq, 0) * tq, tq)
    lhs = jnp.concatenate([kc_ref[...], ks_ref[pl.ds(near_start, 2 * tq), :],
                           kw_ref[pl.ds(near_start, 2 * tq), :], kw_ref[pl.ds(win_start, wr), :]], axis=0)
    s_all = jnp.dot(lhs, qbd[...], preferred_element_type=F32)
    r_slc, r_win, r_far = nc, nc + 2 * tq, nc + 4 * tq

    sc = s_all[0:nc] + cbias_ref[...]
    nrow = lax.broadcasted_iota(jnp.int32, (nc, nw), 0)
    cmask = (nrow * CMP_STRIDE + (CMP_BLOCK - 1)) <= i * tq + lane_q((nc, nw))
    sc = jnp.where(cmask, sc, NEG)
    pc = jnp.where(cmask, jnp.exp(sc - _col_max(sc)), 0.0)
    lc = jnp.sum(pc, axis=0, keepdims=True)
    pc = pc * jnp.where(lc > 0.0, 1.0 / lc, 0.0)
    o_cmp = pv(vcT_ref, None, nc, pc)

    psum = jnp.concatenate(
        [sum(pc[:, g * gw + h * tq:g * gw + (h + 1) * tq] for h in range(hg)) for g in range(g_)], axis=1)
    imp = jnp.dot(ovl_ref[...], psum, preferred_element_type=F32, precision=lax.Precision.HIGHEST)
    jrow = lax.broadcasted_iota(jnp.int32, (n_sel, g_ * tq), 0)
    tq_abs = i * tq + lane_q((n_sel, g_ * tq))
    cur = tq_abs // SEL_BLOCK
    forced = (jrow == 0) | (jrow == cur) | (jrow == cur - 1)
    valid = jrow * SEL_BLOCK <= tq_abs
    work = jnp.where(forced, jnp.inf, jnp.where(valid, imp, -jnp.inf))
    jrow_f = jrow.astype(F32)
    sel = jnp.zeros((n_sel, g_ * tq), F32)
    for _ in range(min(SEL_TOP_N, n_sel)):
        mx = _col_max(work)
        first = jnp.min(jnp.where(work == mx, jrow_f, float(n_sel)), axis=0, keepdims=True)
        pick = jrow_f == first
        sel = jnp.where(pick, 1.0, sel)
        work = jnp.where(pick, -jnp.inf, work)
    sel_s[...] = jnp.where(sel > 0.5, 0.0, NEG)

    def sel_add(first_blk, n_blk, limit_blk=None, bias_row=None):
        parts = []
        for c in range(n_blk):
            rowv = sel_s[pl.ds(first_blk + c, 1), :]
            if limit_blk is not None:
                rowv = rowv + jnp.where(first_blk + c < limit_blk, 0.0, NEG)
            rowv = jnp.concatenate([rowv[:, g * tq:(g + 1) * tq] for g in range(g_) for _ in range(hg)], axis=1)
            if bias_row is not None:
                rowv = rowv + bias_row
            parts.append(jnp.broadcast_to(rowv, (SEL_BLOCK, nw)))
        return jnp.concatenate(parts, axis=0)

    def scores(k_ref, start, nrows):
        return jnp.dot(k_ref[pl.ds(start, nrows), :], qbd[...], preferred_element_type=F32)

    def flash_init(s, vT_ref, start, nrows):
        m = _col_max(s)
        m_s[...] = m
        acc_s[...] = pv(vT_ref, start, nrows, jnp.exp(s - m))

    def flash_update(s, vT_ref, start, nrows):
        m_old = m_s[...]
        m_new = jnp.maximum(m_old, _col_max(s))
        m_s[...] = m_new
        acc_s[...] = jnp.exp(m_old - m_new) * acc_s[...] + pv(vT_ref, start, nrows, jnp.exp(s - m_new))

    def flash_result():
        acc = acc_s[...]
        return acc[0:dh] * (1.0 / acc[dh:dh + 1])

    bias_off = pl.multiple_of(jnp.where(i == 0, tq, 0), tq)
    near_add = tnear_ref[pl.ds(bias_off, 2 * tq), :]

    flash_init(s_all[r_slc:r_win] + near_add + sel_add(near_t * per, 2 * per), vsT_ref, near_start, 2 * tq)
    fr = FAR_TILES * tq

    def far_slc(c, carry):
        hi_t = far_end - c * FAR_TILES
        st_t = jnp.maximum(hi_t - FAR_TILES, 0)
        start = pl.multiple_of(st_t * tq, tq)
        add = sel_add(st_t * per, FAR_TILES * per, hi_t * per, cfar_ref[...])
        flash_update(scores(ks_ref, start, fr) + add, vsT_ref, start, fr)
        return carry

    lax.fori_loop(0, (far_end + FAR_TILES - 1) // FAR_TILES, far_slc, 0)
    o_slc = flash_result()

    flash_init(s_all[r_win:r_far] + near_add, vwT_ref, near_start, 2 * tq)
    nwin = WINDOW // tq
    parts = []
    for c in range(wr // tq):
        tile_ok = win_start // tq + c < far_end
        part = s_all[r_far + c * tq:r_far + (c + 1) * tq] + (cfar_ref[...] + jnp.where(tile_ok, 0.0, NEG))
        if c == 0:
            krow = lax.broadcasted_iota(jnp.int32, (tq, nw), 0)
            part = part + jnp.where(krow > lane_q((tq, nw)) - jnp.where(i < nwin, tq, 0), 0.0, NEG)
        parts.append(part)
    flash_update(jnp.concatenate(parts, axis=0), vwT_ref, win_start, wr)
    o_win = flash_result()

    gate = _sigmoid(gT_ref[...])

    def grow(j):
        return jnp.concatenate([gate[h * 3 + j:h * 3 + j + 1, :] for h in range(N_HEADS)], axis=1)

    o_t = grow(0) * o_cmp + grow(1) * o_slc + grow(2) * o_win
    o_hd = jnp.concatenate([o_t[:, h * tq:(h + 1) * tq] for h in range(N_HEADS)], axis=0)
    o_ref[...] = o_hd.T.astype(o_ref.dtype)


def _t5_bucket_np(dist):
    n = np.maximum(dist, 0)
    max_exact = N_BUCKETS // 2
    nf = np.maximum(n, 1).astype(np.float64)
    large = max_exact + (np.log(nf / max_exact) / math.log(MAX_DISTANCE / max_exact)
                         * (N_BUCKETS - max_exact)).astype(np.int64)
    large = np.minimum(large, N_BUCKETS - 1)
    return np.where(n < max_exact, n, large).astype(np.int32)


def _bias_tables(rel_bias, seq):
    tq = min(TQ, seq)
    assert MAX_DISTANCE <= tq, "tiles two or more behind the diagonal must all fall in the last bucket"
    nh = N_HEADS
    nc = _round_up((seq - CMP_BLOCK) // CMP_STRIDE + 1, LANE)
    off_max = (nc - 1) * CMP_STRIDE + CMP_BLOCK - 1
    fd = rel_bias.astype(F32)[_t5_bucket_np(np.arange(-off_max, seq))]
    fdT = fd.T
    kk = np.arange(tq)[:, None]
    qq = np.arange(tq)[None, :]

    def toeplitz(offset):
        idx = off_max + np.maximum(offset + qq - kk, 0)
        return jnp.transpose(fd[idx], (0, 2, 1)).reshape(tq, nh * tq)

    future = jnp.asarray(np.tile(np.where(kk <= qq, 0.0, NEG).astype(np.float32), (1, nh)))
    tnear = jnp.concatenate([toeplitz(tq), toeplitz(0) + future, jnp.full((tq, nh * tq), NEG, F32)], axis=0)
    cfar = jnp.broadcast_to(rel_bias.astype(F32)[N_BUCKETS - 1][:, None], (nh, tq)).reshape(1, nh * tq)
    rows = [lax.slice_in_dim(fdT, off_max - (n * CMP_STRIDE + CMP_BLOCK - 1),
                             off_max - (n * CMP_STRIDE + CMP_BLOCK - 1) + seq, axis=1) for n in range(nc)]
    cb = jnp.stack(rows, axis=0).reshape(nc, nh, seq // tq, tq)
    cbias = jnp.transpose(cb, (2, 0, 1, 3)).reshape(seq // tq, nc, nh * tq)
    n_sel = seq // SEL_BLOCK
    cmp_start = np.arange(nc) * CMP_STRIDE
    sel_start = np.arange(n_sel) * SEL_BLOCK
    ovl = ((cmp_start[None, :] < sel_start[:, None] + SEL_BLOCK)
           & (cmp_start[None, :] + CMP_BLOCK > sel_start[:, None])).astype(np.float32)
    return tnear, cfar, cbias, jnp.asarray(ovl)


def _attention(qT, kc, vcT, zb3, ks_blk, kw_blk, vsT, vwT, gT, tables, batch, seq):
    tnear, cfar, cbias, ovl = tables
    g_, dh = N_KV_GROUPS, HEAD_DIM
    tq = min(TQ, seq)
    assert seq % tq == 0 and seq >= FAR_TILES * tq and WINDOW % tq == 0 and tq % SEL_BLOCK == 0
    nqt = seq // tq
    nw = N_HEADS * tq
    nc = kc.shape[1]
    n_sel = seq // SEL_BLOCK
    kvw = g_ * dh
    per_b = lambda shape: pl.BlockSpec((None,) + shape, lambda b, i: (b,) + (0,) * len(shape))
    full = lambda shape: pl.BlockSpec(shape, lambda b, i: (0,) * len(shape))
    return pl.pallas_call(
        _attn_kernel,
        grid=(batch, nqt),
        in_specs=[pl.BlockSpec((None, N_HEADS * dh, tq), lambda b, i: (b, 0, i)),
                  per_b((nc, kvw)),
                  per_b((kvw, nc)),
                  pl.BlockSpec((None, seq, kvw), lambda b, i: (b, 0, ks_blk)),
                  pl.BlockSpec((None, seq, kvw), lambda b, i: (b, 0, kw_blk)),
                  per_b((vsT.shape[1], seq)), per_b((vwT.shape[1], seq)),
                  pl.BlockSpec((None, 3 * N_HEADS, tq), lambda b, i: (b, 0, i)),
                  full((3 * tq, nw)), full((1, nw)),
                  pl.BlockSpec((None, nc, nw), lambda b, i: (i, 0, 0)),
                  full((n_sel, nc))],
        out_specs=pl.BlockSpec((None, tq, N_HEADS * dh), lambda b, i: (b, i, 0)),
        out_shape=jax.ShapeDtypeStruct((batch, seq, N_HEADS * dh), BF16),
        scratch_shapes=[pltpu.VMEM((kvw, nw), BF16), pltpu.VMEM((n_sel, g_ * tq), F32),
                        pltpu.VMEM((1, nw), F32), pltpu.VMEM((vsT.shape[1] // g_, nw), F32)],
        compiler_params=_cparams(("arbitrary", "arbitrary")),
        name="nsa_attention",
    )(qT, kc, vcT, zb3, zb3, vsT, vwT, gT, tnear, cfar, cbias, ovl)


def _merge_kernel(yr_ref, ya_ref, wr_ref, wa_ref, ma_ref, mb_ref, o_ref):
    pr = jnp.dot(yr_ref[...], wr_ref[...], preferred_element_type=F32)
    pa = jnp.dot(ya_ref[...], wa_ref[...], preferred_element_type=F32)
    o_ref[...] = (_sigmoid(ma_ref[...]) * pr + _sigmoid(mb_ref[...]) * pa).astype(o_ref.dtype)


def _merge(y_rnn, y_att, w_ur, w_ua, z2d, ma_blk, mb_blk):
    t, r = y_rnn.shape
    a = y_att.shape[1]
    d = w_ur.shape[1]
    tm = min(TM_MERGE, t)
    tn = min(TN_MERGE, d)
    nj = d // tn
    return pl.pallas_call(
        _merge_kernel,
        grid=(t // tm, nj),
        in_specs=[pl.BlockSpec((tm, r), lambda i, j: (i, 0)),
                  pl.BlockSpec((tm, a), lambda i, j: (i, 0)),
                  pl.BlockSpec((r, tn), lambda i, j: (0, j)),
                  pl.BlockSpec((a, tn), lambda i, j: (0, j)),
                  pl.BlockSpec((tm, tn), lambda i, j: (i, ma_blk * nj + j)),
                  pl.BlockSpec((tm, tn), lambda i, j: (i, mb_blk * nj + j))],
        out_specs=pl.BlockSpec((tm, tn), lambda i, j: (i, j)),
        out_shape=jax.ShapeDtypeStruct((t, d), BF16),
        compiler_params=_cparams(("arbitrary", "arbitrary")),
        name="merge",
    )(y_rnn, y_att, w_ur, w_ua, z2d, z2d)


def _outproj_kernel(mg_ref, w_ref, x_ref, g1_ref, nw_ref, sc_ref, sh_ref, x1_ref, h2_ref):
    x1 = x_ref[...] + g1_ref[...] * jnp.dot(mg_ref[...], w_ref[...], preferred_element_type=F32)
    x1_ref[...] = x1
    ms = jnp.mean(x1 * x1, axis=-1, keepdims=True)
    y = (x1 * lax.rsqrt(ms + EPS)) * nw_ref[...]
    h2_ref[...] = (y * (1.0 + sc_ref[...]) + sh_ref[...]).astype(h2_ref.dtype)


def _outproj(merged, w_out, x2d, modr, mod_base, seq, norm_w):
    t, d = x2d.shape
    tm = min(TM_OUT, seq)

    def mod_idx(k):
        return lambda i: (mod_base + ((i * tm) // seq) * 6 + k, 0, 0)

    row_spec = pl.BlockSpec((tm, d), lambda i: (i, 0))
    return pl.pallas_call(
        _outproj_kernel,
        grid=(t // tm,),
        in_specs=[row_spec, pl.BlockSpec((d, d), lambda i: (0, 0)), row_spec,
                  pl.BlockSpec((None, 1, d), mod_idx(2)),
                  pl.BlockSpec((1, d), lambda i: (0, 0)),
                  pl.BlockSpec((None, 1, d), mod_idx(4)),
                  pl.BlockSpec((None, 1, d), mod_idx(3))],
        out_specs=[row_spec, row_spec],
        out_shape=[jax.ShapeDtypeStruct((t, d), F32), jax.ShapeDtypeStruct((t, d), BF16)],
        compiler_params=_cparams(("arbitrary",)),
        name="outproj",
    )(merged, w_out, x2d, modr, norm_w.reshape(1, d), modr, modr)


def _router_kernel(h_ref, rw_ref, rb_ref, idx_ref, wt_ref, rank_ref, cnt_ref, carry):
    ne = rw_ref.shape[0]
    tm = h_ref.shape[0]
    per = ne // N_EXPERT_GROUPS
    assert per == SUBLANE, "one expert group per sublane tile"

    @pl.when(pl.program_id(0) == 0)
    def _():
        carry[...] = jnp.zeros_like(carry)

    logits = lax.dot_general(rw_ref[...], h_ref[...].astype(BF16), (((1,), (1,)), ((), ())),
                             preferred_element_type=F32)
    scores = _sigmoid(logits)
    biased = scores + rb_ref[...]
    erow = lax.broadcasted_iota(jnp.int32, (ne, tm), 0).astype(F32)
    grow = lax.broadcasted_iota(jnp.int32, (ne, tm), 0) // per

    gparts = []
    sub = lax.broadcasted_iota(jnp.int32, (per, tm), 0).astype(F32)
    for gi in range(N_EXPERT_GROUPS):
        xg = biased[gi * per:(gi + 1) * per, :]
        m1 = _col_max(xg)
        f1 = jnp.min(jnp.where(xg == m1, sub, float(per)), axis=0, keepdims=True)
        m2 = _col_max(jnp.where(sub == f1, -jnp.inf, xg))
        gparts.append(jnp.broadcast_to(m1 + m2, (per, tm)))
    gscore = jnp.concatenate(gparts, axis=0)

    kparts = []
    for gi in range(N_EXPERT_GROUPS):
        gs = gscore[gi * per:gi * per + 1, :]
        beats = (gscore > gs) | ((gscore == gs) & (grow < gi))
        nbeat = jnp.sum(beats.astype(F32), axis=0, keepdims=True)
        kparts.append(jnp.broadcast_to(nbeat < float(TOPK_EXPERT_GROUPS * per), (per, tm)))
    gkeep = jnp.concatenate(kparts, axis=0)

    work = jnp.where(gkeep, biased, -jnp.inf)
    picks = []
    chosen = jnp.zeros((ne, tm), F32)
    for _ in range(EXPERT_TOP_K):
        mx = _col_max(work)
        first = jnp.min(jnp.where(work == mx, erow, float(ne)), axis=0, keepdims=True)
        pick = erow == first
        picks.append(pick)
        chosen = jnp.where(pick, 1.0, chosen)
        work = jnp.where(pick, -jnp.inf, work)

    tri = (lax.broadcasted_iota(jnp.int32, (tm, tm), 0) < lax.broadcasted_iota(jnp.int32, (tm, tm), 1))
    before = jnp.dot(chosen.astype(BF16), tri.astype(BF16), preferred_element_type=F32)
    pos = before + carry[:, 0:1]
    new_carry = carry[:, 0:1] + jnp.sum(chosen, axis=1, keepdims=True)
    carry[...] = jnp.broadcast_to(new_carry, carry.shape)
    cnt_ref[...] = carry[...]

    krow = lax.broadcasted_iota(jnp.int32, (EXPERT_TOP_K, tm), 0)
    idx_o = jnp.zeros((EXPERT_TOP_K, tm), F32)
    wt_o = jnp.zeros((EXPERT_TOP_K, tm), F32)
    rk_o = jnp.zeros((EXPERT_TOP_K, tm), F32)
    for k, pick in enumerate(picks):
        sel = lambda v: jnp.sum(jnp.where(pick, v, 0.0), axis=0, keepdims=True)
        idx_o = jnp.where(krow == k, sel(erow), idx_o)
        wt_o = jnp.where(krow == k, sel(scores), wt_o)
        rk_o = jnp.where(krow == k, sel(pos), rk_o)
    wsum = jnp.sum(wt_o, axis=0, keepdims=True)
    idx_ref[...] = idx_o.astype(jnp.int32)
    wt_ref[...] = (ROUTED_SCALE * wt_o) / wsum
    rank_ref[...] = rk_o.astype(jnp.int32)


def _router(h2, router_w, router_bias):
    t, d = h2.shape
    ne = router_w.shape[1]
    tm = min(TM_ROUTE, t)
    k_spec = pl.BlockSpec((EXPERT_TOP_K, tm), lambda i: (0, i))
    return pl.pallas_call(
        _router_kernel,
        grid=(t // tm,),
        in_specs=[pl.BlockSpec((tm, d), lambda i: (i, 0)),
                  pl.BlockSpec((ne, d), lambda i: (0, 0)),
                  pl.BlockSpec((ne, 1), lambda i: (0, 0))],
        out_specs=[k_spec, k_spec, k_spec, pl.BlockSpec((ne, LANE), lambda i: (0, 0))],
        out_shape=[jax.ShapeDtypeStruct((EXPERT_TOP_K, t), jnp.int32),
                   jax.ShapeDtypeStruct((EXPERT_TOP_K, t), F32),
                   jax.ShapeDtypeStruct((EXPERT_TOP_K, t), jnp.int32),
                   jax.ShapeDtypeStruct((ne, LANE), F32)],
        scratch_shapes=[pltpu.VMEM((ne, LANE), F32)],
        compiler_params=_cparams(("arbitrary",)),
        name="moe_router",
    )(h2, router_w.T.astype(BF16), router_bias.reshape(ne, 1).astype(F32))


def _row_copy(src_ref, src_row, dst_ref, dst_row, sem):
    return pltpu.make_async_copy(src_ref.at[pl.ds(src_row, 1)], dst_ref.at[pl.ds(dst_row, 1)], sem)


def _pack_bf16_pairs(x):
    half = x.shape[1] // 2
    bits = lax.bitcast_convert_type(x.astype(BF16).astype(F32), jnp.uint32)
    return bits[:, :half] | (bits[:, half:] >> 16)


def _unpack_bf16_pairs(w):
    hi = lax.bitcast_convert_type(w & jnp.uint32(0xFFFF0000), F32)
    lo = lax.bitcast_convert_type(w << 16, F32)
    return hi, lo


def _row_tiles(words):
    return (words // LANE, LANE)


def _dispatch_kernel(start_ref, cnt_ref, dest_ref, h_ref, xs_ref, hp, zblk, sem, zsem):
    i = pl.program_id(0)
    n_steps = pl.num_programs(0)
    slot = i % 2
    tm = h_ref.shape[0]
    ne = start_ref.shape[0] - 1
    rb = zblk.shape[0]
    n_blocks = xs_ref.shape[0] // rb
    hp[slot] = _pack_bf16_pairs(h_ref[...]).reshape(hp.shape[1:])

    def pad_rows(e, fn):
        lo = start_ref[e] + cnt_ref[e]
        lax.fori_loop(lo, start_ref[e + 1], lambda r, c: fn(_row_copy(zblk, 0, xs_ref, r, zsem)) or c, 0)

    def tail_blocks(fn):
        def body(b, c):
            fn(pltpu.make_async_copy(zblk, xs_ref.at[pl.ds(pl.multiple_of(b * rb, rb), rb)], zsem))
            return c
        lax.fori_loop(start_ref[ne] // rb, n_blocks, body, 0)

    @pl.when(i == 0)
    def _():
        zblk[...] = jnp.zeros_like(zblk)
        lax.fori_loop(0, ne, lambda e, c: pad_rows(e, lambda cp: cp.start()) or c, 0)
        tail_blocks(lambda cp: cp.start())
        lax.fori_loop(0, ne, lambda e, c: pad_rows(e, lambda cp: cp.wait()) or c, 0)
        tail_blocks(lambda cp: cp.wait())

    def token_rows(s, fn):
        def body(r, c):
            for k in range(EXPERT_TOP_K):
                fn(_row_copy(hp.at[s], r, xs_ref, dest_ref[k, r], sem.at[s]), k)
            return c
        lax.fori_loop(0, tm, body, 0)

    token_rows(slot, lambda cp, k: cp.start(priority=k % DMA_PRIORITIES))

    @pl.when(i > 0)
    def _():
        token_rows(1 - slot, lambda cp, k: cp.wait())

    @pl.when(i == n_steps - 1)
    def _():
        token_rows(slot, lambda cp, k: cp.wait())


def _dispatch(starts, cnt, dest, h2, n_rows):
    t, d = h2.shape
    tm = min(TM_DISP, t)
    return pl.pallas_call(
        _dispatch_kernel,
        grid_spec=pltpu.PrefetchScalarGridSpec(
            num_scalar_prefetch=2,
            grid=(t // tm,),
            in_specs=[pl.BlockSpec((EXPERT_TOP_K, tm), lambda i, *_: (0, i), memory_space=pltpu.SMEM),
                      pl.BlockSpec((tm, d), lambda i, *_: (i, 0))],
            out_specs=pl.BlockSpec(memory_space=pl.ANY),
            scratch_shapes=[pltpu.VMEM((2, tm) + _row_tiles(d // 2), jnp.uint32),
                            pltpu.VMEM((min(ROW_BLOCK, n_rows),) + _row_tiles(d // 2), jnp.uint32),
                            pltpu.SemaphoreType.DMA((2,)), pltpu.SemaphoreType.DMA(())]),
        out_shape=jax.ShapeDtypeStruct((n_rows,) + _row_tiles(d // 2), jnp.uint32),
        compiler_params=_cparams(("arbitrary",)),
        name="moe_dispatch",
    )(starts, cnt, dest, h2)


def _mlp_kernel(be_ref, nb_ref, x_ref, wg_ref, wu_ref, wd_ref, y_ref, wg_s, wu_s, wd_s, *, packed):
    i = pl.program_id(0)

    @pl.when((i == 0) | (be_ref[i] != be_ref[jnp.maximum(i - 1, 0)]))
    def _():
        wg_s[...] = wg_ref[...].astype(BF16)
        wu_s[...] = wu_ref[...].astype(BF16)
        wd_s[...] = wd_ref[...].astype(BF16)

    @pl.when(i < nb_ref[0])
    def _():
        if packed:
            xw = x_ref[...]
            xw = xw.reshape(xw.shape[0], xw.shape[1] * xw.shape[2])
            x = jnp.concatenate(_unpack_bf16_pairs(xw), axis=1).astype(BF16)
        else:
            x = x_ref[...].astype(BF16)
        gt = jnp.dot(x, wg_s[...], preferred_element_type=F32)
        up = jnp.dot(x, wu_s[...], preferred_element_type=F32)
        hb = ((gt * _sigmoid(gt)) * up).astype(BF16)
        y = jnp.dot(hb, wd_s[...], preferred_element_type=F32)
        y_ref[...] = _pack_bf16_pairs(y).reshape(y_ref.shape) if packed else y

    @pl.when(i >= nb_ref[0])
    def _():
        y_ref[...] = jnp.zeros_like(y_ref)


def _grouped_mlp(block_expert, n_used, xs, layer, w_gate, w_up, w_down, packed, name):
    n_rows, row_shape = xs.shape[0], xs.shape[1:]
    zeros = (0,) * len(row_shape)
    d, hid = w_gate.shape[2], w_gate.shape[3]
    rb = min(ROW_BLOCK, n_rows)
    w_spec = lambda shape: pl.BlockSpec((None, None) + shape, lambda i, be, nb: (layer, be[i], 0, 0))
    return pl.pallas_call(
        functools.partial(_mlp_kernel, packed=packed),
        grid_spec=pltpu.PrefetchScalarGridSpec(
            num_scalar_prefetch=2,
            grid=(n_rows // rb,),
            in_specs=[pl.BlockSpec((rb,) + row_shape, lambda i, be, nb: (jnp.minimum(i, nb[0] - 1),) + zeros),
                      w_spec((d, hid)), w_spec((d, hid)), w_spec((hid, d))],
            out_specs=pl.BlockSpec((rb,) + row_shape, lambda i, be, nb: (i,) + zeros),
            scratch_shapes=[pltpu.VMEM((d, hid), BF16), pltpu.VMEM((d, hid), BF16), pltpu.VMEM((hid, d), BF16)]),
        out_shape=jax.ShapeDtypeStruct(xs.shape, xs.dtype if packed else F32),
        compiler_params=_cparams(("arbitrary",)),
        name=name,
    )(block_expert, n_used, xs, w_gate, w_up, w_down)


def _combine_kernel(dest_ref, dnext_ref, wt_ref, ysh_ref, x_ref, g2_ref, fn_ref, ys_ref, o_ref, ybuf, sem, *,
                    final):
    i = pl.program_id(0)
    n_steps = pl.num_programs(0)
    slot = i % 2
    tm, d = x_ref.shape

    def token_rows(rows_ref, s, fn):
        def body(r, c):
            for k in range(EXPERT_TOP_K):
                fn(_row_copy(ys_ref, rows_ref[k, r], ybuf.at[s].at[k], r, sem.at[s]), k)
            return c
        lax.fori_loop(0, tm, body, 0)

    start = lambda cp, k: cp.start(priority=k % DMA_PRIORITIES)

    @pl.when(i == 0)
    def _():
        token_rows(dest_ref, 0, start)

    @pl.when(i + 1 < n_steps)
    def _():
        token_rows(dnext_ref, 1 - slot, start)

    token_rows(dest_ref, slot, lambda cp, k: cp.wait())

    wt = wt_ref[...]
    acc_hi = jnp.zeros((tm, d // 2), F32)
    acc_lo = jnp.zeros((tm, d // 2), F32)
    for k in range(EXPERT_TOP_K):
        hi, lo = _unpack_bf16_pairs(ybuf[slot, k].reshape(tm, d // 2))
        acc_hi = acc_hi + wt[:, k:k + 1] * hi
        acc_lo = acc_lo + wt[:, k:k + 1] * lo
    acc = ysh_ref[...] + jnp.concatenate([acc_hi, acc_lo], axis=1)
    xn = x_ref[...] + g2_ref[...] * acc
    if final:
        ms = jnp.mean(xn * xn, axis=-1, keepdims=True)
        xn = (xn * lax.rsqrt(ms + EPS)) * fn_ref[...]
    o_ref[...] = xn


def _combine(dest, wts_t, y_sorted, y_shared, x2d, modr, mod_base, seq, final_norm, final):
    t, d = x2d.shape
    tm = min(TM_COMB, seq)
    row_spec = pl.BlockSpec((tm, d), lambda i: (i, 0))
    n_steps = t // tm
    return pl.pallas_call(
        functools.partial(_combine_kernel, final=final),
        grid=(n_steps,),
        in_specs=[pl.BlockSpec((EXPERT_TOP_K, tm), lambda i: (0, i), memory_space=pltpu.SMEM),
                  pl.BlockSpec((EXPERT_TOP_K, tm), lambda i: (0, jnp.minimum(i + 1, n_steps - 1)),
                               memory_space=pltpu.SMEM),
                  pl.BlockSpec((tm, EXPERT_TOP_K), lambda i: (i, 0)),
                  row_spec, row_spec,
                  pl.BlockSpec((None, 1, d), lambda i: (mod_base + ((i * tm) // seq) * 6 + 5, 0, 0)),
                  pl.BlockSpec((1, d), lambda i: (0, 0)),
                  pl.BlockSpec(memory_space=pl.ANY)],
        out_specs=row_spec,
        out_shape=jax.ShapeDtypeStruct((t, d), F32),
        scratch_shapes=[pltpu.VMEM((2, EXPERT_TOP_K, tm) + _row_tiles(d // 2), jnp.uint32),
                        pltpu.SemaphoreType.DMA((2,))],
        compiler_params=_cparams(("arbitrary",)),
        name="moe_combine",
    )(dest, dest, wts_t, y_shared, x2d, modr, final_norm.reshape(1, d), y_sorted)


def _layout(d_model):
    r, a, kvw = RNN_WIDTH, N_HEADS * HEAD_DIM, N_KV_GROUPS * HEAD_DIM
    off = {}
    off["ma"], off["mb"] = 0, d_model
    off["u"] = 2 * d_model
    off["g"] = off["u"] + r
    off["kc"] = off["g"] + r
    off["vc"] = off["kc"] + kvw
    off["nf"] = off["vc"] + kvw
    off["q"] = 0
    off["ks"], off["vs"], off["kw"], off["vw"] = a, a + kvw, a + 2 * kvw, a + 3 * kvw
    off["gn"] = a + 4 * kvw
    off["nb"] = off["gn"] + TN_IN
    assert off["nf"] % TN_IN == 0 and off["gn"] % TN_IN == 0
    return off


def _pack_w_in(w_in_l, d_model):
    r, a, kvw = RNN_WIDTH, N_HEADS * HEAD_DIM, N_KV_GROUPS * HEAD_DIM
    off = _layout(d_model)
    s_u, s_g, s_q = 0, r, 2 * r
    s_kv = s_q + a
    s_gn = s_kv + 6 * kvw
    s_ma = s_gn + 3 * N_HEADS
    s_mb = s_ma + d_model
    cols = [w_in_l[:, s_ma:s_mb], w_in_l[:, s_mb:s_mb + d_model], w_in_l[:, s_u:s_g], w_in_l[:, s_g:s_q],
            w_in_l[:, s_kv:s_kv + 2 * kvw],
            w_in_l[:, s_q:s_kv], w_in_l[:, s_kv + 2 * kvw:s_gn], w_in_l[:, s_gn:s_ma]]
    w = jnp.concatenate(cols, axis=1)
    return jnp.pad(w, ((0, 0), (0, off["nf"] + off["nb"] - w.shape[1]))).astype(BF16)


def _moe_plan(idx, rank, counts, n_tok):
    cnt = counts[:, 0].astype(jnp.int32)
    padded = (cnt + ROW_BLOCK - 1) // ROW_BLOCK * ROW_BLOCK
    ends = jnp.cumsum(padded)
    starts = jnp.concatenate([jnp.zeros((1,), jnp.int32), ends]).astype(jnp.int32)
    onehot = idx[:, :, None] == jnp.arange(N_EXPERTS, dtype=jnp.int32)[None, None, :]
    dest = rank + jnp.sum(jnp.where(onehot, starts[None, None, :N_EXPERTS], 0), axis=2)
    n_blocks = (n_tok * EXPERT_TOP_K) // ROW_BLOCK + N_EXPERTS
    blk_start = jnp.arange(n_blocks, dtype=jnp.int32) * ROW_BLOCK
    owner = jnp.sum((ends[None, :] <= blk_start[:, None]).astype(jnp.int32), axis=1)
    block_expert = jnp.minimum(owner, N_EXPERTS - 1).astype(jnp.int32)
    n_used = (ends[-1] // ROW_BLOCK).astype(jnp.int32).reshape(1)
    return starts, cnt, dest.astype(jnp.int32), block_expert, n_used, n_blocks * ROW_BLOCK


def kernel(x, c, rel_bias, final_norm, ada_w, ada_b, norm_mix, norm_ffn, w_in, conv_w, conv_b, lru_wa, lru_ba, lru_wx, lru_bx, lru_lambda, cmp_pe_k, cmp_w1_k, cmp_w2_k, cmp_pe_v, cmp_w1_v, cmp_w2_v, w_up_rnn, w_up_att, w_out, router_w, router_bias, exp_w_gate, exp_w_up, exp_w_down, sh_w_gate, sh_w_up, sh_w_down):
    batch, seq, d = x.shape
    n_tok = batch * seq
    depth = ada_w.shape[0]
    g_, dh = N_KV_GROUPS, HEAD_DIM
    a_w, kvw, r = N_HEADS * HEAD_DIM, N_KV_GROUPS * HEAD_DIM, RNN_WIDTH
    off = _layout(d)
    assert seq % CMP_STRIDE == 0 and CMP_BLOCK == 2 * CMP_STRIDE

    mod = _adaln_mod(c, ada_w, ada_b)
    modr = mod.reshape(depth * batch * 6, 1, d)
    tables = _bias_tables(rel_bias, seq)
    nc = tables[2].shape[1]
    n_chunk = seq // CMP_STRIDE

    x2d = x.reshape(n_tok, d)
    for l in range(depth):
        mod_base = l * batch * 6
        zf, zb, zg = _inproj(x2d, norm_mix[l], modr, mod_base, seq, _pack_w_in(w_in[l], d), off["nf"])
        y_rnn = _rglru(zf, batch, seq, off["u"] // r, off["g"] // r, conv_w[l], conv_b[l],
                       lru_wa[l], lru_ba[l], lru_wx[l], lru_bx[l], lru_lambda[l])

        zb3 = zb.reshape(batch, seq, -1)
        zcol = lambda z3, name, w: z3[:, :, off[name]:off[name] + w]

        def unfold(v):
            ch = jnp.transpose(v.reshape(batch, n_chunk, CMP_STRIDE, g_, dh), (0, 3, 1, 2, 4))
            ch = ch.reshape(batch * g_, n_chunk, CMP_STRIDE * dh)
            blocks = jnp.concatenate([ch[:, :-1], ch[:, 1:]], axis=2)
            return jnp.pad(blocks, ((0, 0), (0, nc - (n_chunk - 1)), (0, 0)))

        zf3 = zf.reshape(batch, seq, -1)
        kc, vc = _compress(unfold(zcol(zf3, "kc", kvw)), unfold(zcol(zf3, "vc", kvw)),
                           cmp_pe_k[l], cmp_w1_k[l], cmp_w2_k[l], cmp_pe_v[l], cmp_w1_v[l], cmp_w2_v[l])
        kc = jnp.transpose(kc.reshape(batch, g_, nc, dh), (0, 2, 1, 3)).reshape(batch, nc, kvw)
        vcT = jnp.swapaxes(vc.reshape(batch, g_, nc, dh), 2, 3).reshape(batch, kvw, nc)
        qT = jnp.swapaxes(zcol(zb3, "q", a_w), 1, 2)
        def values_t(name):
            vt = jnp.swapaxes(zcol(zb3, name, kvw), 1, 2).reshape(batch, g_, dh, seq)
            ones = jnp.ones((batch, g_, ONES_ROWS, seq), BF16)
            return jnp.concatenate([vt, ones], axis=2).reshape(batch, g_ * (dh + ONES_ROWS), seq)

        vsT, vwT = values_t("vs"), values_t("vw")
        gT = jnp.swapaxes(zg.reshape(batch, seq, LANE)[:, :, :3 * N_HEADS], 1, 2)
        y_att = _attention(qT, kc, vcT, zb3, off["ks"] // kvw, off["kw"] // kvw, vsT, vwT, gT, tables,
                           batch, seq).reshape(n_tok, a_w)

        merged = _merge(y_rnn, y_att, w_up_rnn[l].astype(BF16), w_up_att[l].astype(BF16), zf,
                        off["ma"] // d, off["mb"] // d)
        x1, h2 = _outproj(merged, w_out[l].astype(BF16), x2d, modr, mod_base, seq, norm_ffn[l])

        idx, wts, rank, counts = _router(h2, router_w[l], router_bias[l])
        starts, cnt, dest, block_expert, n_used, n_rows = _moe_plan(idx, rank, counts, n_tok)
        xs = _dispatch(starts, cnt, dest, h2, n_rows)
        y_sorted = _grouped_mlp(block_expert, n_used, xs, l, exp_w_gate, exp_w_up, exp_w_down, True, "moe_experts")
        sh_blocks = n_tok // min(ROW_BLOCK, n_tok)
        y_shared = _grouped_mlp(jnp.zeros((sh_blocks,), jnp.int32), jnp.full((1,), sh_blocks, jnp.int32), h2, l,
                                sh_w_gate[:, None], sh_w_up[:, None], sh_w_down[:, None], False, "moe_shared")
        x2d = _combine(dest, wts.T, y_sorted, y_shared, x1, modr, mod_base, seq, final_norm, l == depth - 1)
    return x2d.reshape(batch, seq, d)
```

```python
import functools
import math

import numpy as np
import jax
import jax.numpy as jnp
from jax import lax
from jax.experimental import pallas as pl
from jax.experimental.pallas import tpu as pltpu

DEPTH = 2
RNN_WIDTH = 1024
RNN_BLOCKS = 8
CONV_WIDTH = 4
LRU_C = 8.0
N_HEADS = 16
N_KV_GROUPS = 4
HEAD_DIM = 64
CMP_BLOCK = 32
CMP_STRIDE = 16
CMP_HIDDEN = 128
SEL_BLOCK = 64
SEL_TOP_N = 8
WINDOW = 512
N_BUCKETS = 32
MAX_DISTANCE = 128
N_EXPERTS = 64
EXPERT_TOP_K = 8
N_EXPERT_GROUPS = 8
TOPK_EXPERT_GROUPS = 4
ROUTED_SCALE = 2.5
EPS = 1e-6
NEG = -1e30

LANE = 128
SUBLANE = 8
VMEM_LIMIT = 52 * 1024 * 1024
DMA_PRIORITIES = 2

TM_IN = 1024
TN_IN = 512
TN_MOD = 1024
TC_RNN = 256
TQ = 128
FAR_TILES = 4
ONES_ROWS = 16
TM_MERGE = 512
TN_MERGE = 2048
TM_OUT = 512
TM_ROUTE = 256
TM_DISP = 128
ROW_BLOCK = 256
TM_COMB = 128

F32 = jnp.float32
BF16 = jnp.bfloat16


def _cparams(sem):
    return pltpu.CompilerParams(dimension_semantics=sem, vmem_limit_bytes=VMEM_LIMIT)


def _round_up(a, b):
    return (a + b - 1) // b * b


def _tile(n, pref):
    if n <= pref:
        return n
    t = pref // LANE * LANE
    while n % t:
        t -= LANE
    return t


def _gelu_tanh(x):
    return x * (0.5 * (1.0 + jnp.tanh(math.sqrt(2.0 / math.pi) * (x + 0.044715 * (x * x * x)))))


def _sigmoid(x):
    return jax.nn.sigmoid(x)


def _mod_kernel(c_ref, w_ref, b_ref, o_ref):
    c = c_ref[...]
    ca = (c * _sigmoid(c)).astype(BF16)
    o_ref[...] = jnp.dot(ca, w_ref[...].astype(BF16), preferred_element_type=F32) + b_ref[...]


def _adaln_mod(c, ada_w, ada_b):
    nl, d, n6 = ada_w.shape
    b = c.shape[0]
    tn = _tile(n6, TN_MOD)
    return pl.pallas_call(
        _mod_kernel,
        grid=(nl, n6 // tn),
        in_specs=[pl.BlockSpec((b, d), lambda l, j: (0, 0)),
                  pl.BlockSpec((None, d, tn), lambda l, j: (l, 0, j)),
                  pl.BlockSpec((None, 1, tn), lambda l, j: (l, 0, j))],
        out_specs=pl.BlockSpec((None, b, tn), lambda l, j: (l, 0, j)),
        out_shape=jax.ShapeDtypeStruct((nl, b, n6), F32),
        compiler_params=_cparams(("arbitrary", "arbitrary")),
        name="adaln_mod",
    )(c, ada_w, ada_b.reshape(nl, 1, n6))


def _inproj_kernel(x_ref, nw_ref, sc_ref, sh_ref, w_ref, zf_ref, zb_ref, zg_ref, h_ref, *, nf, nj):
    j = pl.program_id(1)

    @pl.when(j == 0)
    def _():
        x = x_ref[...]
        ms = jnp.mean(x * x, axis=-1, keepdims=True)
        y = (x * lax.rsqrt(ms + EPS)) * nw_ref[...]
        h_ref[...] = (y * (1.0 + sc_ref[...]) + sh_ref[...]).astype(BF16)

    res = jnp.dot(h_ref[...], w_ref[...], preferred_element_type=F32)

    @pl.when(j < nf)
    def _():
        zf_ref[...] = res

    @pl.when(j >= nf)
    def _():
        zb_ref[...] = res.astype(BF16)

    @pl.when(j == nj - 1)
    def _():
        zg_ref[...] = res[:, 0:LANE]


def _inproj(x2d, norm_w, modr, mod_base, seq, w_p, n_f32):
    t, d = x2d.shape
    n_p = w_p.shape[1]
    tm = min(TM_IN, seq)
    tn = TN_IN
    assert n_f32 % tn == 0 and n_p % tn == 0
    nf, nj = n_f32 // tn, n_p // tn

    def mod_idx(k):
        return lambda i, j: (mod_base + ((i * tm) // seq) * 6 + k, 0, 0)

    return pl.pallas_call(
        functools.partial(_inproj_kernel, nf=nf, nj=nj),
        grid=(t // tm, nj),
        in_specs=[pl.BlockSpec((tm, d), lambda i, j: (i, 0)),
                  pl.BlockSpec((1, d), lambda i, j: (0, 0)),
                  pl.BlockSpec((None, 1, d), mod_idx(1)),
                  pl.BlockSpec((None, 1, d), mod_idx(0)),
                  pl.BlockSpec((d, tn), lambda i, j: (0, j))],
        out_specs=[pl.BlockSpec((tm, tn), lambda i, j: (i, jnp.minimum(j, nf - 1))),
                   pl.BlockSpec((tm, tn), lambda i, j: (i, jnp.maximum(j - nf, 0))),
                   pl.BlockSpec((tm, LANE), lambda i, j: (i, 0))],
        out_shape=[jax.ShapeDtypeStruct((t, n_f32), F32), jax.ShapeDtypeStruct((t, n_p - n_f32), BF16),
                   jax.ShapeDtypeStruct((t, LANE), F32)],
        scratch_shapes=[pltpu.VMEM((tm, d), BF16)],
        compiler_params=_cparams(("arbitrary", "arbitrary")),
        name="inproj",
    )(x2d, norm_w.reshape(1, d), modr, modr, w_p)


def _rglru_kernel(u_ref, g_ref, cw_ref, cb_ref, wa_ref, ba_ref, wx_ref, bx_ref, lam_ref, y_ref,
                  ubuf, a_s, b_s, h_s, hcar):
    tc, r = u_ref.shape
    nb = wa_ref.shape[0]
    bw = r // nb

    @pl.when(pl.program_id(1) == 0)
    def _():
        ubuf[0:SUBLANE, :] = jnp.zeros((SUBLANE, r), F32)
        hcar[...] = jnp.zeros_like(hcar)

    ubuf[SUBLANE:SUBLANE + tc, :] = u_ref[...]
    cw = cw_ref[...]
    uc = cb_ref[...] + cw[CONV_WIDTH - 1:CONV_WIDTH, :] * ubuf[SUBLANE:SUBLANE + tc, :]
    for k in range(CONV_WIDTH - 1):
        off = SUBLANE - (CONV_WIDTH - 1) + k
        uc = uc + cw[k:k + 1, :] * ubuf[off:off + tc, :]
    ubuf[0:SUBLANE, :] = ubuf[tc:tc + SUBLANE, :]

    ucb = uc.astype(BF16)
    rp = []
    xp = []
    for n in range(nb):
        blk = ucb[:, n * bw:(n + 1) * bw]
        rp.append(jnp.dot(blk, wa_ref[n], preferred_element_type=F32))
        xp.append(jnp.dot(blk, wx_ref[n], preferred_element_type=F32))
    rg = _sigmoid(jnp.concatenate(rp, axis=1) + ba_ref[...])
    ig = _sigmoid(jnp.concatenate(xp, axis=1) + bx_ref[...])
    nl = -lam_ref[...]
    sp = jnp.maximum(nl, 0.0) + jnp.log1p(jnp.exp(-jnp.abs(nl)))
    log_a = (-LRU_C * rg) * sp
    a_s[...] = jnp.exp(log_a)
    th = jnp.tanh(log_a)
    one_minus_a2 = (-2.0 * th) / (1.0 - th)
    b_s[...] = jnp.sqrt(one_minus_a2) * (ig * uc)

    def step(t, h):
        h = a_s[pl.ds(t, 1), :] * h + b_s[pl.ds(t, 1), :]
        h_s[pl.ds(t, 1), :] = h
        return h

    h_last = lax.fori_loop(0, tc, step, hcar[0:1, :], unroll=8)
    hcar[0:1, :] = h_last
    y_ref[...] = (_gelu_tanh(g_ref[...]) * h_s[...]).astype(y_ref.dtype)


def _rglru(z2d, batch, seq, u_blk, g_blk, conv_w, conv_b, wa, ba, wx, bx, lam):
    r = conv_w.shape[1]
    tc = min(TC_RNN, seq)
    nt = seq // tc
    nb, bw, _ = wa.shape
    row = lambda v: v.reshape(1, r)
    full = lambda shape: pl.BlockSpec(shape, lambda b, t: (0,) * len(shape))
    return pl.pallas_call(
        _rglru_kernel,
        grid=(batch, nt),
        in_specs=[pl.BlockSpec((tc, r), lambda b, t: (b * nt + t, u_blk)),
                  pl.BlockSpec((tc, r), lambda b, t: (b * nt + t, g_blk)),
                  full((CONV_WIDTH, r)), full((1, r)),
                  full((nb, bw, bw)), full((1, r)),
                  full((nb, bw, bw)), full((1, r)), full((1, r))],
        out_specs=pl.BlockSpec((tc, r), lambda b, t: (b * nt + t, 0)),
        out_shape=jax.ShapeDtypeStruct((batch * seq, r), BF16),
        scratch_shapes=[pltpu.VMEM((tc + SUBLANE, r), F32), pltpu.VMEM((tc, r), F32),
                        pltpu.VMEM((tc, r), F32), pltpu.VMEM((tc, r), F32),
                        pltpu.VMEM((SUBLANE, r), F32)],
        compiler_params=_cparams(("arbitrary", "arbitrary")),
        name="rglru",
    )(z2d, z2d, conv_w, row(conv_b), wa.astype(BF16), row(ba), wx.astype(BF16), row(bx), row(lam))


def _compress_kernel(xk_ref, xv_ref, pek_ref, w1k_ref, w2k_ref, pev_ref, w1v_ref, w2v_ref, kc_ref, vc_ref):
    def one(x_ref, pe_ref, w1_ref, w2_ref, o_ref):
        blocks = (x_ref[...] + pe_ref[...]).astype(BF16)
        hid = _gelu_tanh(jnp.dot(blocks, w1_ref[...], preferred_element_type=F32))
        o_ref[...] = jnp.dot(hid.astype(BF16), w2_ref[...], preferred_element_type=F32).astype(o_ref.dtype)

    one(xk_ref, pek_ref, w1k_ref, w2k_ref, kc_ref)
    one(xv_ref, pev_ref, w1v_ref, w2v_ref, vc_ref)


def _compress(xk, xv, pe_k, w1_k, w2_k, pe_v, w1_v, w2_v):
    bg, nc, kd = xk.shape
    dh = w2_k.shape[1]
    hid = w2_k.shape[0]
    x_spec = pl.BlockSpec((None, nc, kd), lambda i: (i, 0, 0))
    full = lambda shape: pl.BlockSpec(shape, lambda i: (0,) * len(shape))
    o_spec = pl.BlockSpec((None, nc, dh), lambda i: (i, 0, 0))
    prep = lambda pe, w1, w2: (pe.reshape(1, kd), w1.reshape(kd, hid).astype(BF16), w2.astype(BF16))
    return pl.pallas_call(
        _compress_kernel,
        grid=(bg,),
        in_specs=[x_spec, x_spec, full((1, kd)), full((kd, hid)), full((hid, dh)),
                  full((1, kd)), full((kd, hid)), full((hid, dh))],
        out_specs=[o_spec, o_spec],
        out_shape=[jax.ShapeDtypeStruct((bg, nc, dh), BF16)] * 2,
        compiler_params=_cparams(("arbitrary",)),
        name="nsa_compress",
    )(xk, xv, *prep(pe_k, w1_k, w2_k), *prep(pe_v, w1_v, w2_v))


def _col_max(x):
    return jnp.max(x, axis=0, keepdims=True)


def _attn_kernel(qT_ref, kc_ref, vcT_ref, ks_ref, kw_ref, vsT_ref, vwT_ref, gT_ref,
                 tnear_ref, cfar_ref, cbias_ref, ovl_ref, o_ref,
                 qbd, sel_s, m_s, acc_s):
    i = pl.program_id(1)
    g_, hg, dh = N_KV_GROUPS, N_HEADS // N_KV_GROUPS, HEAD_DIM
    tq = qT_ref.shape[1]
    gw = hg * tq
    nw = g_ * gw
    n_sel = sel_s.shape[0]
    per = tq // SEL_BLOCK

    qT = (qT_ref[...].astype(F32) * (HEAD_DIM ** -0.5)).astype(BF16)
    zero_blk = jnp.zeros((dh, gw), BF16)
    rows = []
    for g in range(g_):
        qcat = jnp.concatenate([qT[(g * hg + h) * dh:(g * hg + h + 1) * dh, :] for h in range(hg)], axis=1)
        rows.append(jnp.concatenate([zero_blk] * g + [qcat] + [zero_blk] * (g_ - 1 - g), axis=1))
    qbd[...] = jnp.concatenate(rows, axis=0)

    def lane_q(shape):
        return lax.broadcasted_iota(jnp.int32, shape, 1) % tq

    def pv(vT_ref, start, nrows, p):
        pb = p.astype(BF16)
        outs = []
        for g in range(g_):
            if start is None:
                v = vT_ref[g * dh:(g + 1) * dh, :]
            else:
                v = jnp.concatenate([vT_ref[g * dh:(g + 1) * dh, pl.ds(start, nrows)],
                                     jnp.ones((ONES_ROWS, nrows), BF16)], axis=0)
            outs.append(jnp.dot(v, pb[:, g * gw:(g + 1) * gw], preferred_element_type=F32))
        return jnp.concatenate(outs, axis=1)

    nc = kc_ref.shape[0]
    near_t = jnp.maximum(i - 1, 0)
    near_start = pl.multiple_of(near_t * tq, tq)
    far_end = near_t
    wr = WINDOW - tq
    win_start = pl.multiple_of(jnp.maximum(i - WINDOW // tq, 0) * tq, tq)
    lhs = jnp.concatenate([kc_ref[...], ks_ref[pl.ds(near_start, 2 * tq), :],
                           kw_ref[pl.ds(near_start, 2 * tq), :], kw_ref[pl.ds(win_start, wr), :]], axis=0)
    s_all = jnp.dot(lhs, qbd[...], preferred_element_type=F32)
    r_slc, r_win, r_far = nc, nc + 2 * tq, nc + 4 * tq

    sc = s_all[0:nc] + cbias_ref[...]
    nrow = lax.broadcasted_iota(jnp.int32, (nc, nw), 0)
    cmask = (nrow * CMP_STRIDE + (CMP_BLOCK - 1)) <= i * tq + lane_q((nc, nw))
    sc = jnp.where(cmask, sc, NEG)
    pc = jnp.where(cmask, jnp.exp(sc - _col_max(sc)), 0.0)
    lc = jnp.sum(pc, axis=0, keepdims=True)
    pc = pc * jnp.where(lc > 0.0, 1.0 / lc, 0.0)
    o_cmp = pv(vcT_ref, None, nc, pc)

    psum = jnp.concatenate(
        [sum(pc[:, g * gw + h * tq:g * gw + (h + 1) * tq] for h in range(hg)) for g in range(g_)], axis=1)
    imp = jnp.dot(ovl_ref[...], psum, preferred_element_type=F32, precision=lax.Precision.HIGHEST)
    jrow = lax.broadcasted_iota(jnp.int32, (n_sel, g_ * tq), 0)
    tq_abs = i * tq + lane_q((n_sel, g_ * tq))
    cur = tq_abs // SEL_BLOCK
    forced = (jrow == 0) | (jrow == cur) | (jrow == cur - 1)
    valid = jrow * SEL_BLOCK <= tq_abs
    work = jnp.where(forced, jnp.inf, jnp.where(valid, imp, -jnp.inf))
    jrow_f = jrow.astype(F32)
    sel = jnp.zeros((n_sel, g_ * tq), F32)
    for _ in range(min(SEL_TOP_N, n_sel)):
        mx = _col_max(work)
        first = jnp.min(jnp.where(work == mx, jrow_f, float(n_sel)), axis=0, keepdims=True)
        pick = jrow_f == first
        sel = jnp.where(pick, 1.0, sel)
        work = jnp.where(pick, -jnp.inf, work)
    sel_s[...] = jnp.where(sel > 0.5, 0.0, NEG)

    def sel_add(first_blk, n_blk, limit_blk=None, bias_row=None):
        parts = []
        for c in range(n_blk):
            rowv = sel_s[pl.ds(first_blk + c, 1), :]
            if limit_blk is not None:
                rowv = rowv + jnp.where(first_blk + c < limit_blk, 0.0, NEG)
            rowv = jnp.concatenate([rowv[:, g * tq:(g + 1) * tq] for g in range(g_) for _ in range(hg)], axis=1)
            if bias_row is not None:
                rowv = rowv + bias_row
            parts.append(jnp.broadcast_to(rowv, (SEL_BLOCK, nw)))
        return jnp.concatenate(parts, axis=0)

    def scores(k_ref, start, nrows):
        return jnp.dot(k_ref[pl.ds(start, nrows), :], qbd[...], preferred_element_type=F32)

    def flash_init(s, vT_ref, start, nrows):
        m = _col_max(s)
        m_s[...] = m
        acc_s[...] = pv(vT_ref, start, nrows, jnp.exp(s - m))

    def flash_update(s, vT_ref, start, nrows):
        m_old = m_s[...]
        m_new = jnp.maximum(m_old, _col_max(s))
        m_s[...] = m_new
        acc_s[...] = jnp.exp(m_old - m_new) * acc_s[...] + pv(vT_ref, start, nrows, jnp.exp(s - m_new))

    def flash_result():
        acc = acc_s[...]
        return acc[0:dh] * (1.0 / acc[dh:dh + 1])

    bias_off = pl.multiple_of(jnp.where(i == 0, tq, 0), tq)
    near_add = tnear_ref[pl.ds(bias_off, 2 * tq), :]

    flash_init(s_all[r_slc:r_win] + near_add + sel_add(near_t * per, 2 * per), vsT_ref, near_start, 2 * tq)
    fr = FAR_TILES * tq

    def far_slc(c, carry):
        hi_t = far_end - c * FAR_TILES
        st_t = jnp.maximum(hi_t - FAR_TILES, 0)
        start = pl.multiple_of(st_t * tq, tq)
        add = sel_add(st_t * per, FAR_TILES * per, hi_t * per, cfar_ref[...])
        flash_update(scores(ks_ref, start, fr) + add, vsT_ref, start, fr)
        return carry

    lax.fori_loop(0, (far_end + FAR_TILES - 1) // FAR_TILES, far_slc, 0)
    o_slc = flash_result()

    flash_init(s_all[r_win:r_far] + near_add, vwT_ref, near_start, 2 * tq)
    nwin = WINDOW // tq
    parts = []
    for c in range(wr // tq):
        tile_ok = win_start // tq + c < far_end
        part = s_all[r_far + c * tq:r_far + (c + 1) * tq] + (cfar_ref[...] + jnp.where(tile_ok, 0.0, NEG))
        if c == 0:
            krow = lax.broadcasted_iota(jnp.int32, (tq, nw), 0)
            part = part + jnp.where(krow > lane_q((tq, nw)) - jnp.where(i < nwin, tq, 0), 0.0, NEG)
        parts.append(part)
    flash_update(jnp.concatenate(parts, axis=0), vwT_ref, win_start, wr)
    o_win = flash_result()

    gate = _sigmoid(gT_ref[...])

    def grow(j):
        return jnp.concatenate([gate[h * 3 + j:h * 3 + j + 1, :] for h in range(N_HEADS)], axis=1)

    o_t = grow(0) * o_cmp + grow(1) * o_slc + grow(2) * o_win
    o_hd = jnp.concatenate([o_t[:, h * tq:(h + 1) * tq] for h in range(N_HEADS)], axis=0)
    o_ref[...] = o_hd.T.astype(o_ref.dtype)


def _t5_bucket_np(dist):
    n = np.maximum(dist, 0)
    max_exact = N_BUCKETS // 2
    nf = np.maximum(n, 1).astype(np.float64)
    large = max_exact + (np.log(nf / max_exact) / math.log(MAX_DISTANCE / max_exact)
                         * (N_BUCKETS - max_exact)).astype(np.int64)
    large = np.minimum(large, N_BUCKETS - 1)
    return np.where(n < max_exact, n, large).astype(np.int32)


def _bias_tables(rel_bias, seq):
    tq = min(TQ, seq)
    assert MAX_DISTANCE <= tq, "tiles two or more behind the diagonal must all fall in the last bucket"
    nh = N_HEADS
    nc = _round_up((seq - CMP_BLOCK) // CMP_STRIDE + 1, LANE)
    off_max = (nc - 1) * CMP_STRIDE + CMP_BLOCK - 1
    fd = rel_bias.astype(F32)[_t5_bucket_np(np.arange(-off_max, seq))]
    fdT = fd.T
    kk = np.arange(tq)[:, None]
    qq = np.arange(tq)[None, :]

    def toeplitz(offset):
        idx = off_max + np.maximum(offset + qq - kk, 0)
        return jnp.transpose(fd[idx], (0, 2, 1)).reshape(tq, nh * tq)

    future = jnp.asarray(np.tile(np.where(kk <= qq, 0.0, NEG).astype(np.float32), (1, nh)))
    tnear = jnp.concatenate([toeplitz(tq), toeplitz(0) + future, jnp.full((tq, nh * tq), NEG, F32)], axis=0)
    cfar = jnp.broadcast_to(rel_bias.astype(F32)[N_BUCKETS - 1][:, None], (nh, tq)).reshape(1, nh * tq)
    rows = [lax.slice_in_dim(fdT, off_max - (n * CMP_STRIDE + CMP_BLOCK - 1),
                             off_max - (n * CMP_STRIDE + CMP_BLOCK - 1) + seq, axis=1) for n in range(nc)]
    cb = jnp.stack(rows, axis=0).reshape(nc, nh, seq // tq, tq)
    cbias = jnp.transpose(cb, (2, 0, 1, 3)).reshape(seq // tq, nc, nh * tq)
    n_sel = seq // SEL_BLOCK
    cmp_start = np.arange(nc) * CMP_STRIDE
    sel_start = np.arange(n_sel) * SEL_BLOCK
    ovl = ((cmp_start[None, :] < sel_start[:, None] + SEL_BLOCK)
           & (cmp_start[None, :] + CMP_BLOCK > sel_start[:, None])).astype(np.float32)
    return tnear, cfar, cbias, jnp.asarray(ovl)


def _attention(qT, kc, vcT, zb3, ks_blk, kw_blk, vsT, vwT, gT, tables, batch, seq):
    tnear, cfar, cbias, ovl = tables
    g_, dh = N_KV_GROUPS, HEAD_DIM
    tq = min(TQ, seq)
    assert seq % tq == 0 and seq >= FAR_TILES * tq and WINDOW % tq == 0 and tq % SEL_BLOCK == 0
    nqt = seq // tq
    nw = N_HEADS * tq
    nc = kc.shape[1]
    n_sel = seq // SEL_BLOCK
    kvw = g_ * dh
    per_b = lambda shape: pl.BlockSpec((None,) + shape, lambda b, i: (b,) + (0,) * len(shape))
    full = lambda shape: pl.BlockSpec(shape, lambda b, i: (0,) * len(shape))
    return pl.pallas_call(
        _attn_kernel,
        grid=(batch, nqt),
        in_specs=[pl.BlockSpec((None, N_HEADS * dh, tq), lambda b, i: (b, 0, i)),
                  per_b((nc, kvw)),
                  per_b((kvw, nc)),
                  pl.BlockSpec((None, seq, kvw), lambda b, i: (b, 0, ks_blk)),
                  pl.BlockSpec((None, seq, kvw), lambda b, i: (b, 0, kw_blk)),
                  per_b((kvw, seq)), per_b((kvw, seq)),
                  pl.BlockSpec((None, 3 * N_HEADS, tq), lambda b, i: (b, 0, i)),
                  full((3 * tq, nw)), full((1, nw)),
                  pl.BlockSpec((None, nc, nw), lambda b, i: (i, 0, 0)),
                  full((n_sel, nc))],
        out_specs=pl.BlockSpec((None, tq, N_HEADS * dh), lambda b, i: (b, i, 0)),
        out_shape=jax.ShapeDtypeStruct((batch, seq, N_HEADS * dh), BF16),
        scratch_shapes=[pltpu.VMEM((kvw, nw), BF16), pltpu.VMEM((n_sel, g_ * tq), F32),
                        pltpu.VMEM((1, nw), F32), pltpu.VMEM((dh + ONES_ROWS, nw), F32)],
        compiler_params=_cparams(("arbitrary", "arbitrary")),
        name="nsa_attention",
    )(qT, kc, vcT, zb3, zb3, vsT, vwT, gT, tnear, cfar, cbias, ovl)


def _merge_kernel(yr_ref, ya_ref, wr_ref, wa_ref, ma_ref, mb_ref, o_ref):
    pr = jnp.dot(yr_ref[...], wr_ref[...], preferred_element_type=F32)
    pa = jnp.dot(ya_ref[...], wa_ref[...], preferred_element_type=F32)
    o_ref[...] = (_sigmoid(ma_ref[...]) * pr + _sigmoid(mb_ref[...]) * pa).astype(o_ref.dtype)


def _merge(y_rnn, y_att, w_ur, w_ua, z2d, ma_blk, mb_blk):
    t, r = y_rnn.shape
    a = y_att.shape[1]
    d = w_ur.shape[1]
    tm = min(TM_MERGE, t)
    tn = min(TN_MERGE, d)
    nj = d // tn
    return pl.pallas_call(
        _merge_kernel,
        grid=(t // tm, nj),
        in_specs=[pl.BlockSpec((tm, r), lambda i, j: (i, 0)),
                  pl.BlockSpec((tm, a), lambda i, j: (i, 0)),
                  pl.BlockSpec((r, tn), lambda i, j: (0, j)),
                  pl.BlockSpec((a, tn), lambda i, j: (0, j)),
                  pl.BlockSpec((tm, tn), lambda i, j: (i, ma_blk * nj + j)),
                  pl.BlockSpec((tm, tn), lambda i, j: (i, mb_blk * nj + j))],
        out_specs=pl.BlockSpec((tm, tn), lambda i, j: (i, j)),
        out_shape=jax.ShapeDtypeStruct((t, d), BF16),
        compiler_params=_cparams(("arbitrary", "arbitrary")),
        name="merge",
    )(y_rnn, y_att, w_ur, w_ua, z2d, z2d)


def _outproj_kernel(mg_ref, w_ref, x_ref, g1_ref, nw_ref, sc_ref, sh_ref, x1_ref, h2_ref):
    x1 = x_ref[...] + g1_ref[...] * jnp.dot(mg_ref[...], w_ref[...], preferred_element_type=F32)
    x1_ref[...] = x1
    ms = jnp.mean(x1 * x1, axis=-1, keepdims=True)
    y = (x1 * lax.rsqrt(ms + EPS)) * nw_ref[...]
    h2_ref[...] = (y * (1.0 + sc_ref[...]) + sh_ref[...]).astype(h2_ref.dtype)


def _outproj(merged, w_out, x2d, modr, mod_base, seq, norm_w):
    t, d = x2d.shape
    tm = min(TM_OUT, seq)

    def mod_idx(k):
        return lambda i: (mod_base + ((i * tm) // seq) * 6 + k, 0, 0)

    row_spec = pl.BlockSpec((tm, d), lambda i: (i, 0))
    return pl.pallas_call(
        _outproj_kernel,
        grid=(t // tm,),
        in_specs=[row_spec, pl.BlockSpec((d, d), lambda i: (0, 0)), row_spec,
                  pl.BlockSpec((None, 1, d), mod_idx(2)),
                  pl.BlockSpec((1, d), lambda i: (0, 0)),
                  pl.BlockSpec((None, 1, d), mod_idx(4)),
                  pl.BlockSpec((None, 1, d), mod_idx(3))],
        out_specs=[row_spec, row_spec],
        out_shape=[jax.ShapeDtypeStruct((t, d), F32), jax.ShapeDtypeStruct((t, d), BF16)],
        compiler_params=_cparams(("arbitrary",)),
        name="outproj",
    )(merged, w_out, x2d, modr, norm_w.reshape(1, d), modr, modr)


def _router_kernel(h_ref, rw_ref, rb_ref, idx_ref, wt_ref, rank_ref, cnt_ref, carry):
    ne = rw_ref.shape[0]
    tm = h_ref.shape[0]
    per = ne // N_EXPERT_GROUPS
    assert per == SUBLANE, "one expert group per sublane tile"

    @pl.when(pl.program_id(0) == 0)
    def _():
        carry[...] = jnp.zeros_like(carry)

    logits = lax.dot_general(rw_ref[...], h_ref[...].astype(BF16), (((1,), (1,)), ((), ())),
                             preferred_element_type=F32)
    scores = _sigmoid(logits)
    biased = scores + rb_ref[...]
    erow = lax.broadcasted_iota(jnp.int32, (ne, tm), 0).astype(F32)
    grow = lax.broadcasted_iota(jnp.int32, (ne, tm), 0) // per

    gparts = []
    sub = lax.broadcasted_iota(jnp.int32, (per, tm), 0).astype(F32)
    for gi in range(N_EXPERT_GROUPS):
        xg = biased[gi * per:(gi + 1) * per, :]
        m1 = _col_max(xg)
        f1 = jnp.min(jnp.where(xg == m1, sub, float(per)), axis=0, keepdims=True)
        m2 = _col_max(jnp.where(sub == f1, -jnp.inf, xg))
        gparts.append(jnp.broadcast_to(m1 + m2, (per, tm)))
    gscore = jnp.concatenate(gparts, axis=0)

    kparts = []
    for gi in range(N_EXPERT_GROUPS):
        gs = gscore[gi * per:gi * per + 1, :]
        beats = (gscore > gs) | ((gscore == gs) & (grow < gi))
        nbeat = jnp.sum(beats.astype(F32), axis=0, keepdims=True)
        kparts.append(jnp.broadcast_to(nbeat < float(TOPK_EXPERT_GROUPS * per), (per, tm)))
    gkeep = jnp.concatenate(kparts, axis=0)

    work = jnp.where(gkeep, biased, -jnp.inf)
    picks = []
    chosen = jnp.zeros((ne, tm), F32)
    for _ in range(EXPERT_TOP_K):
        mx = _col_max(work)
        first = jnp.min(jnp.where(work == mx, erow, float(ne)), axis=0, keepdims=True)
        pick = erow == first
        picks.append(pick)
        chosen = jnp.where(pick, 1.0, chosen)
        work = jnp.where(pick, -jnp.inf, work)

    tri = (lax.broadcasted_iota(jnp.int32, (tm, tm), 0) < lax.broadcasted_iota(jnp.int32, (tm, tm), 1))
    before = jnp.dot(chosen.astype(BF16), tri.astype(BF16), preferred_element_type=F32)
    pos = before + carry[:, 0:1]
    new_carry = carry[:, 0:1] + jnp.sum(chosen, axis=1, keepdims=True)
    carry[...] = jnp.broadcast_to(new_carry, carry.shape)
    cnt_ref[...] = carry[...]

    krow = lax.broadcasted_iota(jnp.int32, (EXPERT_TOP_K, tm), 0)
    idx_o = jnp.zeros((EXPERT_TOP_K, tm), F32)
    wt_o = jnp.zeros((EXPERT_TOP_K, tm), F32)
    rk_o = jnp.zeros((EXPERT_TOP_K, tm), F32)
    for k, pick in enumerate(picks):
        sel = lambda v: jnp.sum(jnp.where(pick, v, 0.0), axis=0, keepdims=True)
        idx_o = jnp.where(krow == k, sel(erow), idx_o)
        wt_o = jnp.where(krow == k, sel(scores), wt_o)
        rk_o = jnp.where(krow == k, sel(pos), rk_o)
    wsum = jnp.sum(wt_o, axis=0, keepdims=True)
    idx_ref[...] = idx_o.astype(jnp.int32)
    wt_ref[...] = (ROUTED_SCALE * wt_o) / wsum
    rank_ref[...] = rk_o.astype(jnp.int32)


def _router(h2, router_w, router_bias):
    t, d = h2.shape
    ne = router_w.shape[1]
    tm = min(TM_ROUTE, t)
    k_spec = pl.BlockSpec((EXPERT_TOP_K, tm), lambda i: (0, i))
    return pl.pallas_call(
        _router_kernel,
        grid=(t // tm,),
        in_specs=[pl.BlockSpec((tm, d), lambda i: (i, 0)),
                  pl.BlockSpec((ne, d), lambda i: (0, 0)),
                  pl.BlockSpec((ne, 1), lambda i: (0, 0))],
        out_specs=[k_spec, k_spec, k_spec, pl.BlockSpec((ne, LANE), lambda i: (0, 0))],
        out_shape=[jax.ShapeDtypeStruct((EXPERT_TOP_K, t), jnp.int32),
                   jax.ShapeDtypeStruct((EXPERT_TOP_K, t), F32),
                   jax.ShapeDtypeStruct((EXPERT_TOP_K, t), jnp.int32),
                   jax.ShapeDtypeStruct((ne, LANE), F32)],
        scratch_shapes=[pltpu.VMEM((ne, LANE), F32)],
        compiler_params=_cparams(("arbitrary",)),
        name="moe_router",
    )(h2, router_w.T.astype(BF16), router_bias.reshape(ne, 1).astype(F32))


def _row_copy(src_ref, src_row, dst_ref, dst_row, sem):
    return pltpu.make_async_copy(src_ref.at[pl.ds(src_row, 1)], dst_ref.at[pl.ds(dst_row, 1)], sem)


def _pack_bf16_pairs(x):
    half = x.shape[1] // 2
    bits = lax.bitcast_convert_type(x.astype(BF16).astype(F32), jnp.uint32)
    return bits[:, :half] | (bits[:, half:] >> 16)


def _unpack_bf16_pairs(w):
    hi = lax.bitcast_convert_type(w & jnp.uint32(0xFFFF0000), F32)
    lo = lax.bitcast_convert_type(w << 16, F32)
    return hi, lo


def _row_tiles(words):
    return (words // LANE, LANE)


def _dispatch_kernel(start_ref, cnt_ref, dest_ref, h_ref, xs_ref, hp, zblk, sem, zsem):
    i = pl.program_id(0)
    n_steps = pl.num_programs(0)
    slot = i % 2
    tm = h_ref.shape[0]
    ne = start_ref.shape[0] - 1
    rb = zblk.shape[0]
    n_blocks = xs_ref.shape[0] // rb
    hp[slot] = _pack_bf16_pairs(h_ref[...]).reshape(hp.shape[1:])

    def pad_rows(e, fn):
        lo = start_ref[e] + cnt_ref[e]
        lax.fori_loop(lo, start_ref[e + 1], lambda r, c: fn(_row_copy(zblk, 0, xs_ref, r, zsem)) or c, 0)

    def tail_blocks(fn):
        def body(b, c):
            fn(pltpu.make_async_copy(zblk, xs_ref.at[pl.ds(pl.multiple_of(b * rb, rb), rb)], zsem))
            return c
        lax.fori_loop(start_ref[ne] // rb, n_blocks, body, 0)

    @pl.when(i == 0)
    def _():
        zblk[...] = jnp.zeros_like(zblk)
        lax.fori_loop(0, ne, lambda e, c: pad_rows(e, lambda cp: cp.start()) or c, 0)
        tail_blocks(lambda cp: cp.start())
        lax.fori_loop(0, ne, lambda e, c: pad_rows(e, lambda cp: cp.wait()) or c, 0)
        tail_blocks(lambda cp: cp.wait())

    def token_rows(s, fn):
        def body(r, c):
            for k in range(EXPERT_TOP_K):
                fn(_row_copy(hp.at[s], r, xs_ref, dest_ref[k, r], sem.at[s]), k)
            return c
        lax.fori_loop(0, tm, body, 0)

    token_rows(slot, lambda cp, k: cp.start(priority=k % DMA_PRIORITIES))

    @pl.when(i > 0)
    def _():
        token_rows(1 - slot, lambda cp, k: cp.wait())

    @pl.when(i == n_steps - 1)
    def _():
        token_rows(slot, lambda cp, k: cp.wait())


def _dispatch(starts, cnt, dest, h2, n_rows):
    t, d = h2.shape
    tm = min(TM_DISP, t)
    return pl.pallas_call(
        _dispatch_kernel,
        grid_spec=pltpu.PrefetchScalarGridSpec(
            num_scalar_prefetch=2,
            grid=(t // tm,),
            in_specs=[pl.BlockSpec((EXPERT_TOP_K, tm), lambda i, *_: (0, i), memory_space=pltpu.SMEM),
                      pl.BlockSpec((tm, d), lambda i, *_: (i, 0))],
            out_specs=pl.BlockSpec(memory_space=pl.ANY),
            scratch_shapes=[pltpu.VMEM((2, tm) + _row_tiles(d // 2), jnp.uint32),
                            pltpu.VMEM((min(ROW_BLOCK, n_rows),) + _row_tiles(d // 2), jnp.uint32),
                            pltpu.SemaphoreType.DMA((2,)), pltpu.SemaphoreType.DMA(())]),
        out_shape=jax.ShapeDtypeStruct((n_rows,) + _row_tiles(d // 2), jnp.uint32),
        compiler_params=_cparams(("arbitrary",)),
        name="moe_dispatch",
    )(starts, cnt, dest, h2)


def _mlp_kernel(be_ref, nb_ref, x_ref, wg_ref, wu_ref, wd_ref, y_ref, wg_s, wu_s, wd_s, *, packed):
    i = pl.program_id(0)

    @pl.when((i == 0) | (be_ref[i] != be_ref[jnp.maximum(i - 1, 0)]))
    def _():
        wg_s[...] = wg_ref[...].astype(BF16)
        wu_s[...] = wu_ref[...].astype(BF16)
        wd_s[...] = wd_ref[...].astype(BF16)

    @pl.when(i < nb_ref[0])
    def _():
        if packed:
            xw = x_ref[...]
            xw = xw.reshape(xw.shape[0], xw.shape[1] * xw.shape[2])
            x = jnp.concatenate(_unpack_bf16_pairs(xw), axis=1).astype(BF16)
        else:
            x = x_ref[...].astype(BF16)
        gt = jnp.dot(x, wg_s[...], preferred_element_type=F32)
        up = jnp.dot(x, wu_s[...], preferred_element_type=F32)
        hb = ((gt * _sigmoid(gt)) * up).astype(BF16)
        y = jnp.dot(hb, wd_s[...], preferred_element_type=F32)
        y_ref[...] = _pack_bf16_pairs(y).reshape(y_ref.shape) if packed else y

    @pl.when(i >= nb_ref[0])
    def _():
        y_ref[...] = jnp.zeros_like(y_ref)


def _grouped_mlp(block_expert, n_used, xs, layer, w_gate, w_up, w_down, packed, name):
    n_rows, row_shape = xs.shape[0], xs.shape[1:]
    zeros = (0,) * len(row_shape)
    d, hid = w_gate.shape[2], w_gate.shape[3]
    rb = min(ROW_BLOCK, n_rows)
    w_spec = lambda shape: pl.BlockSpec((None, None) + shape, lambda i, be, nb: (layer, be[i], 0, 0))
    return pl.pallas_call(
        functools.partial(_mlp_kernel, packed=packed),
        grid_spec=pltpu.PrefetchScalarGridSpec(
            num_scalar_prefetch=2,
            grid=(n_rows // rb,),
            in_specs=[pl.BlockSpec((rb,) + row_shape, lambda i, be, nb: (jnp.minimum(i, nb[0] - 1),) + zeros),
                      w_spec((d, hid)), w_spec((d, hid)), w_spec((hid, d))],
            out_specs=pl.BlockSpec((rb,) + row_shape, lambda i, be, nb: (i,) + zeros),
            scratch_shapes=[pltpu.VMEM((d, hid), BF16), pltpu.VMEM((d, hid), BF16), pltpu.VMEM((hid, d), BF16)]),
        out_shape=jax.ShapeDtypeStruct(xs.shape, xs.dtype if packed else F32),
        compiler_params=_cparams(("arbitrary",)),
        name=name,
    )(block_expert, n_used, xs, w_gate, w_up, w_down)


def _combine_kernel(dest_ref, dnext_ref, wt_ref, ysh_ref, x_ref, g2_ref, fn_ref, ys_ref, o_ref, ybuf, sem, *,
                    final):
    i = pl.program_id(0)
    n_steps = pl.num_programs(0)
    slot = i % 2
    tm, d = x_ref.shape

    def token_rows(rows_ref, s, fn):
        def body(r, c):
            for k in range(EXPERT_TOP_K):
                fn(_row_copy(ys_ref, rows_ref[k, r], ybuf.at[s].at[k], r, sem.at[s]), k)
            return c
        lax.fori_loop(0, tm, body, 0)

    start = lambda cp, k: cp.start(priority=k % DMA_PRIORITIES)

    @pl.when(i == 0)
    def _():
        token_rows(dest_ref, 0, start)

    @pl.when(i + 1 < n_steps)
    def _():
        token_rows(dnext_ref, 1 - slot, start)

    token_rows(dest_ref, slot, lambda cp, k: cp.wait())

    wt = wt_ref[...]
    acc_hi = jnp.zeros((tm, d // 2), F32)
    acc_lo = jnp.zeros((tm, d // 2), F32)
    for k in range(EXPERT_TOP_K):
        hi, lo = _unpack_bf16_pairs(ybuf[slot, k].reshape(tm, d // 2))
        acc_hi = acc_hi + wt[:, k:k + 1] * hi
        acc_lo = acc_lo + wt[:, k:k + 1] * lo
    acc = ysh_ref[...] + jnp.concatenate([acc_hi, acc_lo], axis=1)
    xn = x_ref[...] + g2_ref[...] * acc
    if final:
        ms = jnp.mean(xn * xn, axis=-1, keepdims=True)
        xn = (xn * lax.rsqrt(ms + EPS)) * fn_ref[...]
    o_ref[...] = xn


def _combine(dest, wts_t, y_sorted, y_shared, x2d, modr, mod_base, seq, final_norm, final):
    t, d = x2d.shape
    tm = min(TM_COMB, seq)
    row_spec = pl.BlockSpec((tm, d), lambda i: (i, 0))
    n_steps = t // tm
    return pl.pallas_call(
        functools.partial(_combine_kernel, final=final),
        grid=(n_steps,),
        in_specs=[pl.BlockSpec((EXPERT_TOP_K, tm), lambda i: (0, i), memory_space=pltpu.SMEM),
                  pl.BlockSpec((EXPERT_TOP_K, tm), lambda i: (0, jnp.minimum(i + 1, n_steps - 1)),
                               memory_space=pltpu.SMEM),
                  pl.BlockSpec((tm, EXPERT_TOP_K), lambda i: (i, 0)),
                  row_spec, row_spec,
                  pl.BlockSpec((None, 1, d), lambda i: (mod_base + ((i * tm) // seq) * 6 + 5, 0, 0)),
                  pl.BlockSpec((1, d), lambda i: (0, 0)),
                  pl.BlockSpec(memory_space=pl.ANY)],
        out_specs=row_spec,
        out_shape=jax.ShapeDtypeStruct((t, d), F32),
        scratch_shapes=[pltpu.VMEM((2, EXPERT_TOP_K, tm) + _row_tiles(d // 2), jnp.uint32),
                        pltpu.SemaphoreType.DMA((2,))],
        compiler_params=_cparams(("arbitrary",)),
        name="moe_combine",
    )(dest, dest, wts_t, y_shared, x2d, modr, final_norm.reshape(1, d), y_sorted)


def _layout(d_model):
    r, a, kvw = RNN_WIDTH, N_HEADS * HEAD_DIM, N_KV_GROUPS * HEAD_DIM
    off = {}
    off["ma"], off["mb"] = 0, d_model
    off["u"] = 2 * d_model
    off["g"] = off["u"] + r
    off["kc"] = off["g"] + r
    off["vc"] = off["kc"] + kvw
    off["nf"] = off["vc"] + kvw
    off["q"] = 0
    off["ks"], off["vs"], off["kw"], off["vw"] = a, a + kvw, a + 2 * kvw, a + 3 * kvw
    off["gn"] = a + 4 * kvw
    off["nb"] = off["gn"] + TN_IN
    assert off["nf"] % TN_IN == 0 and off["gn"] % TN_IN == 0
    return off


def _pack_w_in(w_in_l, d_model):
    r, a, kvw = RNN_WIDTH, N_HEADS * HEAD_DIM, N_KV_GROUPS * HEAD_DIM
    off = _layout(d_model)
    s_u, s_g, s_q = 0, r, 2 * r
    s_kv = s_q + a
    s_gn = s_kv + 6 * kvw
    s_ma = s_gn + 3 * N_HEADS
    s_mb = s_ma + d_model
    cols = [w_in_l[:, s_ma:s_mb], w_in_l[:, s_mb:s_mb + d_model], w_in_l[:, s_u:s_g], w_in_l[:, s_g:s_q],
            w_in_l[:, s_kv:s_kv + 2 * kvw],
            w_in_l[:, s_q:s_kv], w_in_l[:, s_kv + 2 * kvw:s_gn], w_in_l[:, s_gn:s_ma]]
    w = jnp.concatenate(cols, axis=1)
    return jnp.pad(w, ((0, 0), (0, off["nf"] + off["nb"] - w.shape[1]))).astype(BF16)


def _moe_plan(idx, rank, counts, n_tok):
    cnt = counts[:, 0].astype(jnp.int32)
    padded = (cnt + ROW_BLOCK - 1) // ROW_BLOCK * ROW_BLOCK
    ends = jnp.cumsum(padded)
    starts = jnp.concatenate([jnp.zeros((1,), jnp.int32), ends]).astype(jnp.int32)
    onehot = idx[:, :, None] == jnp.arange(N_EXPERTS, dtype=jnp.int32)[None, None, :]
    dest = rank + jnp.sum(jnp.where(onehot, starts[None, None, :N_EXPERTS], 0), axis=2)
    n_blocks = (n_tok * EXPERT_TOP_K) // ROW_BLOCK + N_EXPERTS
    blk_start = jnp.arange(n_blocks, dtype=jnp.int32) * ROW_BLOCK
    owner = jnp.sum((ends[None, :] <= blk_start[:, None]).astype(jnp.int32), axis=1)
    block_expert = jnp.minimum(owner, N_EXPERTS - 1).astype(jnp.int32)
    n_used = (ends[-1] // ROW_BLOCK).astype(jnp.int32).reshape(1)
    return starts, cnt, dest.astype(jnp.int32), block_expert, n_used, n_blocks * ROW_BLOCK


def kernel(x, c, rel_bias, final_norm, ada_w, ada_b, norm_mix, norm_ffn, w_in, conv_w, conv_b, lru_wa, lru_ba, lru_wx, lru_bx, lru_lambda, cmp_pe_k, cmp_w1_k, cmp_w2_k, cmp_pe_v, cmp_w1_v, cmp_w2_v, w_up_rnn, w_up_att, w_out, router_w, router_bias, exp_w_gate, exp_w_up, exp_w_down, sh_w_gate, sh_w_up, sh_w_down):
    batch, seq, d = x.shape
    n_tok = batch * seq
    depth = ada_w.shape[0]
    g_, dh = N_KV_GROUPS, HEAD_DIM
    a_w, kvw, r = N_HEADS * HEAD_DIM, N_KV_GROUPS * HEAD_DIM, RNN_WIDTH
    off = _layout(d)
    assert seq % CMP_STRIDE == 0 and CMP_BLOCK == 2 * CMP_STRIDE

    mod = _adaln_mod(c, ada_w, ada_b)
    modr = mod.reshape(depth * batch * 6, 1, d)
    tables = _bias_tables(rel_bias, seq)
    nc = tables[2].shape[1]
    n_chunk = seq // CMP_STRIDE

    x2d = x.reshape(n_tok, d)
    for l in range(depth):
        mod_base = l * batch * 6
        zf, zb, zg = _inproj(x2d, norm_mix[l], modr, mod_base, seq, _pack_w_in(w_in[l], d), off["nf"])
        y_rnn = _rglru(zf, batch, seq, off["u"] // r, off["g"] // r, conv_w[l], conv_b[l],
                       lru_wa[l], lru_ba[l], lru_wx[l], lru_bx[l], lru_lambda[l])

        zb3 = zb.reshape(batch, seq, -1)
        zcol = lambda z3, name, w: z3[:, :, off[name]:off[name] + w]

        def unfold(v):
            ch = jnp.transpose(v.reshape(batch, n_chunk, CMP_STRIDE, g_, dh), (0, 3, 1, 2, 4))
            ch = ch.reshape(batch * g_, n_chunk, CMP_STRIDE * dh)
            blocks = jnp.concatenate([ch[:, :-1], ch[:, 1:]], axis=2)
            return jnp.pad(blocks, ((0, 0), (0, nc - (n_chunk - 1)), (0, 0)))

        zf3 = zf.reshape(batch, seq, -1)
        kc, vc = _compress(unfold(zcol(zf3, "kc", kvw)), unfold(zcol(zf3, "vc", kvw)),
                           cmp_pe_k[l], cmp_w1_k[l], cmp_w2_k[l], cmp_pe_v[l], cmp_w1_v[l], cmp_w2_v[l])
        kc = jnp.transpose(kc.reshape(batch, g_, nc, dh), (0, 2, 1, 3)).reshape(batch, nc, kvw)
        vcT = jnp.swapaxes(vc.reshape(batch, g_, nc, dh), 2, 3).reshape(batch, kvw, nc)
        qT = jnp.swapaxes(zcol(zb3, "q", a_w), 1, 2)
        vsT = jnp.swapaxes(zcol(zb3, "vs", kvw), 1, 2)
        vwT = jnp.swapaxes(zcol(zb3, "vw", kvw), 1, 2)
        gT = jnp.swapaxes(zg.reshape(batch, seq, LANE)[:, :, :3 * N_HEADS], 1, 2)
        y_att = _attention(qT, kc, vcT, zb3, off["ks"] // kvw, off["kw"] // kvw, vsT, vwT, gT, tables,
                           batch, seq).reshape(n_tok, a_w)

        merged = _merge(y_rnn, y_att, w_up_rnn[l].astype(BF16), w_up_att[l].astype(BF16), zf,
                        off["ma"] // d, off["mb"] // d)
        x1, h2 = _outproj(merged, w_out[l].astype(BF16), x2d, modr, mod_base, seq, norm_ffn[l])

        idx, wts, rank, counts = _router(h2, router_w[l], router_bias[l])
        starts, cnt, dest, block_expert, n_used, n_rows = _moe_plan(idx, rank, counts, n_tok)
        xs = _dispatch(starts, cnt, dest, h2, n_rows)
        y_sorted = _grouped_mlp(block_expert, n_used, xs, l, exp_w_gate, exp_w_up, exp_w_down, True, "moe_experts")
        sh_blocks = n_tok // min(ROW_BLOCK, n_tok)
        y_shared = _grouped_mlp(jnp.zeros((sh_blocks,), jnp.int32), jnp.full((1,), sh_blocks, jnp.int32), h2, l,
                                sh_w_gate[:, None], sh_w_up[:, None], sh_w_down[:, None], False, "moe_shared")
        x2d = _combine(dest, wts.T, y_sorted, y_shared, x1, modr, mod_base, seq, final_norm, l == depth - 1)
    return x2d.reshape(batch, seq, d)
```

```python
import functools
import math

import numpy as np
import jax
import jax.numpy as jnp
from jax import lax
from jax.experimental import pallas as pl
from jax.experimental.pallas import tpu as pltpu

DEPTH = 2
RNN_WIDTH = 1024
RNN_BLOCKS = 8
CONV_WIDTH = 4
LRU_C = 8.0
N_HEADS = 16
N_KV_GROUPS = 4
HEAD_DIM = 64
CMP_BLOCK = 32
CMP_STRIDE = 16
CMP_HIDDEN = 128
SEL_BLOCK = 64
SEL_TOP_N = 8
WINDOW = 512
N_BUCKETS = 32
MAX_DISTANCE = 128
N_EXPERTS = 64
EXPERT_TOP_K = 8
N_EXPERT_GROUPS = 8
TOPK_EXPERT_GROUPS = 4
ROUTED_SCALE = 2.5
EPS = 1e-6
NEG = -1e30

LANE = 128
SUBLANE = 8
VMEM_LIMIT = 52 * 1024 * 1024
DMA_PRIORITIES = 1

TM_IN = 1024
TN_IN = 512
TN_MOD = 1024
TC_RNN = 256
TQ = 128
FAR_TILES = 4
ONES_ROWS = 16
TM_MERGE = 512
TN_MERGE = 2048
TM_OUT = 512
TM_ROUTE = 256
TM_DISP = 128
ROW_BLOCK = 256
TM_COMB = 128

F32 = jnp.float32
BF16 = jnp.bfloat16


def _cparams(sem):
    return pltpu.CompilerParams(dimension_semantics=sem, vmem_limit_bytes=VMEM_LIMIT)


def _round_up(a, b):
    return (a + b - 1) // b * b


def _tile(n, pref):
    if n <= pref:
        return n
    t = pref // LANE * LANE
    while n % t:
        t -= LANE
    return t


def _gelu_tanh(x):
    return x * (0.5 * (1.0 + jnp.tanh(math.sqrt(2.0 / math.pi) * (x + 0.044715 * (x * x * x)))))


def _sigmoid(x):
    return jax.nn.sigmoid(x)


def _mod_kernel(c_ref, w_ref, b_ref, o_ref):
    c = c_ref[...]
    ca = (c * _sigmoid(c)).astype(BF16)
    o_ref[...] = jnp.dot(ca, w_ref[...].astype(BF16), preferred_element_type=F32) + b_ref[...]


def _adaln_mod(c, ada_w, ada_b):
    nl, d, n6 = ada_w.shape
    b = c.shape[0]
    tn = _tile(n6, TN_MOD)
    return pl.pallas_call(
        _mod_kernel,
        grid=(nl, n6 // tn),
        in_specs=[pl.BlockSpec((b, d), lambda l, j: (0, 0)),
                  pl.BlockSpec((None, d, tn), lambda l, j: (l, 0, j)),
                  pl.BlockSpec((None, 1, tn), lambda l, j: (l, 0, j))],
        out_specs=pl.BlockSpec((None, b, tn), lambda l, j: (l, 0, j)),
        out_shape=jax.ShapeDtypeStruct((nl, b, n6), F32),
        compiler_params=_cparams(("arbitrary", "arbitrary")),
        name="adaln_mod",
    )(c, ada_w, ada_b.reshape(nl, 1, n6))


def _inproj_kernel(x_ref, nw_ref, sc_ref, sh_ref, w_ref, zf_ref, zb_ref, zg_ref, h_ref, *, nf, nj):
    j = pl.program_id(1)

    @pl.when(j == 0)
    def _():
        x = x_ref[...]
        ms = jnp.mean(x * x, axis=-1, keepdims=True)
        y = (x * lax.rsqrt(ms + EPS)) * nw_ref[...]
        h_ref[...] = (y * (1.0 + sc_ref[...]) + sh_ref[...]).astype(BF16)

    res = jnp.dot(h_ref[...], w_ref[...], preferred_element_type=F32)

    @pl.when(j < nf)
    def _():
        zf_ref[...] = res

    @pl.when(j >= nf)
    def _():
        zb_ref[...] = res.astype(BF16)

    @pl.when(j == nj - 1)
    def _():
        zg_ref[...] = res[:, 0:LANE]


def _inproj(x2d, norm_w, modr, mod_base, seq, w_p, n_f32):
    t, d = x2d.shape
    n_p = w_p.shape[1]
    tm = min(TM_IN, seq)
    tn = TN_IN
    assert n_f32 % tn == 0 and n_p % tn == 0
    nf, nj = n_f32 // tn, n_p // tn

    def mod_idx(k):
        return lambda i, j: (mod_base + ((i * tm) // seq) * 6 + k, 0, 0)

    return pl.pallas_call(
        functools.partial(_inproj_kernel, nf=nf, nj=nj),
        grid=(t // tm, nj),
        in_specs=[pl.BlockSpec((tm, d), lambda i, j: (i, 0)),
                  pl.BlockSpec((1, d), lambda i, j: (0, 0)),
                  pl.BlockSpec((None, 1, d), mod_idx(1)),
                  pl.BlockSpec((None, 1, d), mod_idx(0)),
                  pl.BlockSpec((d, tn), lambda i, j: (0, j))],
        out_specs=[pl.BlockSpec((tm, tn), lambda i, j: (i, jnp.minimum(j, nf - 1))),
                   pl.BlockSpec((tm, tn), lambda i, j: (i, jnp.maximum(j - nf, 0))),
                   pl.BlockSpec((tm, LANE), lambda i, j: (i, 0))],
        out_shape=[jax.ShapeDtypeStruct((t, n_f32), F32), jax.ShapeDtypeStruct((t, n_p - n_f32), BF16),
                   jax.ShapeDtypeStruct((t, LANE), F32)],
        scratch_shapes=[pltpu.VMEM((tm, d), BF16)],
        compiler_params=_cparams(("arbitrary", "arbitrary")),
        name="inproj",
    )(x2d, norm_w.reshape(1, d), modr, modr, w_p)


def _rglru_kernel(u_ref, g_ref, cw_ref, cb_ref, wa_ref, ba_ref, wx_ref, bx_ref, lam_ref, y_ref,
                  ubuf, a_s, b_s, h_s, hcar):
    tc, r = u_ref.shape
    nb = wa_ref.shape[0]
    bw = r // nb

    @pl.when(pl.program_id(1) == 0)
    def _():
        ubuf[0:SUBLANE, :] = jnp.zeros((SUBLANE, r), F32)
        hcar[...] = jnp.zeros_like(hcar)

    ubuf[SUBLANE:SUBLANE + tc, :] = u_ref[...]
    cw = cw_ref[...]
    uc = cb_ref[...] + cw[CONV_WIDTH - 1:CONV_WIDTH, :] * ubuf[SUBLANE:SUBLANE + tc, :]
    for k in range(CONV_WIDTH - 1):
        off = SUBLANE - (CONV_WIDTH - 1) + k
        uc = uc + cw[k:k + 1, :] * ubuf[off:off + tc, :]
    ubuf[0:SUBLANE, :] = ubuf[tc:tc + SUBLANE, :]

    ucb = uc.astype(BF16)
    rp = []
    xp = []
    for n in range(nb):
        blk = ucb[:, n * bw:(n + 1) * bw]
        rp.append(jnp.dot(blk, wa_ref[n], preferred_element_type=F32))
        xp.append(jnp.dot(blk, wx_ref[n], preferred_element_type=F32))
    rg = _sigmoid(jnp.concatenate(rp, axis=1) + ba_ref[...])
    ig = _sigmoid(jnp.concatenate(xp, axis=1) + bx_ref[...])
    nl = -lam_ref[...]
    sp = jnp.maximum(nl, 0.0) + jnp.log1p(jnp.exp(-jnp.abs(nl)))
    log_a = (-LRU_C * rg) * sp
    a_s[...] = jnp.exp(log_a)
    th = jnp.tanh(log_a)
    one_minus_a2 = (-2.0 * th) / (1.0 - th)
    b_s[...] = jnp.sqrt(one_minus_a2) * (ig * uc)

    def step(t, h):
        h = a_s[pl.ds(t, 1), :] * h + b_s[pl.ds(t, 1), :]
        h_s[pl.ds(t, 1), :] = h
        return h

    h_last = lax.fori_loop(0, tc, step, hcar[0:1, :], unroll=8)
    hcar[0:1, :] = h_last
    y_ref[...] = (_gelu_tanh(g_ref[...]) * h_s[...]).astype(y_ref.dtype)


def _rglru(z2d, batch, seq, u_blk, g_blk, conv_w, conv_b, wa, ba, wx, bx, lam):
    r = conv_w.shape[1]
    tc = min(TC_RNN, seq)
    nt = seq // tc
    nb, bw, _ = wa.shape
    row = lambda v: v.reshape(1, r)
    full = lambda shape: pl.BlockSpec(shape, lambda b, t: (0,) * len(shape))
    return pl.pallas_call(
        _rglru_kernel,
        grid=(batch, nt),
        in_specs=[pl.BlockSpec((tc, r), lambda b, t: (b * nt + t, u_blk)),
                  pl.BlockSpec((tc, r), lambda b, t: (b * nt + t, g_blk)),
                  full((CONV_WIDTH, r)), full((1, r)),
                  full((nb, bw, bw)), full((1, r)),
                  full((nb, bw, bw)), full((1, r)), full((1, r))],
        out_specs=pl.BlockSpec((tc, r), lambda b, t: (b * nt + t, 0)),
        out_shape=jax.ShapeDtypeStruct((batch * seq, r), BF16),
        scratch_shapes=[pltpu.VMEM((tc + SUBLANE, r), F32), pltpu.VMEM((tc, r), F32),
                        pltpu.VMEM((tc, r), F32), pltpu.VMEM((tc, r), F32),
                        pltpu.VMEM((SUBLANE, r), F32)],
        compiler_params=_cparams(("arbitrary", "arbitrary")),
        name="rglru",
    )(z2d, z2d, conv_w, row(conv_b), wa.astype(BF16), row(ba), wx.astype(BF16), row(bx), row(lam))


def _compress_kernel(xk_ref, xv_ref, pek_ref, w1k_ref, w2k_ref, pev_ref, w1v_ref, w2v_ref, kc_ref, vc_ref):
    def one(x_ref, pe_ref, w1_ref, w2_ref, o_ref):
        blocks = (x_ref[...] + pe_ref[...]).astype(BF16)
        hid = _gelu_tanh(jnp.dot(blocks, w1_ref[...], preferred_element_type=F32))
        o_ref[...] = jnp.dot(hid.astype(BF16), w2_ref[...], preferred_element_type=F32).astype(o_ref.dtype)

    one(xk_ref, pek_ref, w1k_ref, w2k_ref, kc_ref)
    one(xv_ref, pev_ref, w1v_ref, w2v_ref, vc_ref)


def _compress(xk, xv, pe_k, w1_k, w2_k, pe_v, w1_v, w2_v):
    bg, nc, kd = xk.shape
    dh = w2_k.shape[1]
    hid = w2_k.shape[0]
    x_spec = pl.BlockSpec((None, nc, kd), lambda i: (i, 0, 0))
    full = lambda shape: pl.BlockSpec(shape, lambda i: (0,) * len(shape))
    o_spec = pl.BlockSpec((None, nc, dh), lambda i: (i, 0, 0))
    prep = lambda pe, w1, w2: (pe.reshape(1, kd), w1.reshape(kd, hid).astype(BF16), w2.astype(BF16))
    return pl.pallas_call(
        _compress_kernel,
        grid=(bg,),
        in_specs=[x_spec, x_spec, full((1, kd)), full((kd, hid)), full((hid, dh)),
                  full((1, kd)), full((kd, hid)), full((hid, dh))],
        out_specs=[o_spec, o_spec],
        out_shape=[jax.ShapeDtypeStruct((bg, nc, dh), BF16)] * 2,
        compiler_params=_cparams(("arbitrary",)),
        name="nsa_compress",
    )(xk, xv, *prep(pe_k, w1_k, w2_k), *prep(pe_v, w1_v, w2_v))


def _col_max(x):
    return jnp.max(x, axis=0, keepdims=True)


def _attn_kernel(qT_ref, kc_ref, vcT_ref, ks_ref, kw_ref, vsT_ref, vwT_ref, gT_ref,
                 tnear_ref, cfar_ref, cbias_ref, ovl_ref, o_ref,
                 qbd, sel_s, m_s, acc_s):
    i = pl.program_id(1)
    g_, hg, dh = N_KV_GROUPS, N_HEADS // N_KV_GROUPS, HEAD_DIM
    tq = qT_ref.shape[1]
    gw = hg * tq
    nw = g_ * gw
    n_sel = sel_s.shape[0]
    per = tq // SEL_BLOCK

    qT = (qT_ref[...].astype(F32) * (HEAD_DIM ** -0.5)).astype(BF16)
    zero_blk = jnp.zeros((dh, gw), BF16)
    rows = []
    for g in range(g_):
        qcat = jnp.concatenate([qT[(g * hg + h) * dh:(g * hg + h + 1) * dh, :] for h in range(hg)], axis=1)
        rows.append(jnp.concatenate([zero_blk] * g + [qcat] + [zero_blk] * (g_ - 1 - g), axis=1))
    qbd[...] = jnp.concatenate(rows, axis=0)

    def lane_q(shape):
        return lax.broadcasted_iota(jnp.int32, shape, 1) % tq

    def pv(vT_ref, start, nrows, p):
        pb = p.astype(BF16)
        outs = []
        for g in range(g_):
            if start is None:
                v = vT_ref[g * dh:(g + 1) * dh, :]
            else:
                v = jnp.concatenate([vT_ref[g * dh:(g + 1) * dh, pl.ds(start, nrows)],
                                     jnp.ones((ONES_ROWS, nrows), BF16)], axis=0)
            outs.append(jnp.dot(v, pb[:, g * gw:(g + 1) * gw], preferred_element_type=F32))
        return jnp.concatenate(outs, axis=1)

    nc = kc_ref.shape[0]
    near_t = jnp.maximum(i - 1, 0)
    near_start = pl.multiple_of(near_t * tq, tq)
    far_end = near_t
    wr = WINDOW - tq
    win_start = pl.multiple_of(jnp.maximum(i - WINDOW // tq, 0) * tq, tq)
    lhs = jnp.concatenate([kc_ref[...], ks_ref[pl.ds(near_start, 2 * tq), :],
                           kw_ref[pl.ds(near_start, 2 * tq), :], kw_ref[pl.ds(win_start, wr), :]], axis=0)
    s_all = jnp.dot(lhs, qbd[...], preferred_element_type=F32)
    r_slc, r_win, r_far = nc, nc + 2 * tq, nc + 4 * tq

    sc = s_all[0:nc] + cbias_ref[...]
    nrow = lax.broadcasted_iota(jnp.int32, (nc, nw), 0)
    cmask = (nrow * CMP_STRIDE + (CMP_BLOCK - 1)) <= i * tq + lane_q((nc, nw))
    sc = jnp.where(cmask, sc, NEG)
    pc = jnp.where(cmask, jnp.exp(sc - _col_max(sc)), 0.0)
    lc = jnp.sum(pc, axis=0, keepdims=True)
    pc = pc * jnp.where(lc > 0.0, 1.0 / lc, 0.0)
    o_cmp = pv(vcT_ref, None, nc, pc)

    psum = jnp.concatenate(
        [sum(pc[:, g * gw + h * tq:g * gw + (h + 1) * tq] for h in range(hg)) for g in range(g_)], axis=1)
    imp = jnp.dot(ovl_ref[...], psum, preferred_element_type=F32, precision=lax.Precision.HIGHEST)
    jrow = lax.broadcasted_iota(jnp.int32, (n_sel, g_ * tq), 0)
    tq_abs = i * tq + lane_q((n_sel, g_ * tq))
    cur = tq_abs // SEL_BLOCK
    forced = (jrow == 0) | (jrow == cur) | (jrow == cur - 1)
    valid = jrow * SEL_BLOCK <= tq_abs
    work = jnp.where(forced, jnp.inf, jnp.where(valid, imp, -jnp.inf))
    jrow_f = jrow.astype(F32)
    sel = jnp.zeros((n_sel, g_ * tq), F32)
    for _ in range(min(SEL_TOP_N, n_sel)):
        mx = _col_max(work)
        first = jnp.min(jnp.where(work == mx, jrow_f, float(n_sel)), axis=0, keepdims=True)
        pick = jrow_f == first
        sel = jnp.where(pick, 1.0, sel)
        work = jnp.where(pick, -jnp.inf, work)
    sel_s[...] = jnp.where(sel > 0.5, 0.0, NEG)

    def sel_add(first_blk, n_blk, limit_blk=None, bias_row=None):
        parts = []
        for c in range(n_blk):
            rowv = sel_s[pl.ds(first_blk + c, 1), :]
            if limit_blk is not None:
                rowv = rowv + jnp.where(first_blk + c < limit_blk, 0.0, NEG)
            rowv = jnp.concatenate([rowv[:, g * tq:(g + 1) * tq] for g in range(g_) for _ in range(hg)], axis=1)
            if bias_row is not None:
                rowv = rowv + bias_row
            parts.append(jnp.broadcast_to(rowv, (SEL_BLOCK, nw)))
        return jnp.concatenate(parts, axis=0)

    def scores(k_ref, start, nrows):
        return jnp.dot(k_ref[pl.ds(start, nrows), :], qbd[...], preferred_element_type=F32)

    def flash_init(s, vT_ref, start, nrows):
        m = _col_max(s)
        m_s[...] = m
        acc_s[...] = pv(vT_ref, start, nrows, jnp.exp(s - m))

    def flash_update(s, vT_ref, start, nrows):
        m_old = m_s[...]
        m_new = jnp.maximum(m_old, _col_max(s))
        m_s[...] = m_new
        acc_s[...] = jnp.exp(m_old - m_new) * acc_s[...] + pv(vT_ref, start, nrows, jnp.exp(s - m_new))

    def flash_result():
        acc = acc_s[...]
        return acc[0:dh] * (1.0 / acc[dh:dh + 1])

    bias_off = pl.multiple_of(jnp.where(i == 0, tq, 0), tq)
    near_add = tnear_ref[pl.ds(bias_off, 2 * tq), :]

    flash_init(s_all[r_slc:r_win] + near_add + sel_add(near_t * per, 2 * per), vsT_ref, near_start, 2 * tq)
    fr = FAR_TILES * tq

    def far_slc(c, carry):
        hi_t = far_end - c * FAR_TILES
        st_t = jnp.maximum(hi_t - FAR_TILES, 0)
        start = pl.multiple_of(st_t * tq, tq)
        add = sel_add(st_t * per, FAR_TILES * per, hi_t * per, cfar_ref[...])
        flash_update(scores(ks_ref, start, fr) + add, vsT_ref, start, fr)
        return carry

    lax.fori_loop(0, (far_end + FAR_TILES - 1) // FAR_TILES, far_slc, 0)
    o_slc = flash_result()

    flash_init(s_all[r_win:r_far] + near_add, vwT_ref, near_start, 2 * tq)
    nwin = WINDOW // tq
    parts = []
    for c in range(wr // tq):
        tile_ok = win_start // tq + c < far_end
        part = s_all[r_far + c * tq:r_far + (c + 1) * tq] + (cfar_ref[...] + jnp.where(tile_ok, 0.0, NEG))
        if c == 0:
            krow = lax.broadcasted_iota(jnp.int32, (tq, nw), 0)
            part = part + jnp.where(krow > lane_q((tq, nw)) - jnp.where(i < nwin, tq, 0), 0.0, NEG)
        parts.append(part)
    flash_update(jnp.concatenate(parts, axis=0), vwT_ref, win_start, wr)
    o_win = flash_result()

    gate = _sigmoid(gT_ref[...])

    def grow(j):
        return jnp.concatenate([gate[h * 3 + j:h * 3 + j + 1, :] for h in range(N_HEADS)], axis=1)

    o_t = grow(0) * o_cmp + grow(1) * o_slc + grow(2) * o_win
    o_hd = jnp.concatenate([o_t[:, h * tq:(h + 1) * tq] for h in range(N_HEADS)], axis=0)
    o_ref[...] = o_hd.T.astype(o_ref.dtype)


def _t5_bucket_np(dist):
    n = np.maximum(dist, 0)
    max_exact = N_BUCKETS // 2
    nf = np.maximum(n, 1).astype(np.float64)
    large = max_exact + (np.log(nf / max_exact) / math.log(MAX_DISTANCE / max_exact)
                         * (N_BUCKETS - max_exact)).astype(np.int64)
    large = np.minimum(large, N_BUCKETS - 1)
    return np.where(n < max_exact, n, large).astype(np.int32)


def _bias_tables(rel_bias, seq):
    tq = min(TQ, seq)
    assert MAX_DISTANCE <= tq, "tiles two or more behind the diagonal must all fall in the last bucket"
    nh = N_HEADS
    nc = _round_up((seq - CMP_BLOCK) // CMP_STRIDE + 1, LANE)
    off_max = (nc - 1) * CMP_STRIDE + CMP_BLOCK - 1
    fd = rel_bias.astype(F32)[_t5_bucket_np(np.arange(-off_max, seq))]
    fdT = fd.T
    kk = np.arange(tq)[:, None]
    qq = np.arange(tq)[None, :]

    def toeplitz(offset):
        idx = off_max + np.maximum(offset + qq - kk, 0)
        return jnp.transpose(fd[idx], (0, 2, 1)).reshape(tq, nh * tq)

    future = jnp.asarray(np.tile(np.where(kk <= qq, 0.0, NEG).astype(np.float32), (1, nh)))
    tnear = jnp.concatenate([toeplitz(tq), toeplitz(0) + future, jnp.full((tq, nh * tq), NEG, F32)], axis=0)
    cfar = jnp.broadcast_to(rel_bias.astype(F32)[N_BUCKETS - 1][:, None], (nh, tq)).reshape(1, nh * tq)
    rows = [lax.slice_in_dim(fdT, off_max - (n * CMP_STRIDE + CMP_BLOCK - 1),
                             off_max - (n * CMP_STRIDE + CMP_BLOCK - 1) + seq, axis=1) for n in range(nc)]
    cb = jnp.stack(rows, axis=0).reshape(nc, nh, seq // tq, tq)
    cbias = jnp.transpose(cb, (2, 0, 1, 3)).reshape(seq // tq, nc, nh * tq)
    n_sel = seq // SEL_BLOCK
    cmp_start = np.arange(nc) * CMP_STRIDE
    sel_start = np.arange(n_sel) * SEL_BLOCK
    ovl = ((cmp_start[None, :] < sel_start[:, None] + SEL_BLOCK)
           & (cmp_start[None, :] + CMP_BLOCK > sel_start[:, None])).astype(np.float32)
    return tnear, cfar, cbias, jnp.asarray(ovl)


def _attention(qT, kc, vcT, zb3, ks_blk, kw_blk, vsT, vwT, gT, tables, batch, seq):
    tnear, cfar, cbias, ovl = tables
    g_, dh = N_KV_GROUPS, HEAD_DIM
    tq = min(TQ, seq)
    assert seq % tq == 0 and seq >= FAR_TILES * tq and WINDOW % tq == 0 and tq % SEL_BLOCK == 0
    nqt = seq // tq
    nw = N_HEADS * tq
    nc = kc.shape[1]
    n_sel = seq // SEL_BLOCK
    kvw = g_ * dh
    per_b = lambda shape: pl.BlockSpec((None,) + shape, lambda b, i: (b,) + (0,) * len(shape))
    full = lambda shape: pl.BlockSpec(shape, lambda b, i: (0,) * len(shape))
    return pl.pallas_call(
        _attn_kernel,
        grid=(batch, nqt),
        in_specs=[pl.BlockSpec((None, N_HEADS * dh, tq), lambda b, i: (b, 0, i)),
                  per_b((nc, kvw)),
                  per_b((kvw, nc)),
                  pl.BlockSpec((None, seq, kvw), lambda b, i: (b, 0, ks_blk)),
                  pl.BlockSpec((None, seq, kvw), lambda b, i: (b, 0, kw_blk)),
                  per_b((kvw, seq)), per_b((kvw, seq)),
                  pl.BlockSpec((None, 3 * N_HEADS, tq), lambda b, i: (b, 0, i)),
                  full((3 * tq, nw)), full((1, nw)),
                  pl.BlockSpec((None, nc, nw), lambda b, i: (i, 0, 0)),
                  full((n_sel, nc))],
        out_specs=pl.BlockSpec((None, tq, N_HEADS * dh), lambda b, i: (b, i, 0)),
        out_shape=jax.ShapeDtypeStruct((batch, seq, N_HEADS * dh), BF16),
        scratch_shapes=[pltpu.VMEM((kvw, nw), BF16), pltpu.VMEM((n_sel, g_ * tq), F32),
                        pltpu.VMEM((1, nw), F32), pltpu.VMEM((dh + ONES_ROWS, nw), F32)],
        compiler_params=_cparams(("arbitrary", "arbitrary")),
        name="nsa_attention",
    )(qT, kc, vcT, zb3, zb3, vsT, vwT, gT, tnear, cfar, cbias, ovl)


def _merge_kernel(yr_ref, ya_ref, wr_ref, wa_ref, ma_ref, mb_ref, o_ref):
    pr = jnp.dot(yr_ref[...], wr_ref[...], preferred_element_type=F32)
    pa = jnp.dot(ya_ref[...], wa_ref[...], preferred_element_type=F32)
    o_ref[...] = (_sigmoid(ma_ref[...]) * pr + _sigmoid(mb_ref[...]) * pa).astype(o_ref.dtype)


def _merge(y_rnn, y_att, w_ur, w_ua, z2d, ma_blk, mb_blk):
    t, r = y_rnn.shape
    a = y_att.shape[1]
    d = w_ur.shape[1]
    tm = min(TM_MERGE, t)
    tn = min(TN_MERGE, d)
    nj = d // tn
    return pl.pallas_call(
        _merge_kernel,
        grid=(t // tm, nj),
        in_specs=[pl.BlockSpec((tm, r), lambda i, j: (i, 0)),
                  pl.BlockSpec((tm, a), lambda i, j: (i, 0)),
                  pl.BlockSpec((r, tn), lambda i, j: (0, j)),
                  pl.BlockSpec((a, tn), lambda i, j: (0, j)),
                  pl.BlockSpec((tm, tn), lambda i, j: (i, ma_blk * nj + j)),
                  pl.BlockSpec((tm, tn), lambda i, j: (i, mb_blk * nj + j))],
        out_specs=pl.BlockSpec((tm, tn), lambda i, j: (i, j)),
        out_shape=jax.ShapeDtypeStruct((t, d), BF16),
        compiler_params=_cparams(("arbitrary", "arbitrary")),
        name="merge",
    )(y_rnn, y_att, w_ur, w_ua, z2d, z2d)


def _outproj_kernel(mg_ref, w_ref, x_ref, g1_ref, nw_ref, sc_ref, sh_ref, x1_ref, h2_ref):
    x1 = x_ref[...] + g1_ref[...] * jnp.dot(mg_ref[...], w_ref[...], preferred_element_type=F32)
    x1_ref[...] = x1
    ms = jnp.mean(x1 * x1, axis=-1, keepdims=True)
    y = (x1 * lax.rsqrt(ms + EPS)) * nw_ref[...]
    h2_ref[...] = (y * (1.0 + sc_ref[...]) + sh_ref[...]).astype(h2_ref.dtype)


def _outproj(merged, w_out, x2d, modr, mod_base, seq, norm_w):
    t, d = x2d.shape
    tm = min(TM_OUT, seq)

    def mod_idx(k):
        return lambda i: (mod_base + ((i * tm) // seq) * 6 + k, 0, 0)

    row_spec = pl.BlockSpec((tm, d), lambda i: (i, 0))
    return pl.pallas_call(
        _outproj_kernel,
        grid=(t // tm,),
        in_specs=[row_spec, pl.BlockSpec((d, d), lambda i: (0, 0)), row_spec,
                  pl.BlockSpec((None, 1, d), mod_idx(2)),
                  pl.BlockSpec((1, d), lambda i: (0, 0)),
                  pl.BlockSpec((None, 1, d), mod_idx(4)),
                  pl.BlockSpec((None, 1, d), mod_idx(3))],
        out_specs=[row_spec, row_spec],
        out_shape=[jax.ShapeDtypeStruct((t, d), F32), jax.ShapeDtypeStruct((t, d), BF16)],
        compiler_params=_cparams(("arbitrary",)),
        name="outproj",
    )(merged, w_out, x2d, modr, norm_w.reshape(1, d), modr, modr)


def _router_kernel(h_ref, rw_ref, rb_ref, idx_ref, wt_ref, rank_ref, cnt_ref, carry):
    ne = rw_ref.shape[0]
    tm = h_ref.shape[0]
    per = ne // N_EXPERT_GROUPS
    assert per == SUBLANE, "one expert group per sublane tile"

    @pl.when(pl.program_id(0) == 0)
    def _():
        carry[...] = jnp.zeros_like(carry)

    logits = lax.dot_general(rw_ref[...], h_ref[...].astype(BF16), (((1,), (1,)), ((), ())),
                             preferred_element_type=F32)
    scores = _sigmoid(logits)
    biased = scores + rb_ref[...]
    erow = lax.broadcasted_iota(jnp.int32, (ne, tm), 0).astype(F32)
    grow = lax.broadcasted_iota(jnp.int32, (ne, tm), 0) // per

    gparts = []
    sub = lax.broadcasted_iota(jnp.int32, (per, tm), 0).astype(F32)
    for gi in range(N_EXPERT_GROUPS):
        xg = biased[gi * per:(gi + 1) * per, :]
        m1 = _col_max(xg)
        f1 = jnp.min(jnp.where(xg == m1, sub, float(per)), axis=0, keepdims=True)
        m2 = _col_max(jnp.where(sub == f1, -jnp.inf, xg))
        gparts.append(jnp.broadcast_to(m1 + m2, (per, tm)))
    gscore = jnp.concatenate(gparts, axis=0)

    kparts = []
    for gi in range(N_EXPERT_GROUPS):
        gs = gscore[gi * per:gi * per + 1, :]
        beats = (gscore > gs) | ((gscore == gs) & (grow < gi))
        nbeat = jnp.sum(beats.astype(F32), axis=0, keepdims=True)
        kparts.append(jnp.broadcast_to(nbeat < float(TOPK_EXPERT_GROUPS * per), (per, tm)))
    gkeep = jnp.concatenate(kparts, axis=0)

    work = jnp.where(gkeep, biased, -jnp.inf)
    picks = []
    chosen = jnp.zeros((ne, tm), F32)
    for _ in range(EXPERT_TOP_K):
        mx = _col_max(work)
        first = jnp.min(jnp.where(work == mx, erow, float(ne)), axis=0, keepdims=True)
        pick = erow == first
        picks.append(pick)
        chosen = jnp.where(pick, 1.0, chosen)
        work = jnp.where(pick, -jnp.inf, work)

    tri = (lax.broadcasted_iota(jnp.int32, (tm, tm), 0) < lax.broadcasted_iota(jnp.int32, (tm, tm), 1))
    before = jnp.dot(chosen.astype(BF16), tri.astype(BF16), preferred_element_type=F32)
    pos = before + carry[:, 0:1]
    new_carry = carry[:, 0:1] + jnp.sum(chosen, axis=1, keepdims=True)
    carry[...] = jnp.broadcast_to(new_carry, carry.shape)
    cnt_ref[...] = carry[...]

    krow = lax.broadcasted_iota(jnp.int32, (EXPERT_TOP_K, tm), 0)
    idx_o = jnp.zeros((EXPERT_TOP_K, tm), F32)
    wt_o = jnp.zeros((EXPERT_TOP_K, tm), F32)
    rk_o = jnp.zeros((EXPERT_TOP_K, tm), F32)
    for k, pick in enumerate(picks):
        sel = lambda v: jnp.sum(jnp.where(pick, v, 0.0), axis=0, keepdims=True)
        idx_o = jnp.where(krow == k, sel(erow), idx_o)
        wt_o = jnp.where(krow == k, sel(scores), wt_o)
        rk_o = jnp.where(krow == k, sel(pos), rk_o)
    wsum = jnp.sum(wt_o, axis=0, keepdims=True)
    idx_ref[...] = idx_o.astype(jnp.int32)
    wt_ref[...] = (ROUTED_SCALE * wt_o) / wsum
    rank_ref[...] = rk_o.astype(jnp.int32)


def _router(h2, router_w, router_bias):
    t, d = h2.shape
    ne = router_w.shape[1]
    tm = min(TM_ROUTE, t)
    k_spec = pl.BlockSpec((EXPERT_TOP_K, tm), lambda i: (0, i))
    return pl.pallas_call(
        _router_kernel,
        grid=(t // tm,),
        in_specs=[pl.BlockSpec((tm, d), lambda i: (i, 0)),
                  pl.BlockSpec((ne, d), lambda i: (0, 0)),
                  pl.BlockSpec((ne, 1), lambda i: (0, 0))],
        out_specs=[k_spec, k_spec, k_spec, pl.BlockSpec((ne, LANE), lambda i: (0, 0))],
        out_shape=[jax.ShapeDtypeStruct((EXPERT_TOP_K, t), jnp.int32),
                   jax.ShapeDtypeStruct((EXPERT_TOP_K, t), F32),
                   jax.ShapeDtypeStruct((EXPERT_TOP_K, t), jnp.int32),
                   jax.ShapeDtypeStruct((ne, LANE), F32)],
        scratch_shapes=[pltpu.VMEM((ne, LANE), F32)],
        compiler_params=_cparams(("arbitrary",)),
        name="moe_router",
    )(h2, router_w.T.astype(BF16), router_bias.reshape(ne, 1).astype(F32))


def _row_copy(src_ref, src_row, dst_ref, dst_row, sem):
    return pltpu.make_async_copy(src_ref.at[pl.ds(src_row, 1)], dst_ref.at[pl.ds(dst_row, 1)], sem)


def _pack_bf16_pairs(x):
    half = x.shape[1] // 2
    bits = lax.bitcast_convert_type(x.astype(BF16).astype(F32), jnp.uint32)
    return bits[:, :half] | (bits[:, half:] >> 16)


def _unpack_bf16_pairs(w):
    hi = lax.bitcast_convert_type(w & jnp.uint32(0xFFFF0000), F32)
    lo = lax.bitcast_convert_type(w << 16, F32)
    return hi, lo


def _row_tiles(words):
    return (words // LANE, LANE)


def _dispatch_kernel(start_ref, cnt_ref, dest_ref, h_ref, xs_ref, hp, zblk, sem, zsem):
    i = pl.program_id(0)
    n_steps = pl.num_programs(0)
    slot = i % 2
    tm = h_ref.shape[0]
    ne = start_ref.shape[0] - 1
    rb = zblk.shape[0]
    n_blocks = xs_ref.shape[0] // rb
    hp[slot] = _pack_bf16_pairs(h_ref[...]).reshape(hp.shape[1:])

    def pad_rows(e, fn):
        lo = start_ref[e] + cnt_ref[e]
        lax.fori_loop(lo, start_ref[e + 1], lambda r, c: fn(_row_copy(zblk, 0, xs_ref, r, zsem)) or c, 0)

    def tail_blocks(fn):
        def body(b, c):
            fn(pltpu.make_async_copy(zblk, xs_ref.at[pl.ds(pl.multiple_of(b * rb, rb), rb)], zsem))
            return c
        lax.fori_loop(start_ref[ne] // rb, n_blocks, body, 0)

    @pl.when(i == 0)
    def _():
        zblk[...] = jnp.zeros_like(zblk)
        lax.fori_loop(0, ne, lambda e, c: pad_rows(e, lambda cp: cp.start()) or c, 0)
        tail_blocks(lambda cp: cp.start())
        lax.fori_loop(0, ne, lambda e, c: pad_rows(e, lambda cp: cp.wait()) or c, 0)
        tail_blocks(lambda cp: cp.wait())

    def token_rows(s, fn):
        def body(r, c):
            for k in range(EXPERT_TOP_K):
                fn(_row_copy(hp.at[s], r, xs_ref, dest_ref[k, r], sem.at[s]), k)
            return c
        lax.fori_loop(0, tm, body, 0)

    token_rows(slot, lambda cp, k: cp.start(priority=k % DMA_PRIORITIES))

    @pl.when(i > 0)
    def _():
        token_rows(1 - slot, lambda cp, k: cp.wait())

    @pl.when(i == n_steps - 1)
    def _():
        token_rows(slot, lambda cp, k: cp.wait())


def _dispatch(starts, cnt, dest, h2, n_rows):
    t, d = h2.shape
    tm = min(TM_DISP, t)
    return pl.pallas_call(
        _dispatch_kernel,
        grid_spec=pltpu.PrefetchScalarGridSpec(
            num_scalar_prefetch=2,
            grid=(t // tm,),
            in_specs=[pl.BlockSpec((EXPERT_TOP_K, tm), lambda i, *_: (0, i), memory_space=pltpu.SMEM),
                      pl.BlockSpec((tm, d), lambda i, *_: (i, 0))],
            out_specs=pl.BlockSpec(memory_space=pl.ANY),
            scratch_shapes=[pltpu.VMEM((2, tm) + _row_tiles(d // 2), jnp.uint32),
                            pltpu.VMEM((min(ROW_BLOCK, n_rows),) + _row_tiles(d // 2), jnp.uint32),
                            pltpu.SemaphoreType.DMA((2,)), pltpu.SemaphoreType.DMA(())]),
        out_shape=jax.ShapeDtypeStruct((n_rows,) + _row_tiles(d // 2), jnp.uint32),
        compiler_params=_cparams(("arbitrary",)),
        name="moe_dispatch",
    )(starts, cnt, dest, h2)


def _mlp_kernel(be_ref, nb_ref, x_ref, wg_ref, wu_ref, wd_ref, y_ref, wg_s, wu_s, wd_s, *, packed):
    i = pl.program_id(0)

    @pl.when((i == 0) | (be_ref[i] != be_ref[jnp.maximum(i - 1, 0)]))
    def _():
        wg_s[...] = wg_ref[...].astype(BF16)
        wu_s[...] = wu_ref[...].astype(BF16)
        wd_s[...] = wd_ref[...].astype(BF16)

    @pl.when(i < nb_ref[0])
    def _():
        if packed:
            xw = x_ref[...]
            xw = xw.reshape(xw.shape[0], xw.shape[1] * xw.shape[2])
            x = jnp.concatenate(_unpack_bf16_pairs(xw), axis=1).astype(BF16)
        else:
            x = x_ref[...].astype(BF16)
        gt = jnp.dot(x, wg_s[...], preferred_element_type=F32)
        up = jnp.dot(x, wu_s[...], preferred_element_type=F32)
        hb = ((gt * _sigmoid(gt)) * up).astype(BF16)
        y = jnp.dot(hb, wd_s[...], preferred_element_type=F32)
        y_ref[...] = _pack_bf16_pairs(y).reshape(y_ref.shape) if packed else y

    @pl.when(i >= nb_ref[0])
    def _():
        y_ref[...] = jnp.zeros_like(y_ref)


def _grouped_mlp(block_expert, n_used, xs, layer, w_gate, w_up, w_down, packed, name):
    n_rows, row_shape = xs.shape[0], xs.shape[1:]
    zeros = (0,) * len(row_shape)
    d, hid = w_gate.shape[2], w_gate.shape[3]
    rb = min(ROW_BLOCK, n_rows)
    w_spec = lambda shape: pl.BlockSpec((None, None) + shape, lambda i, be, nb: (layer, be[i], 0, 0))
    return pl.pallas_call(
        functools.partial(_mlp_kernel, packed=packed),
        grid_spec=pltpu.PrefetchScalarGridSpec(
            num_scalar_prefetch=2,
            grid=(n_rows // rb,),
            in_specs=[pl.BlockSpec((rb,) + row_shape, lambda i, be, nb: (jnp.minimum(i, nb[0] - 1),) + zeros),
                      w_spec((d, hid)), w_spec((d, hid)), w_spec((hid, d))],
            out_specs=pl.BlockSpec((rb,) + row_shape, lambda i, be, nb: (i,) + zeros),
            scratch_shapes=[pltpu.VMEM((d, hid), BF16), pltpu.VMEM((d, hid), BF16), pltpu.VMEM((hid, d), BF16)]),
        out_shape=jax.ShapeDtypeStruct(xs.shape, xs.dtype if packed else F32),
        compiler_params=_cparams(("arbitrary",)),
        name=name,
    )(block_expert, n_used, xs, w_gate, w_up, w_down)


def _combine_kernel(dest_ref, dnext_ref, wt_ref, ysh_ref, x_ref, g2_ref, fn_ref, ys_ref, o_ref, ybuf, sem, *,
                    final):
    i = pl.program_id(0)
    n_steps = pl.num_programs(0)
    slot = i % 2
    tm, d = x_ref.shape

    def token_rows(rows_ref, s, fn):
        def body(r, c):
            for k in range(EXPERT_TOP_K):
                fn(_row_copy(ys_ref, rows_ref[k, r], ybuf.at[s].at[k], r, sem.at[s]), k)
            return c
        lax.fori_loop(0, tm, body, 0)

    start = lambda cp, k: cp.start(priority=k % DMA_PRIORITIES)

    @pl.when(i == 0)
    def _():
        token_rows(dest_ref, 0, start)

    @pl.when(i + 1 < n_steps)
    def _():
        token_rows(dnext_ref, 1 - slot, start)

    token_rows(dest_ref, slot, lambda cp, k: cp.wait())

    wt = wt_ref[...]
    acc_hi = jnp.zeros((tm, d // 2), F32)
    acc_lo = jnp.zeros((tm, d // 2), F32)
    for k in range(EXPERT_TOP_K):
        hi, lo = _unpack_bf16_pairs(ybuf[slot, k].reshape(tm, d // 2))
        acc_hi = acc_hi + wt[:, k:k + 1] * hi
        acc_lo = acc_lo + wt[:, k:k + 1] * lo
    acc = ysh_ref[...] + jnp.concatenate([acc_hi, acc_lo], axis=1)
    xn = x_ref[...] + g2_ref[...] * acc
    if final:
        ms = jnp.mean(xn * xn, axis=-1, keepdims=True)
        xn = (xn * lax.rsqrt(ms + EPS)) * fn_ref[...]
    o_ref[...] = xn


def _combine(dest, wts_t, y_sorted, y_shared, x2d, modr, mod_base, seq, final_norm, final):
    t, d = x2d.shape
    tm = min(TM_COMB, seq)
    row_spec = pl.BlockSpec((tm, d), lambda i: (i, 0))
    n_steps = t // tm
    return pl.pallas_call(
        functools.partial(_combine_kernel, final=final),
        grid=(n_steps,),
        in_specs=[pl.BlockSpec((EXPERT_TOP_K, tm), lambda i: (0, i), memory_space=pltpu.SMEM),
                  pl.BlockSpec((EXPERT_TOP_K, tm), lambda i: (0, jnp.minimum(i + 1, n_steps - 1)),
                               memory_space=pltpu.SMEM),
                  pl.BlockSpec((tm, EXPERT_TOP_K), lambda i: (i, 0)),
                  row_spec, row_spec,
                  pl.BlockSpec((None, 1, d), lambda i: (mod_base + ((i * tm) // seq) * 6 + 5, 0, 0)),
                  pl.BlockSpec((1, d), lambda i: (0, 0)),
                  pl.BlockSpec(memory_space=pl.ANY)],
        out_specs=row_spec,
        out_shape=jax.ShapeDtypeStruct((t, d), F32),
        scratch_shapes=[pltpu.VMEM((2, EXPERT_TOP_K, tm) + _row_tiles(d // 2), jnp.uint32),
                        pltpu.SemaphoreType.DMA((2,))],
        compiler_params=_cparams(("arbitrary",)),
        name="moe_combine",
    )(dest, dest, wts_t, y_shared, x2d, modr, final_norm.reshape(1, d), y_sorted)


def _layout(d_model):
    r, a, kvw = RNN_WIDTH, N_HEADS * HEAD_DIM, N_KV_GROUPS * HEAD_DIM
    off = {}
    off["ma"], off["mb"] = 0, d_model
    off["u"] = 2 * d_model
    off["g"] = off["u"] + r
    off["kc"] = off["g"] + r
    off["vc"] = off["kc"] + kvw
    off["nf"] = off["vc"] + kvw
    off["q"] = 0
    off["ks"], off["vs"], off["kw"], off["vw"] = a, a + kvw, a + 2 * kvw, a + 3 * kvw
    off["gn"] = a + 4 * kvw
    off["nb"] = off["gn"] + TN_IN
    assert off["nf"] % TN_IN == 0 and off["gn"] % TN_IN == 0
    return off


def _pack_w_in(w_in_l, d_model):
    r, a, kvw = RNN_WIDTH, N_HEADS * HEAD_DIM, N_KV_GROUPS * HEAD_DIM
    off = _layout(d_model)
    s_u, s_g, s_q = 0, r, 2 * r
    s_kv = s_q + a
    s_gn = s_kv + 6 * kvw
    s_ma = s_gn + 3 * N_HEADS
    s_mb = s_ma + d_model
    cols = [w_in_l[:, s_ma:s_mb], w_in_l[:, s_mb:s_mb + d_model], w_in_l[:, s_u:s_g], w_in_l[:, s_g:s_q],
            w_in_l[:, s_kv:s_kv + 2 * kvw],
            w_in_l[:, s_q:s_kv], w_in_l[:, s_kv + 2 * kvw:s_gn], w_in_l[:, s_gn:s_ma]]
    w = jnp.concatenate(cols, axis=1)
    return jnp.pad(w, ((0, 0), (0, off["nf"] + off["nb"] - w.shape[1]))).astype(BF16)


def _moe_plan(idx, rank, counts, n_tok):
    cnt = counts[:, 0].astype(jnp.int32)
    padded = (cnt + ROW_BLOCK - 1) // ROW_BLOCK * ROW_BLOCK
    ends = jnp.cumsum(padded)
    starts = jnp.concatenate([jnp.zeros((1,), jnp.int32), ends]).astype(jnp.int32)
    onehot = idx[:, :, None] == jnp.arange(N_EXPERTS, dtype=jnp.int32)[None, None, :]
    dest = rank + jnp.sum(jnp.where(onehot, starts[None, None, :N_EXPERTS], 0), axis=2)
    n_blocks = (n_tok * EXPERT_TOP_K) // ROW_BLOCK + N_EXPERTS
    blk_start = jnp.arange(n_blocks, dtype=jnp.int32) * ROW_BLOCK
    owner = jnp.sum((ends[None, :] <= blk_start[:, None]).astype(jnp.int32), axis=1)
    block_expert = jnp.minimum(owner, N_EXPERTS - 1).astype(jnp.int32)
    n_used = (ends[-1] // ROW_BLOCK).astype(jnp.int32).reshape(1)
    return starts, cnt, dest.astype(jnp.int32), block_expert, n_used, n_blocks * ROW_BLOCK


def kernel(x, c, rel_bias, final_norm, ada_w, ada_b, norm_mix, norm_ffn, w_in, conv_w, conv_b, lru_wa, lru_ba, lru_wx, lru_bx, lru_lambda, cmp_pe_k, cmp_w1_k, cmp_w2_k, cmp_pe_v, cmp_w1_v, cmp_w2_v, w_up_rnn, w_up_att, w_out, router_w, router_bias, exp_w_gate, exp_w_up, exp_w_down, sh_w_gate, sh_w_up, sh_w_down):
    batch, seq, d = x.shape
    n_tok = batch * seq
    depth = ada_w.shape[0]
    g_, dh = N_KV_GROUPS, HEAD_DIM
    a_w, kvw, r = N_HEADS * HEAD_DIM, N_KV_GROUPS * HEAD_DIM, RNN_WIDTH
    off = _layout(d)
    assert seq % CMP_STRIDE == 0 and CMP_BLOCK == 2 * CMP_STRIDE

    mod = _adaln_mod(c, ada_w, ada_b)
    modr = mod.reshape(depth * batch * 6, 1, d)
    tables = _bias_tables(rel_bias, seq)
    nc = tables[2].shape[1]
    n_chunk = seq // CMP_STRIDE

    x2d = x.reshape(n_tok, d)
    for l in range(depth):
        mod_base = l * batch * 6
        zf, zb, zg = _inproj(x2d, norm_mix[l], modr, mod_base, seq, _pack_w_in(w_in[l], d), off["nf"])
        y_rnn = _rglru(zf, batch, seq, off["u"] // r, off["g"] // r, conv_w[l], conv_b[l],
                       lru_wa[l], lru_ba[l], lru_wx[l], lru_bx[l], lru_lambda[l])

        zb3 = zb.reshape(batch, seq, -1)
        zcol = lambda z3, name, w: z3[:, :, off[name]:off[name] + w]

        def unfold(v):
            ch = jnp.transpose(v.reshape(batch, n_chunk, CMP_STRIDE, g_, dh), (0, 3, 1, 2, 4))
            ch = ch.reshape(batch * g_, n_chunk, CMP_STRIDE * dh)
            blocks = jnp.concatenate([ch[:, :-1], ch[:, 1:]], axis=2)
            return jnp.pad(blocks, ((0, 0), (0, nc - (n_chunk - 1)), (0, 0)))

        zf3 = zf.reshape(batch, seq, -1)
        kc, vc = _compress(unfold(zcol(zf3, "kc", kvw)), unfold(zcol(zf3, "vc", kvw)),
                           cmp_pe_k[l], cmp_w1_k[l], cmp_w2_k[l], cmp_pe_v[l], cmp_w1_v[l], cmp_w2_v[l])
        kc = jnp.transpose(kc.reshape(batch, g_, nc, dh), (0, 2, 1, 3)).reshape(batch, nc, kvw)
        vcT = jnp.swapaxes(vc.reshape(batch, g_, nc, dh), 2, 3).reshape(batch, kvw, nc)
        qT = jnp.swapaxes(zcol(zb3, "q", a_w), 1, 2)
        vsT = jnp.swapaxes(zcol(zb3, "vs", kvw), 1, 2)
        vwT = jnp.swapaxes(zcol(zb3, "vw", kvw), 1, 2)
        gT = jnp.swapaxes(zg.reshape(batch, seq, LANE)[:, :, :3 * N_HEADS], 1, 2)
        y_att = _attention(qT, kc, vcT, zb3, off["ks"] // kvw, off["kw"] // kvw, vsT, vwT, gT, tables,
                           batch, seq).reshape(n_tok, a_w)

        merged = _merge(y_rnn, y_att, w_up_rnn[l].astype(BF16), w_up_att[l].astype(BF16), zf,
                        off["ma"] // d, off["mb"] // d)
        x1, h2 = _outproj(merged, w_out[l].astype(BF16), x2d, modr, mod_base, seq, norm_ffn[l])

        idx, wts, rank, counts = _router(h2, router_w[l], router_bias[l])
        starts, cnt, dest, block_expert, n_used, n_rows = _moe_plan(idx, rank, counts, n_tok)
        xs = _dispatch(starts, cnt, dest, h2, n_rows)
        y_sorted = _grouped_mlp(block_expert, n_used, xs, l, exp_w_gate, exp_w_up, exp_w_down, True, "moe_experts")
        sh_blocks = n_tok // min(ROW_BLOCK, n_tok)
        y_shared = _grouped_mlp(jnp.zeros((sh_blocks,), jnp.int32), jnp.full((1,), sh_blocks, jnp.int32), h2, l,
                                sh_w_gate[:, None], sh_w_up[:, None], sh_w_down[:, None], False, "moe_shared")
        x2d = _combine(dest, wts.T, y_sorted, y_shared, x1, modr, mod_base, seq, final_norm, l == depth - 1)
    return x2d.reshape(batch, seq, d)
```
